```python
import math
import jax, jax.numpy as jnp
from jax import lax
import numpy as np


D_MODEL = 1024
BATCH = 16
SEQ = 4096
DEPTH = 1

ATTN_HEADS = 8
QK_NOPE_DIM = 64
QK_ROPE_DIM = 32
V_HEAD_DIM = 64
Q_LORA_RANK = 256
KV_LORA_RANK = 128
ATTN_WIDTH = ATTN_HEADS * V_HEAD_DIM
ROPE_THETA = 10000.0
Q_BLOCK = 128

HY_WIDTH = D_MODEL - ATTN_WIDTH
HY_ORDER = 2
HY_GROUPS = 8
HY_SHORT = 3
HY_EMB_DIM = 33
HY_FILTER_HIDDEN = 64
HY_FAST_DECAY = 0.3
HY_SLOW_DECAY = 1.5
HY_TARGET = 1e-2

OFF_CQ = Q_LORA_RANK
OFF_CKV = OFF_CQ + KV_LORA_RANK
OFF_KR = OFF_CKV + QK_ROPE_DIM
IN_WIDTH = OFF_KR + (HY_ORDER + 1) * HY_WIDTH

PEER_HEADS = 8
PEER_NKEYS = 128
PEER_N_EXPERTS = PEER_NKEYS * PEER_NKEYS
PEER_DK = 128
PEER_TOPK = 16
PEER_CHUNK = 128

ALPHA = (2 * DEPTH) ** 0.25
BETA = (8 * DEPTH) ** -0.25
LN_EPS = 1e-5
RMS_EPS = 1e-6

kernel_name = "hymba_mla_hyena_peer_deepnorm_encoder"


def layer_norm(x, g, b):
    xf = x.astype(jnp.float32)
    mu = jnp.mean(xf, axis=-1, keepdims=True)
    var = jnp.mean(jnp.square(xf - mu), axis=-1, keepdims=True)
    return ((xf - mu) * lax.rsqrt(var + LN_EPS) * g.astype(jnp.float32) + b.astype(jnp.float32)).astype(x.dtype)


def rms_norm(x, g):
    xf = x.astype(jnp.float32)
    y = xf * lax.rsqrt(jnp.mean(jnp.square(xf), axis=-1, keepdims=True) + RMS_EPS)
    return (y * g.astype(jnp.float32)).astype(x.dtype)


def rotary_tables(L, dim):
    inv = 1.0 / (ROPE_THETA ** (jnp.arange(0, dim, 2, dtype=jnp.float32) / dim))
    ang = jnp.arange(L, dtype=jnp.float32)[:, None] * inv[None, :]
    return jnp.cos(ang), jnp.sin(ang)


def apply_rope(x, cos, sin):
    half = x.shape[-1] // 2
    xf = x.astype(jnp.float32)
    x1, x2 = xf[..., :half], xf[..., half:]
    return jnp.concatenate([x1 * cos - x2 * sin, x1 * sin + x2 * cos], axis=-1).astype(x.dtype)


def mla_attention(c_q, c_kv, k_rope, q_norm_g, w_uq, kv_norm_g, w_ukv):
    B, S, _ = c_q.shape
    q = (rms_norm(c_q, q_norm_g) @ w_uq).reshape(B, S, ATTN_HEADS, QK_NOPE_DIM + QK_ROPE_DIM)
    kv = (rms_norm(c_kv, kv_norm_g) @ w_ukv).reshape(B, S, ATTN_HEADS, QK_NOPE_DIM + V_HEAD_DIM)
    q_nope, q_pe = q[..., :QK_NOPE_DIM], q[..., QK_NOPE_DIM:]
    k_nope, v = kv[..., :QK_NOPE_DIM], kv[..., QK_NOPE_DIM:]
    cos, sin = rotary_tables(S, QK_ROPE_DIM)
    q_pe = apply_rope(q_pe, cos[:, None, :], sin[:, None, :])
    k_pe = apply_rope(k_rope, cos, sin)
    scale = (QK_NOPE_DIM + QK_ROPE_DIM) ** -0.5
    nb = S // Q_BLOCK
    qn_blocks = q_nope.reshape(B, nb, Q_BLOCK, ATTN_HEADS, QK_NOPE_DIM).transpose(1, 0, 2, 3, 4)
    qr_blocks = q_pe.reshape(B, nb, Q_BLOCK, ATTN_HEADS, QK_ROPE_DIM).transpose(1, 0, 2, 3, 4)

    def one_block(qs):
        qn, qr = qs
        s = (jnp.einsum('bqhd,bkhd->bhqk', qn, k_nope, preferred_element_type=jnp.float32)
             + jnp.einsum('bqhr,bkr->bhqk', qr, k_pe, preferred_element_type=jnp.float32)) * scale
        p = jax.nn.softmax(s, axis=-1)
        return jnp.einsum('bhqk,bkhd->bqhd', p.astype(v.dtype), v)

    o = lax.map(one_block, (qn_blocks, qr_blocks))
    return o.transpose(1, 0, 2, 3, 4).reshape(B, S, ATTN_HEADS, V_HEAD_DIM)


def short_conv(u, w, b):
    L = u.shape[1]
    pad = HY_SHORT // 2
    up = jnp.pad(u, ((0, 0), (pad, pad), (0, 0)))
    y = b
    for i in range(HY_SHORT):
        y = y + up[:, i:i + L] * w[i]
    return y


def hyena_position_features(L):
    t = jnp.linspace(0.0, 1.0, L, dtype=jnp.float32)[:, None]
    bands = (HY_EMB_DIM - 1) // 2
    w = 2.0 * math.pi * jnp.arange(L, dtype=jnp.float32) / L
    f = jnp.linspace(1e-4, bands - 1, bands, dtype=jnp.float32)
    ang = w[:, None] * f[None, :]
    return t, jnp.concatenate([t, jnp.cos(ang), -jnp.sin(ang)], axis=-1)


def hyena_filters(L, w1, b1, fr1, w2, b2, fr2, w3):
    f32 = jnp.float32
    t, z = hyena_position_features(L)
    h = jnp.sin(fr1.astype(f32) * (z @ w1.astype(f32) + b1.astype(f32)))
    h = jnp.sin(fr2.astype(f32) * (h @ w2.astype(f32) + b2.astype(f32)))
    h = (h @ w3.astype(f32)).reshape(L, HY_ORDER, 2, HY_WIDTH)
    max_decay = math.log(HY_TARGET) / HY_FAST_DECAY
    min_decay = math.log(HY_TARGET) / HY_SLOW_DECAY
    deltas = jnp.linspace(min_decay, max_decay, HY_WIDTH, dtype=f32)
    decay = jnp.exp(-t * jnp.abs(deltas)[None, :])
    h = h * decay[:, None, None, :]
    return h / (jnp.sum(jnp.abs(h), axis=0, keepdims=True) + 1e-6)


def hyena_mixer(u, short_w, short_b, w1, b1, fr1, w2, b2, fr2, w3, hy_bias):
    B, L, _ = u.shape
    n = 2 * L
    u = short_conv(u, short_w, short_b)
    v = u[..., :HY_WIDTH]
    gates = (u[..., HY_WIDTH:2 * HY_WIDTH], u[..., 2 * HY_WIDTH:])
    h = hyena_filters(L, w1, b1, fr1, w2, b2, fr2, w3)
    hf = jnp.fft.rfft(h, n=n, axis=0)
    h_bidir = hf[:, :, 0] + jnp.conj(hf[:, :, 1])
    z = v
    for o in range(HY_ORDER):
        zf = jnp.fft.rfft(z.astype(jnp.float32), n=n, axis=1)
        y = jnp.fft.irfft(zf * h_bidir[None, :, o], n=n, axis=1)[:, :L]
        z = gates[o] * (y.astype(z.dtype) + hy_bias[o] * z)
    return z


def peer_ffn(x, w_q, sub_keys, u_tab, v_tab):
    B, S, D = x.shape
    T = B * S
    xt = x.reshape(T // PEER_CHUNK, PEER_CHUNK, D)

    def chunk(xc):
        C = xc.shape[0]
        q = (xc @ w_q).reshape(C, PEER_HEADS, 2, PEER_DK // 2)
        s = jnp.einsum('chpd,hpnd->chpn', q, sub_keys, preferred_element_type=jnp.float32)
        sv, si = lax.top_k(s, PEER_TOPK)
        cand = sv[:, :, 0, :, None] + sv[:, :, 1, None, :]
        cidx = si[:, :, 0, :, None] * PEER_NKEYS + si[:, :, 1, None, :]
        cand = cand.reshape(C, PEER_HEADS, PEER_TOPK * PEER_TOPK)
        cidx = cidx.reshape(C, PEER_HEADS, PEER_TOPK * PEER_TOPK)
        best, pos = lax.top_k(cand, PEER_TOPK)
        eidx = jnp.take_along_axis(cidx, pos, axis=-1)
        g = jax.nn.softmax(best, axis=-1)
        u = u_tab[eidx]
        a = jax.nn.gelu(jnp.einsum('chkd,cd->chk', u, xc, preferred_element_type=jnp.float32), approximate=False)
        vv = v_tab[eidx]
        return jnp.einsum('chk,chkd->cd', (g * a).astype(vv.dtype), vv)

    return lax.map(chunk, xt).reshape(B, S, D)


def setup_inputs(seed: int = 0) -> dict:
    key = jax.random.key(seed)
    ks = jax.random.split(key, 32)
    f32 = jnp.float32

    def nrm(k, shape, scale):
        return jax.random.normal(k, shape, f32) * scale

    def gain(k, shape):
        return 1.0 + 0.02 * jax.random.normal(k, shape, f32)

    L_ = DEPTH
    return {
        "x": jax.random.normal(ks[0], (BATCH, SEQ, D_MODEL), f32),
        "emb_ln_g": gain(ks[1], (D_MODEL,)),
        "emb_ln_b": nrm(ks[2], (D_MODEL,), 0.02),
        "w_in": nrm(ks[3], (L_, D_MODEL, IN_WIDTH), D_MODEL ** -0.5),
        "q_norm_g": gain(ks[4], (L_, Q_LORA_RANK)),
        "w_uq": nrm(ks[5], (L_, Q_LORA_RANK, ATTN_HEADS * (QK_NOPE_DIM + QK_ROPE_DIM)), Q_LORA_RANK ** -0.5),
        "kv_norm_g": gain(ks[6], (L_, KV_LORA_RANK)),
        "w_ukv": nrm(ks[7], (L_, KV_LORA_RANK, ATTN_HEADS * (QK_NOPE_DIM + V_HEAD_DIM)), KV_LORA_RANK ** -0.5),
        "hy_short_w": nrm(ks[8], (L_, HY_SHORT, (HY_ORDER + 1) * HY_WIDTH), HY_SHORT ** -0.5),
        "hy_short_b": nrm(ks[9], (L_, (HY_ORDER + 1) * HY_WIDTH), 0.02),
        "hy_filt_w1": nrm(ks[10], (L_, HY_EMB_DIM, HY_FILTER_HIDDEN), HY_EMB_DIM ** -0.5),
        "hy_filt_b1": nrm(ks[11], (L_, HY_FILTER_HIDDEN), 0.02),
        "hy_filt_freq1": gain(ks[12], (L_, HY_FILTER_HIDDEN)),
        "hy_filt_w2": nrm(ks[13], (L_, HY_FILTER_HIDDEN, HY_FILTER_HIDDEN), HY_FILTER_HIDDEN ** -0.5),
        "hy_filt_b2": nrm(ks[14], (L_, HY_FILTER_HIDDEN), 0.02),
        "hy_filt_freq2": gain(ks[15], (L_, HY_FILTER_HIDDEN)),
        "hy_filt_w3": nrm(ks[16], (L_, HY_FILTER_HIDDEN, HY_ORDER * 2 * HY_WIDTH), HY_FILTER_HIDDEN ** -0.5),
        "hy_bias": nrm(ks[17], (L_, HY_ORDER, HY_WIDTH), 0.5),
        "attn_out_g": gain(ks[18], (L_, ATTN_WIDTH)),
        "hy_out_g": gain(ks[19], (L_, HY_WIDTH)),
        "w_o": nrm(ks[20], (L_, ATTN_WIDTH + HY_WIDTH, D_MODEL), BETA * (ATTN_WIDTH + HY_WIDTH) ** -0.5),
        "ln_mix_g": gain(ks[21], (L_, D_MODEL)),
        "ln_mix_b": nrm(ks[22], (L_, D_MODEL), 0.02),
        "peer_wq": nrm(ks[23], (L_, D_MODEL, PEER_HEADS * PEER_DK), D_MODEL ** -0.5),
        "peer_sub_keys": nrm(ks[24], (L_, PEER_HEADS, 2, PEER_NKEYS, PEER_DK // 2), (PEER_DK // 2) ** -0.5),
        "peer_u": nrm(ks[25], (L_, PEER_N_EXPERTS, D_MODEL), D_MODEL ** -0.5),
        "peer_v": nrm(ks[26], (L_, PEER_N_EXPERTS, D_MODEL), BETA * PEER_HEADS ** -0.5),
        "ln_ffn_g": gain(ks[27], (L_, D_MODEL)),
        "ln_ffn_b": nrm(ks[28], (L_, D_MODEL), 0.02),
    }


def reference(x, emb_ln_g, emb_ln_b, w_in, q_norm_g, w_uq, kv_norm_g, w_ukv,
              hy_short_w, hy_short_b, hy_filt_w1, hy_filt_b1, hy_filt_freq1,
              hy_filt_w2, hy_filt_b2, hy_filt_freq2, hy_filt_w3, hy_bias,
              attn_out_g, hy_out_g, w_o, ln_mix_g, ln_mix_b,
              peer_wq, peer_sub_keys, peer_u, peer_v, ln_ffn_g, ln_ffn_b):
    B, S, _ = x.shape
    h = layer_norm(x, emb_ln_g, emb_ln_b)
    for l in range(DEPTH):
        proj = h @ w_in[l]
        c_q = proj[..., :OFF_CQ]
        c_kv = proj[..., OFF_CQ:OFF_CKV]
        k_rope = proj[..., OFF_CKV:OFF_KR]
        hy_in = proj[..., OFF_KR:]
        a = mla_attention(c_q, c_kv, k_rope, q_norm_g[l], w_uq[l], kv_norm_g[l], w_ukv[l])
        a = rms_norm(a, attn_out_g[l].reshape(ATTN_HEADS, V_HEAD_DIM)).reshape(B, S, ATTN_WIDTH)
        y = hyena_mixer(hy_in, hy_short_w[l], hy_short_b[l], hy_filt_w1[l], hy_filt_b1[l], hy_filt_freq1[l],
                        hy_filt_w2[l], hy_filt_b2[l], hy_filt_freq2[l], hy_filt_w3[l], hy_bias[l])
        gw = HY_WIDTH // HY_GROUPS
        y = rms_norm(y.reshape(B, S, HY_GROUPS, gw), hy_out_g[l].reshape(HY_GROUPS, gw)).reshape(B, S, HY_WIDTH)
        mix = jnp.concatenate([a, y], axis=-1) @ w_o[l]
        h = layer_norm(ALPHA * h + mix, ln_mix_g[l], ln_mix_b[l])
        f = peer_ffn(h, peer_wq[l], peer_sub_keys[l], peer_u[l], peer_v[l])
        h = layer_norm(ALPHA * h + f, ln_ffn_g[l], ln_ffn_b[l])
    return h
```

```python
import functools
import math

import jax
import jax.numpy as jnp
from jax import lax
from jax.experimental import pallas as pl
from jax.experimental.pallas import tpu as pltpu

F32 = jnp.float32
BF16 = jnp.bfloat16
I32 = jnp.int32

D_MODEL = 1024
ATTN_HEADS = 8
QK_NOPE_DIM = 64
QK_ROPE_DIM = 32
V_HEAD_DIM = 64
Q_LORA_RANK = 256
KV_LORA_RANK = 128
ATTN_WIDTH = ATTN_HEADS * V_HEAD_DIM
ROPE_THETA = 10000.0
HY_WIDTH = D_MODEL - ATTN_WIDTH
HY_ORDER = 2
HY_GROUPS = 8
HY_SHORT = 3
HY_EMB_DIM = 33
HY_FILTER_HIDDEN = 64
HY_FAST_DECAY = 0.3
HY_SLOW_DECAY = 1.5
HY_TARGET = 1e-2
OFF_CQ = Q_LORA_RANK
OFF_CKV = OFF_CQ + KV_LORA_RANK
OFF_KR = OFF_CKV + QK_ROPE_DIM
PEER_HEADS = 8
PEER_NKEYS = 128
PEER_DK = 128
PEER_TOPK = 16
DEPTH = 1
ALPHA = (2 * DEPTH) ** 0.25
LN_EPS = 1e-5
RMS_EPS = 1e-6

LANES = 128
HEAD_SLOT = 128
TOEP = 256
N_SEL = PEER_HEADS * PEER_TOPK
W_PITCH = 72
VMEM_LIMIT = 56 * 1024 * 1024


def _cparams(sem):
    return pltpu.CompilerParams(dimension_semantics=sem, vmem_limit_bytes=VMEM_LIMIT)


def _layer_norm(x, g, b):
    mu = jnp.mean(x, axis=-1, keepdims=True)
    xc = x - mu
    var = jnp.mean(xc * xc, axis=-1, keepdims=True)
    return xc * lax.rsqrt(var + LN_EPS) * g + b


def _rms(x, g):
    return x * lax.rsqrt(jnp.mean(x * x, axis=-1, keepdims=True) + RMS_EPS) * g


def _dot(a, b):
    return jnp.dot(a, b, preferred_element_type=F32)


def _dot_nt(a, b):
    return lax.dot_general(a, b, (((1,), (1,)), ((), ())), preferred_element_type=F32)


N_MLA_COLS = Q_LORA_RANK + KV_LORA_RANK + 2 * HEAD_SLOT


def _ln_proj_kernel(x_ref, lg_ref, lb_ref, wall_ref, gq_ref, wq_ref, wqs_ref, gkv_ref, wk_ref, wv_ref,
                    cq_ref, sq_ref, ck_ref, sk_ref, q_ref, k_ref, v_ref, hy_ref):
    h = _layer_norm(x_ref[...], lg_ref[...], lb_ref[...])
    proj = _dot(h.astype(BF16), wall_ref[...])
    c_q = proj[:, :OFF_CQ]
    c_kv = proj[:, OFF_CQ:OFF_CKV]
    kr = proj[:, OFF_CKV:OFF_CKV + HEAD_SLOT]
    kr_sw = proj[:, OFF_CKV + HEAD_SLOT:N_MLA_COLS]
    hy_ref[...] = proj[:, N_MLA_COLS:].astype(BF16)

    nq = _rms(c_q, gq_ref[...]).astype(BF16)
    qa = _dot(nq, wq_ref[...])
    qb = _dot(nq, wqs_ref[...])
    nkv = _rms(c_kv, gkv_ref[...]).astype(BF16)
    kn = _dot(nkv, wk_ref[...])
    v_ref[...] = _dot(nkv, wv_ref[...]).astype(BF16)
    k_pe = kr * ck_ref[...] + kr_sw * sk_ref[...]
    cq, sq = cq_ref[...], sq_ref[...]
    for hd in range(ATTN_HEADS):
        sl = slice(hd * HEAD_SLOT, (hd + 1) * HEAD_SLOT)
        q_ref[:, sl] = (qa[:, sl] * cq + qb[:, sl] * sq).astype(BF16)
        k_ref[:, sl] = (kn[:, sl] + k_pe).astype(BF16)


def _ln_proj(x2, lg, lb, wall, gq, wq, wqs, gkv, wk, wv, cq, sq, ck, sk, seq, tt):
    t = x2.shape[0]
    npos = seq // tt
    full = lambda a: pl.BlockSpec(a.shape, lambda i: (0,) * a.ndim)
    tab = pl.BlockSpec((tt, HEAD_SLOT), lambda i: (i % npos, 0))
    row = lambda w: pl.BlockSpec((tt, w), lambda i: (i, 0))
    hw = wall.shape[1] - N_MLA_COLS
    return pl.pallas_call(
        _ln_proj_kernel,
        grid=(t // tt,),
        in_specs=[row(D_MODEL), full(lg), full(lb), full(wall), full(gq), full(wq), full(wqs), full(gkv),
                  full(wk), full(wv), tab, tab, tab, tab],
        out_specs=[row(ATTN_HEADS * HEAD_SLOT), row(ATTN_HEADS * HEAD_SLOT), row(ATTN_WIDTH), row(hw)],
        out_shape=[jax.ShapeDtypeStruct((t, ATTN_HEADS * HEAD_SLOT), BF16),
                   jax.ShapeDtypeStruct((t, ATTN_HEADS * HEAD_SLOT), BF16),
                   jax.ShapeDtypeStruct((t, ATTN_WIDTH), BF16),
                   jax.ShapeDtypeStruct((t, hw), BF16)],
        compiler_params=_cparams(("parallel",)),
        name="ln_proj",
    )(x2, lg, lb, wall, gq, wq, wqs, gkv, wk, wv, cq, sq, ck, sk)


def _attn_kernel(q_ref, k_ref, v_ref, g_ref, o_ref):
    v = v_ref[...]
    for hh in range(2):
        sl = slice(hh * HEAD_SLOT, (hh + 1) * HEAD_SLOT)
        s = _dot_nt(q_ref[:, sl], k_ref[:, sl])
        m = jnp.max(s, axis=1, keepdims=True)
        p = jnp.exp(s - m)
        l = jnp.sum(p, axis=1, keepdims=True)
        o2 = _dot(p.astype(BF16), v)
        vs = slice(hh * V_HEAD_DIM, (hh + 1) * V_HEAD_DIM)
        o = o2[:, vs] / l
        o_ref[:, vs] = _rms(o, g_ref[:, vs]).astype(BF16)


def _attention(q, k, v, g, batch, seq, tq):
    t = q.shape[0]
    nq = seq // tq
    npair = ATTN_HEADS // 2
    return pl.pallas_call(
        _attn_kernel,
        grid=(batch, npair, nq),
        in_specs=[pl.BlockSpec((tq, 2 * HEAD_SLOT), lambda b, p, i: (b * nq + i, p)),
                  pl.BlockSpec((seq, 2 * HEAD_SLOT), lambda b, p, i: (b, p)),
                  pl.BlockSpec((seq, 2 * V_HEAD_DIM), lambda b, p, i: (b, p)),
                  pl.BlockSpec((1, 2 * V_HEAD_DIM), lambda b, p, i: (0, p))],
        out_specs=pl.BlockSpec((tq, 2 * V_HEAD_DIM), lambda b, p, i: (b * nq + i, p)),
        out_shape=jax.ShapeDtypeStruct((t, ATTN_WIDTH), BF16),
        compiler_params=_cparams(("parallel", "parallel", "parallel")),
        name="attention",
    )(q, k, v, g)


def _short_conv_kernel(prev_ref, x_ref, next_ref, w_ref, b_ref, o_ref):
    j = pl.program_id(1)
    nj = pl.num_programs(1)
    x = x_ref[0].astype(F32)
    rows = x.shape[0]
    before = jnp.where(j > 0, prev_ref[0, 7:8, :].astype(F32), 0.0)
    after = jnp.where(j < nj - 1, next_ref[0, 0:1, :].astype(F32), 0.0)
    rid = lax.broadcasted_iota(I32, x.shape, 0)
    xm = jnp.where(rid == 0, before, pltpu.roll(x, 1, 0))
    xp = jnp.where(rid == rows - 1, after, pltpu.roll(x, rows - 1, 0))
    u = b_ref[...] + xm * w_ref[0:1, :] + x * w_ref[1:2, :] + xp * w_ref[2:3, :]
    o_ref[...] = u.T.astype(BF16)


def _short_conv(hy3, w, b):
    batch, seq, width = hy3.shape
    nj = seq // TOEP
    sub = TOEP // 8
    last8 = seq // 8 - 1
    return pl.pallas_call(
        _short_conv_kernel,
        grid=(batch, nj),
        in_specs=[pl.BlockSpec((1, 8, width), lambda bb, j: (bb, jnp.maximum(j * sub - 1, 0), 0)),
                  pl.BlockSpec((1, TOEP, width), lambda bb, j: (bb, j, 0)),
                  pl.BlockSpec((1, 8, width), lambda bb, j: (bb, jnp.minimum((j + 1) * sub, last8), 0)),
                  pl.BlockSpec(w.shape, lambda bb, j: (0, 0)),
                  pl.BlockSpec(b.shape, lambda bb, j: (0, 0))],
        out_specs=pl.BlockSpec((width, TOEP), lambda bb, j: (0, j * batch + bb)),
        out_shape=jax.ShapeDtypeStruct((width, nj * batch * TOEP), BF16),
        compiler_params=_cparams(("parallel", "parallel")),
        name="short_conv",
    )(hy3, hy3, hy3, w, b)


def _filter_kernel(z_ref, w1_ref, b1_ref, f1_ref, w2_ref, b2_ref, f2_ref, w3_ref, dl_ref, g_ref):
    hp = lax.Precision.HIGHEST
    z = z_ref[...]
    n = z.shape[1]
    half = n // 2
    h1 = jnp.sin(f1_ref[...] * (jnp.dot(w1_ref[...], z, precision=hp, preferred_element_type=F32) + b1_ref[...]))
    h2 = jnp.sin(f2_ref[...] * (jnp.dot(w2_ref[...], h1, precision=hp, preferred_element_type=F32) + b2_ref[...]))
    decay = jnp.exp(-z[0:1, :] * dl_ref[...])
    lane = lax.broadcasted_iota(I32, (1, n), 1)
    masks = (lane >= half, (lane >= 1) & (lane <= half))
    out = None
    for d in range(2):
        hd = jnp.dot(w3_ref[0, d], h2, precision=hp, preferred_element_type=F32) * decay
        hd = jnp.where(masks[d], hd, 0.0)
        hd = hd / (jnp.sum(jnp.abs(hd), axis=1, keepdims=True) + 1e-6)
        out = hd if out is None else out + hd
    g_ref[0] = out


def _filters(zt, w1t, b1, f1, w2t, b2, f2, w3t, dl):
    n = zt.shape[1]
    cb = 128
    full = lambda a: pl.BlockSpec(a.shape, lambda o, c: (0,) * a.ndim)
    return pl.pallas_call(
        _filter_kernel,
        grid=(HY_ORDER, HY_WIDTH // cb),
        in_specs=[full(zt), full(w1t), full(b1), full(f1), full(w2t), full(b2), full(f2),
                  pl.BlockSpec((1, 2, cb, HY_FILTER_HIDDEN), lambda o, c: (o, 0, c, 0)),
                  pl.BlockSpec((cb, 1), lambda o, c: (c, 0))],
        out_specs=pl.BlockSpec((1, cb, n), lambda o, c: (o, c, 0)),
        out_shape=jax.ShapeDtypeStruct((HY_ORDER, HY_WIDTH, n), F32),
        compiler_params=_cparams(("parallel", "parallel")),
        name="filters",
    )(zt, w1t, b1, f1, w2t, b2, f2, w3t, dl)


def _hyena_kernel(bias_ref, v_ref, x1_ref, x2_ref, g_ref, o_ref, acc_ref, *, batch, nblk, cb):
    rows = batch * nblk
    seq = nblk * TOEP
    c0 = pl.program_id(0) * cb
    gates = (x1_ref, x2_ref)

    def per_channel(c, carry):
        zf = v_ref[c].astype(F32)
        for o in range(HY_ORDER):
            zb = zf.astype(BF16)
            acc_ref[...] = jnp.zeros_like(acc_ref)
            for d in range(-(nblk - 1), nblk):
                base = seq + TOEP * d - TOEP
                win = g_ref[o, c, :, pl.ds(base, 2 * TOEP)]
                w = pltpu.roll(jnp.broadcast_to(win, (TOEP, 2 * TOEP)), TOEP, 1, stride=1, stride_axis=0)
                w = w[:, :TOEP].astype(BF16)
                n = rows - batch * abs(d)
                if d >= 0:
                    acc_ref[batch * d:, :] += _dot(zb[:n], w)
                else:
                    acc_ref[:n, :] += _dot(zb[batch * (-d):], w)
            zf = gates[o][c].astype(F32) * (acc_ref[...] + bias_ref[o, c0 + c] * zf)
        o_ref[c] = zf.astype(BF16)
        return carry

    lax.fori_loop(0, cb, per_channel, 0)


def _hyena(bias, ut, g4, batch, nblk, cb):
    rows = batch * nblk
    nc = HY_WIDTH // cb
    kern = functools.partial(_hyena_kernel, batch=batch, nblk=nblk, cb=cb)
    blk = lambda off: pl.BlockSpec((cb, rows, TOEP), lambda i: (off * nc + i, 0, 0))
    return pl.pallas_call(
        kern,
        grid=(nc,),
        in_specs=[pl.BlockSpec(memory_space=pltpu.SMEM), blk(0), blk(1), blk(2),
                  pl.BlockSpec((HY_ORDER, cb, 1, g4.shape[3]), lambda i: (0, i, 0, 0))],
        out_specs=pl.BlockSpec((cb, rows, TOEP), lambda i: (i, 0, 0)),
        out_shape=jax.ShapeDtypeStruct((HY_WIDTH, rows, TOEP), BF16),
        scratch_shapes=[pltpu.VMEM((rows, TOEP), F32)],
        compiler_params=_cparams(("parallel",)),
        name="hyena",
    )(bias, ut, ut, ut, g4)


def _mix_kernel(x_ref, lg_ref, lb_ref, a_ref, yt_ref, hg_ref, wo_ref, mg_ref, mb_ref, h2_ref, h2b_ref, cat_ref):
    y = yt_ref[...].astype(F32).T
    cat_ref[:, :ATTN_WIDTH] = a_ref[...]
    gw = HY_WIDTH // HY_GROUPS
    for g in range(HY_GROUPS):
        sl = slice(g * gw, (g + 1) * gw)
        cat_ref[:, ATTN_WIDTH + g * gw:ATTN_WIDTH + (g + 1) * gw] = _rms(y[:, sl], hg_ref[:, sl]).astype(BF16)
    mix = _dot(cat_ref[...], wo_ref[...])
    h = _layer_norm(x_ref[...], lg_ref[...], lb_ref[...])
    h2 = _layer_norm(ALPHA * h + mix, mg_ref[...], mb_ref[...])
    h2_ref[...] = h2
    h2b_ref[...] = h2.astype(BF16)


def _mix(x2, lg, lb, a, yt2, hg, wo, mg, mb, batch, nblk):
    t = x2.shape[0]
    full = lambda w: pl.BlockSpec(w.shape, lambda i: (0,) * w.ndim)
    row = lambda w: pl.BlockSpec((TOEP, w), lambda i: (i, 0))
    return pl.pallas_call(
        _mix_kernel,
        grid=(t // TOEP,),
        in_specs=[row(D_MODEL), full(lg), full(lb), row(ATTN_WIDTH),
                  pl.BlockSpec((HY_WIDTH, TOEP), lambda i: (0, (i % nblk) * batch + i // nblk)),
                  full(hg), full(wo), full(mg), full(mb)],
        out_specs=[row(D_MODEL), row(D_MODEL)],
        out_shape=[jax.ShapeDtypeStruct((t, D_MODEL), F32), jax.ShapeDtypeStruct((t, D_MODEL), BF16)],
        scratch_shapes=[pltpu.VMEM((TOEP, D_MODEL), BF16)],
        compiler_params=_cparams(("parallel",)),
        name="mix",
    )(x2, lg, lb, a, yt2, hg, wo, mg, mb)


_CAND_COLS = tuple(PEER_TOPK // (r + 1) for r in range(PEER_TOPK))


def _top16(s, val_ref, idx_ref):
    kio = lax.broadcasted_iota(I32, s.shape, 0)
    vals = s
    for r in range(PEER_TOPK):
        m = jnp.max(vals, axis=0, keepdims=True)
        idx = jnp.min(jnp.where(vals == m, kio, PEER_NKEYS), axis=0, keepdims=True)
        val_ref[r:r + 1, :] = m
        idx_ref[r:r + 1, :] = idx
        vals = jnp.where(kio == idx, -jnp.inf, vals)


def _route_kernel(h_ref, wq_ref, keys_ref, isel_ref, jsel_ref, gate_ref,
                  v1_ref, i1_ref, v2_ref, i2_ref, code_ref, best_ref, gsel_ref):
    tt = h_ref.shape[0]
    qt = _dot_nt(wq_ref[...], h_ref[...])
    half = PEER_DK // 2
    neg = -jnp.inf
    for hd in range(PEER_HEADS):
        for p, (vr, ir) in enumerate(((v1_ref, i1_ref), (v2_ref, i2_ref))):
            qp = qt[hd * PEER_DK + p * half: hd * PEER_DK + (p + 1) * half, :].astype(BF16)
            _top16(_dot(keys_ref[hd, p], qp), vr, ir)
        cands, codes = [], []
        for r in range(8):
            ncol = 16 if r == 0 else 8
            cv = v1_ref[r:r + 1, :] + v2_ref[0:ncol, :]
            cio = lax.broadcasted_iota(I32, cv.shape, 0)
            cands.append(jnp.where(cio < _CAND_COLS[r], cv, neg))
            codes.append(i1_ref[r:r + 1, :] * PEER_NKEYS + i2_ref[0:ncol, :])
        cands.append(v1_ref[8:16, :] + v2_ref[0:1, :])
        codes.append(i1_ref[8:16, :] * PEER_NKEYS + i2_ref[0:1, :])
        cand = jnp.concatenate(cands, axis=0)
        code = jnp.concatenate(codes, axis=0)
        rio = lax.broadcasted_iota(I32, cand.shape, 0)
        for kk in range(PEER_TOPK):
            m = jnp.max(cand, axis=0, keepdims=True)
            pos = jnp.min(jnp.where(cand == m, rio, cand.shape[0]), axis=0, keepdims=True)
            sel = rio == pos
            best_ref[kk:kk + 1, :] = m
            code_ref[hd * PEER_TOPK + kk: hd * PEER_TOPK + kk + 1, :] = jnp.max(jnp.where(sel, code, -1), axis=0, keepdims=True)
            cand = jnp.where(sel, neg, cand)
        best = best_ref[...]
        e = jnp.exp(best - best[0:1, :])
        gsel_ref[hd * PEER_TOPK:(hd + 1) * PEER_TOPK, :] = e / jnp.sum(e, axis=0, keepdims=True)
    gate_ref[...] = gsel_ref[...].T
    code_t = code_ref[...].T
    isel_ref[...] = code_t >> 7
    jsel_ref[...] = code_t & (PEER_NKEYS - 1)


def _route(h2b, wqt, keys, tt):
    t = h2b.shape[0]
    full = lambda w: pl.BlockSpec(w.shape, lambda i: (0,) * w.ndim)
    sel = pl.BlockSpec((tt, N_SEL), lambda i: (i, 0))
    return pl.pallas_call(
        _route_kernel,
        grid=(t // tt,),
        in_specs=[pl.BlockSpec((tt, D_MODEL), lambda i: (i, 0)), full(wqt), full(keys)],
        out_specs=[sel, sel, sel],
        out_shape=[jax.ShapeDtypeStruct((t, N_SEL), I32), jax.ShapeDtypeStruct((t, N_SEL), I32),
                   jax.ShapeDtypeStruct((t, N_SEL), F32)],
        scratch_shapes=[pltpu.VMEM((PEER_TOPK, tt), F32), pltpu.VMEM((PEER_TOPK, tt), I32),
                        pltpu.VMEM((PEER_TOPK, tt), F32), pltpu.VMEM((PEER_TOPK, tt), I32),
                        pltpu.VMEM((N_SEL, tt), I32), pltpu.VMEM((PEER_TOPK, tt), F32),
                        pltpu.VMEM((N_SEL, tt), F32)],
        compiler_params=_cparams(("parallel",)),
        name="route",
    )(h2b, wqt, keys)


def _peer_up_kernel(x_ref, u_ref, isel_ref, jsel_ref, gate_ref, w_ref):
    e = pl.program_id(1)
    nsub = u_ref.shape[0] // PEER_NKEYS

    @pl.when(e == 0)
    def _():
        w_ref[...] = jnp.zeros_like(w_ref)

    a = _dot_nt(x_ref[...], u_ref[...])
    isel = isel_ref[...]
    jsel = jsel_ref[...]
    acc = w_ref[...]
    for ii in range(nsub):
        got = jnp.take_along_axis(a[:, ii * PEER_NKEYS:(ii + 1) * PEER_NKEYS], jsel, axis=1)
        acc = jnp.where(isel == e * nsub + ii, got, acc)
    w_ref[...] = acc

    @pl.when(e == pl.num_programs(1) - 1)
    def _():
        s = w_ref[...]
        gelu = 0.5 * s * (1.0 + lax.erf(s * (2.0 ** -0.5)))
        w_ref[...] = gate_ref[...] * gelu


def _peer_up(h2b, ub, isel, jsel, gate, tt, eb):
    t = h2b.shape[0]
    ne = ub.shape[0] // eb
    sel = pl.BlockSpec((tt, N_SEL), lambda i, e: (i, 0))
    return pl.pallas_call(
        _peer_up_kernel,
        grid=(t // tt, ne),
        in_specs=[pl.BlockSpec((tt, D_MODEL), lambda i, e: (i, 0)),
                  pl.BlockSpec((eb, D_MODEL), lambda i, e: (e, 0)), sel, sel, sel],
        out_specs=sel,
        out_shape=jax.ShapeDtypeStruct((t, N_SEL), F32),
        compiler_params=_cparams(("parallel", "arbitrary")),
        name="peer_up",
    )(h2b, ub, isel, jsel, gate)


def _peer_down_kernel(isel_ref, jsel_ref, w_ref, v_ref, h2_ref, lg_ref, lb_ref, o_ref, wd_ref, acc_ref, *, ib):
    e = pl.program_id(1)
    tt = isel_ref.shape[0]
    hi_mask = jnp.uint32(0xFFFF0000)

    @pl.when(e == 0)
    def _():
        acc_ref[...] = jnp.zeros_like(acc_ref)
        sub = lax.broadcasted_iota(I32, (PEER_NKEYS, N_SEL), 0)

        def per_token(tk, carry):
            isr = isel_ref[pl.ds(tk, 1), :]
            jsr = jsel_ref[pl.ds(tk, 1), :]
            wr = w_ref[pl.ds(tk, 1), :]
            pt = jnp.where(sub == isr, wr, 0.0).astype(BF16)
            qt = jnp.where(sub == jsr, 1.0, 0.0).astype(BF16)
            g = _dot_nt(pt, qt).astype(BF16).astype(F32)
            gb = pltpu.bitcast(g, jnp.uint32)
            half = PEER_NKEYS // 2
            packed = (gb[half:] & hi_mask) | (gb[:half] >> 16)
            wd_ref[pl.ds(pl.multiple_of(tk * W_PITCH, 8), half), :] = packed
            return carry

        lax.fori_loop(0, tt, per_token, 0)

    parts = []
    for ii in range(ib):
        pk = wd_ref[pl.ds(e * ib + ii, tt, stride=W_PITCH), :]
        parts.append(pltpu.bitcast(pk << 16, F32).astype(BF16))
        parts.append(pltpu.bitcast(pk & hi_mask, F32).astype(BF16))
    acc_ref[...] += _dot(jnp.concatenate(parts, axis=1), v_ref[...])

    @pl.when(e == pl.num_programs(1) - 1)
    def _():
        o_ref[...] = _layer_norm(ALPHA * h2_ref[...] + acc_ref[...], lg_ref[...], lb_ref[...])


def _peer_down(isel, jsel, w, vperm, h2, lg, lb, tt, ib):
    t = h2.shape[0]
    ne = (PEER_NKEYS // 2) // ib
    vb = ib * 2 * PEER_NKEYS
    sel = pl.BlockSpec((tt, N_SEL), lambda i, e: (i, 0))
    full = lambda a: pl.BlockSpec(a.shape, lambda i, e: (0,) * a.ndim)
    return pl.pallas_call(
        functools.partial(_peer_down_kernel, ib=ib),
        grid=(t // tt, ne),
        in_specs=[sel, sel, sel, pl.BlockSpec((vb, D_MODEL), lambda i, e: (e, 0)),
                  pl.BlockSpec((tt, D_MODEL), lambda i, e: (i, 0)), full(lg), full(lb)],
        out_specs=pl.BlockSpec((tt, D_MODEL), lambda i, e: (i, 0)),
        out_shape=jax.ShapeDtypeStruct((t, D_MODEL), F32),
        scratch_shapes=[pltpu.VMEM((tt * W_PITCH, N_SEL), jnp.uint32), pltpu.VMEM((tt, D_MODEL), F32)],
        compiler_params=_cparams(("parallel", "arbitrary")),
        name="peer_down",
    )(isel, jsel, w, vperm, h2, lg, lb)


def _rope_tables(seq):
    half = QK_ROPE_DIM // 2
    inv = 1.0 / (ROPE_THETA ** (jnp.arange(0, QK_ROPE_DIM, 2, dtype=F32) / QK_ROPE_DIM))
    ang = jnp.arange(seq, dtype=F32)[:, None] * inv[None, :]
    cos, sin = jnp.cos(ang), jnp.sin(ang)
    c32 = jnp.concatenate([cos, cos], axis=1)
    s32 = jnp.concatenate([-sin, sin], axis=1)
    z = lambda w: jnp.zeros((seq, w), F32)
    scale = (QK_NOPE_DIM + QK_ROPE_DIM) ** -0.5
    pad = HEAD_SLOT - QK_NOPE_DIM - QK_ROPE_DIM
    cq = jnp.concatenate([jnp.full((seq, QK_NOPE_DIM), scale, F32), scale * c32, z(pad)], axis=1)
    sq = jnp.concatenate([z(QK_NOPE_DIM), scale * s32, z(pad)], axis=1)
    ck = jnp.concatenate([z(QK_NOPE_DIM), c32, z(pad)], axis=1)
    sk = jnp.concatenate([z(QK_NOPE_DIM), s32, z(pad)], axis=1)
    return cq, sq, ck, sk


def _swap_halves(w):
    half = w.shape[-1] // 2
    return jnp.concatenate([w[..., half:], w[..., :half]], axis=-1)


def _slot(w, offset):
    return jnp.pad(w, ((0, 0), (offset, HEAD_SLOT - offset - w.shape[1])))


def _position_features(seq):
    t = jnp.linspace(0.0, 1.0, seq, dtype=F32)[:, None]
    bands = (HY_EMB_DIM - 1) // 2
    w = 2.0 * math.pi * jnp.arange(seq, dtype=F32) / seq
    f = jnp.linspace(1e-4, bands - 1, bands, dtype=F32)
    ang = w[:, None] * f[None, :]
    return jnp.concatenate([t, jnp.cos(ang), -jnp.sin(ang)], axis=-1)


def kernel(x, emb_ln_g, emb_ln_b, w_in, q_norm_g, w_uq, kv_norm_g, w_ukv, hy_short_w, hy_short_b, hy_filt_w1,
           hy_filt_b1, hy_filt_freq1, hy_filt_w2, hy_filt_b2, hy_filt_freq2, hy_filt_w3, hy_bias, attn_out_g,
           hy_out_g, w_o, ln_mix_g, ln_mix_b, peer_wq, peer_sub_keys, peer_u, peer_v, ln_ffn_g, ln_ffn_b):
    batch, seq, _ = x.shape
    assert w_in.shape[0] == DEPTH == 1 and seq % TOEP == 0
    t = batch * seq
    nblk = seq // TOEP
    r2 = lambda a: a.reshape(1, -1)
    x2 = x.reshape(t, D_MODEL)
    lg, lb = r2(emb_ln_g), r2(emb_ln_b)

    wi = w_in[0]
    w_kr = wi[:, OFF_CKV:OFF_KR]
    wall = jnp.concatenate([wi[:, :OFF_CKV], _slot(w_kr, QK_NOPE_DIM), _slot(_swap_halves(w_kr), QK_NOPE_DIM),
                            wi[:, OFF_KR:]], axis=1).astype(BF16)
    dq = QK_NOPE_DIM + QK_ROPE_DIM
    wuq = w_uq[0].reshape(Q_LORA_RANK, ATTN_HEADS, dq)
    wq = jnp.pad(wuq, ((0, 0), (0, 0), (0, HEAD_SLOT - dq))).reshape(Q_LORA_RANK, -1).astype(BF16)
    wqs = jnp.pad(_swap_halves(wuq[..., QK_NOPE_DIM:]),
                  ((0, 0), (0, 0), (QK_NOPE_DIM, HEAD_SLOT - dq))).reshape(Q_LORA_RANK, -1).astype(BF16)
    wukv = w_ukv[0].reshape(KV_LORA_RANK, ATTN_HEADS, QK_NOPE_DIM + V_HEAD_DIM)
    wk = jnp.pad(wukv[..., :QK_NOPE_DIM],
                 ((0, 0), (0, 0), (0, HEAD_SLOT - QK_NOPE_DIM))).reshape(KV_LORA_RANK, -1).astype(BF16)
    wv = wukv[..., QK_NOPE_DIM:].reshape(KV_LORA_RANK, -1).astype(BF16)
    cq, sq, ck, sk = _rope_tables(seq)

    tt_proj = min(512, seq)
    q, k, v, hy = _ln_proj(x2, lg, lb, wall, r2(q_norm_g[0]), wq, wqs, r2(kv_norm_g[0]), wk, wv,
                           cq, sq, ck, sk, seq, tt_proj)

    a = _attention(q, k, v, r2(attn_out_g[0]), batch, seq, min(256, seq))

    ut = _short_conv(hy.reshape(batch, seq, -1), hy_short_w[0], r2(hy_short_b[0]))
    ut = ut.reshape((HY_ORDER + 1) * HY_WIDTH, nblk * batch, TOEP)

    lag = jnp.abs(jnp.arange(2 * seq) - seq)
    feats = _position_features(seq)
    zt = jnp.pad(feats[jnp.minimum(lag, seq - 1)].T, ((0, HY_FILTER_HIDDEN - HY_EMB_DIM), (0, 0)))
    col = lambda a_: a_.reshape(-1, 1)
    w1t = jnp.pad(hy_filt_w1[0].T, ((0, 0), (0, HY_FILTER_HIDDEN - HY_EMB_DIM)))
    w3t = hy_filt_w3[0].T.reshape(HY_ORDER, 2, HY_WIDTH, HY_FILTER_HIDDEN)
    deltas = jnp.abs(jnp.linspace(math.log(HY_TARGET) / HY_SLOW_DECAY, math.log(HY_TARGET) / HY_FAST_DECAY,
                                  HY_WIDTH, dtype=F32))
    g = _filters(zt, w1t, col(hy_filt_b1[0]), col(hy_filt_freq1[0]), hy_filt_w2[0].T, col(hy_filt_b2[0]),
                 col(hy_filt_freq2[0]), w3t, col(deltas))
    yt = _hyena(hy_bias[0], ut, g.reshape(HY_ORDER, HY_WIDTH, 1, 2 * seq), batch, nblk, 8)

    h2, h2b = _mix(x2, lg, lb, a, yt.reshape(HY_WIDTH, nblk * batch * TOEP), r2(hy_out_g[0]),
                   w_o[0].astype(BF16), r2(ln_mix_g[0]), r2(ln_mix_b[0]), batch, nblk)

    isel, jsel, gate = _route(h2b, peer_wq[0].T.astype(BF16), peer_sub_keys[0].astype(BF16), min(256, t))
    w = _peer_up(h2b, peer_u[0].astype(BF16), isel, jsel, gate, min(1024, t), 1024)
    ib = 8
    half = PEER_NKEYS // 2
    vperm = peer_v[0].astype(BF16).reshape(2, half // ib, ib, PEER_NKEYS, D_MODEL)
    vperm = vperm.transpose(1, 2, 0, 3, 4).reshape(PEER_NKEYS * PEER_NKEYS, D_MODEL)
    out = _peer_down(isel, jsel, w, vperm, h2, r2(ln_ffn_g[0]), r2(ln_ffn_b[0]), min(512, t), ib)
    return out.reshape(batch, seq, D_MODEL)
```

```python
import functools
import math

import jax
import jax.numpy as jnp
from jax import lax
from jax.experimental import pallas as pl
from jax.experimental.pallas import tpu as pltpu

F32 = jnp.float32
BF16 = jnp.bfloat16
I32 = jnp.int32

D_MODEL = 1024
ATTN_HEADS = 8
QK_NOPE_DIM = 64
QK_ROPE_DIM = 32
V_HEAD_DIM = 64
Q_LORA_RANK = 256
KV_LORA_RANK = 128
ATTN_WIDTH = ATTN_HEADS * V_HEAD_DIM
ROPE_THETA = 10000.0
HY_WIDTH = D_MODEL - ATTN_WIDTH
HY_ORDER = 2
HY_GROUPS = 8
HY_SHORT = 3
HY_EMB_DIM = 33
HY_FILTER_HIDDEN = 64
HY_FAST_DECAY = 0.3
HY_SLOW_DECAY = 1.5
HY_TARGET = 1e-2
OFF_CQ = Q_LORA_RANK
OFF_CKV = OFF_CQ + KV_LORA_RANK
OFF_KR = OFF_CKV + QK_ROPE_DIM
PEER_HEADS = 8
PEER_NKEYS = 128
PEER_DK = 128
PEER_TOPK = 16
DEPTH = 1
ALPHA = (2 * DEPTH) ** 0.25
LN_EPS = 1e-5
RMS_EPS = 1e-6

LANES = 128
HEAD_SLOT = 128
TOEP = 256
N_SEL = PEER_HEADS * PEER_TOPK
W_PITCH = 72
TOKEN_GROUP = 16
VMEM_LIMIT = 56 * 1024 * 1024


def _cparams(sem):
    return pltpu.CompilerParams(dimension_semantics=sem, vmem_limit_bytes=VMEM_LIMIT)


def _layer_norm(x, g, b):
    mu = jnp.mean(x, axis=-1, keepdims=True)
    xc = x - mu
    var = jnp.mean(xc * xc, axis=-1, keepdims=True)
    return xc * lax.rsqrt(var + LN_EPS) * g + b


def _rms(x, g):
    return x * lax.rsqrt(jnp.mean(x * x, axis=-1, keepdims=True) + RMS_EPS) * g


def _dot(a, b):
    return jnp.dot(a, b, preferred_element_type=F32)


def _dot_nt(a, b):
    return lax.dot_general(a, b, (((1,), (1,)), ((), ())), preferred_element_type=F32)


N_MLA_COLS = Q_LORA_RANK + KV_LORA_RANK + 2 * HEAD_SLOT


def _ln_proj_kernel(x_ref, lg_ref, lb_ref, wall_ref, gq_ref, wq_ref, wqs_ref, gkv_ref, wk_ref, wv_ref,
                    cq_ref, sq_ref, ck_ref, sk_ref, q_ref, k_ref, v_ref, hy_ref):
    h = _layer_norm(x_ref[...], lg_ref[...], lb_ref[...])
    proj = _dot(h.astype(BF16), wall_ref[...])
    c_q = proj[:, :OFF_CQ]
    c_kv = proj[:, OFF_CQ:OFF_CKV]
    kr = proj[:, OFF_CKV:OFF_CKV + HEAD_SLOT]
    kr_sw = proj[:, OFF_CKV + HEAD_SLOT:N_MLA_COLS]
    hy_ref[...] = proj[:, N_MLA_COLS:].astype(BF16)

    nq = _rms(c_q, gq_ref[...]).astype(BF16)
    qa = _dot(nq, wq_ref[...])
    qb = _dot(nq, wqs_ref[...])
    nkv = _rms(c_kv, gkv_ref[...]).astype(BF16)
    kn = _dot(nkv, wk_ref[...])
    v_ref[...] = _dot(nkv, wv_ref[...]).astype(BF16)
    k_pe = kr * ck_ref[...] + kr_sw * sk_ref[...]
    cq, sq = cq_ref[...], sq_ref[...]
    for hd in range(ATTN_HEADS):
        sl = slice(hd * HEAD_SLOT, (hd + 1) * HEAD_SLOT)
        q_ref[:, sl] = (qa[:, sl] * cq + qb[:, sl] * sq).astype(BF16)
        k_ref[:, sl] = (kn[:, sl] + k_pe).astype(BF16)


def _ln_proj(x2, lg, lb, wall, gq, wq, wqs, gkv, wk, wv, cq, sq, ck, sk, seq, tt):
    t = x2.shape[0]
    npos = seq // tt
    full = lambda a: pl.BlockSpec(a.shape, lambda i: (0,) * a.ndim)
    tab = pl.BlockSpec((tt, HEAD_SLOT), lambda i: (i % npos, 0))
    row = lambda w: pl.BlockSpec((tt, w), lambda i: (i, 0))
    hw = wall.shape[1] - N_MLA_COLS
    return pl.pallas_call(
        _ln_proj_kernel,
        grid=(t // tt,),
        in_specs=[row(D_MODEL), full(lg), full(lb), full(wall), full(gq), full(wq), full(wqs), full(gkv),
                  full(wk), full(wv), tab, tab, tab, tab],
        out_specs=[row(ATTN_HEADS * HEAD_SLOT), row(ATTN_HEADS * HEAD_SLOT), row(ATTN_WIDTH), row(hw)],
        out_shape=[jax.ShapeDtypeStruct((t, ATTN_HEADS * HEAD_SLOT), BF16),
                   jax.ShapeDtypeStruct((t, ATTN_HEADS * HEAD_SLOT), BF16),
                   jax.ShapeDtypeStruct((t, ATTN_WIDTH), BF16),
                   jax.ShapeDtypeStruct((t, hw), BF16)],
        compiler_params=_cparams(("parallel",)),
        name="ln_proj",
    )(x2, lg, lb, wall, gq, wq, wqs, gkv, wk, wv, cq, sq, ck, sk)


def _attn_kernel(q_ref, k_ref, v_ref, g_ref, o_ref):
    v = v_ref[...]
    for hh in range(2):
        sl = slice(hh * HEAD_SLOT, (hh + 1) * HEAD_SLOT)
        s = _dot_nt(q_ref[:, sl], k_ref[:, sl])
        m = jnp.max(s, axis=1, keepdims=True)
        p = jnp.exp(s - m)
        l = jnp.sum(p, axis=1, keepdims=True)
        o2 = _dot(p.astype(BF16), v)
        vs = slice(hh * V_HEAD_DIM, (hh + 1) * V_HEAD_DIM)
        o = o2[:, vs] / l
        o_ref[:, vs] = _rms(o, g_ref[:, vs]).astype(BF16)


def _attention(q, k, v, g, batch, seq, tq):
    t = q.shape[0]
    nq = seq // tq
    npair = ATTN_HEADS // 2
    return pl.pallas_call(
        _attn_kernel,
        grid=(batch, npair, nq),
        in_specs=[pl.BlockSpec((tq, 2 * HEAD_SLOT), lambda b, p, i: (b * nq + i, p)),
                  pl.BlockSpec((seq, 2 * HEAD_SLOT), lambda b, p, i: (b, p)),
                  pl.BlockSpec((seq, 2 * V_HEAD_DIM), lambda b, p, i: (b, p)),
                  pl.BlockSpec((1, 2 * V_HEAD_DIM), lambda b, p, i: (0, p))],
        out_specs=pl.BlockSpec((tq, 2 * V_HEAD_DIM), lambda b, p, i: (b * nq + i, p)),
        out_shape=jax.ShapeDtypeStruct((t, ATTN_WIDTH), BF16),
        compiler_params=_cparams(("parallel", "parallel", "parallel")),
        name="attention",
    )(q, k, v, g)


def _short_conv_kernel(prev_ref, x_ref, next_ref, w_ref, b_ref, o_ref):
    j = pl.program_id(1)
    nj = pl.num_programs(1)
    x = x_ref[0].astype(F32)
    rows = x.shape[0]
    before = jnp.where(j > 0, prev_ref[0, 7:8, :].astype(F32), 0.0)
    after = jnp.where(j < nj - 1, next_ref[0, 0:1, :].astype(F32), 0.0)
    rid = lax.broadcasted_iota(I32, x.shape, 0)
    xm = jnp.where(rid == 0, before, pltpu.roll(x, 1, 0))
    xp = jnp.where(rid == rows - 1, after, pltpu.roll(x, rows - 1, 0))
    u = b_ref[...] + xm * w_ref[0:1, :] + x * w_ref[1:2, :] + xp * w_ref[2:3, :]
    o_ref[...] = u.T.astype(BF16)


def _short_conv(hy3, w, b):
    batch, seq, width = hy3.shape
    nj = seq // TOEP
    sub = TOEP // 8
    last8 = seq // 8 - 1
    return pl.pallas_call(
        _short_conv_kernel,
        grid=(batch, nj),
        in_specs=[pl.BlockSpec((1, 8, width), lambda bb, j: (bb, jnp.maximum(j * sub - 1, 0), 0)),
                  pl.BlockSpec((1, TOEP, width), lambda bb, j: (bb, j, 0)),
                  pl.BlockSpec((1, 8, width), lambda bb, j: (bb, jnp.minimum((j + 1) * sub, last8), 0)),
                  pl.BlockSpec(w.shape, lambda bb, j: (0, 0)),
                  pl.BlockSpec(b.shape, lambda bb, j: (0, 0))],
        out_specs=pl.BlockSpec((width, TOEP), lambda bb, j: (0, j * batch + bb)),
        out_shape=jax.ShapeDtypeStruct((width, nj * batch * TOEP), BF16),
        compiler_params=_cparams(("parallel", "parallel")),
        name="short_conv",
    )(hy3, hy3, hy3, w, b)


def _filter_kernel(z_ref, w1_ref, b1_ref, f1_ref, w2_ref, b2_ref, f2_ref, w3_ref, dl_ref, g_ref):
    hp = lax.Precision.HIGHEST
    z = z_ref[...]
    n = z.shape[1]
    half = n // 2
    h1 = jnp.sin(f1_ref[...] * (jnp.dot(w1_ref[...], z, precision=hp, preferred_element_type=F32) + b1_ref[...]))
    h2 = jnp.sin(f2_ref[...] * (jnp.dot(w2_ref[...], h1, precision=hp, preferred_element_type=F32) + b2_ref[...]))
    decay = jnp.exp(-z[0:1, :] * dl_ref[...])
    lane = lax.broadcasted_iota(I32, (1, n), 1)
    masks = (lane >= half, (lane >= 1) & (lane <= half))
    out = None
    for d in range(2):
        hd = jnp.dot(w3_ref[0, d], h2, precision=hp, preferred_element_type=F32) * decay
        hd = jnp.where(masks[d], hd, 0.0)
        hd = hd / (jnp.sum(jnp.abs(hd), axis=1, keepdims=True) + 1e-6)
        out = hd if out is None else out + hd
    g_ref[0] = out


def _filters(zt, w1t, b1, f1, w2t, b2, f2, w3t, dl):
    n = zt.shape[1]
    cb = 128
    full = lambda a: pl.BlockSpec(a.shape, lambda o, c: (0,) * a.ndim)
    return pl.pallas_call(
        _filter_kernel,
        grid=(HY_ORDER, HY_WIDTH // cb),
        in_specs=[full(zt), full(w1t), full(b1), full(f1), full(w2t), full(b2), full(f2),
                  pl.BlockSpec((1, 2, cb, HY_FILTER_HIDDEN), lambda o, c: (o, 0, c, 0)),
                  pl.BlockSpec((cb, 1), lambda o, c: (c, 0))],
        out_specs=pl.BlockSpec((1, cb, n), lambda o, c: (o, c, 0)),
        out_shape=jax.ShapeDtypeStruct((HY_ORDER, HY_WIDTH, n), F32),
        compiler_params=_cparams(("parallel", "parallel")),
        name="filters",
    )(zt, w1t, b1, f1, w2t, b2, f2, w3t, dl)


def _hyena_kernel(bias_ref, v_ref, x1_ref, x2_ref, g_ref, o_ref, acc_ref, *, batch, nblk, cb):
    rows = batch * nblk
    seq = nblk * TOEP
    c0 = pl.program_id(0) * cb
    gates = (x1_ref, x2_ref)

    def per_channel(c, carry):
        zf = v_ref[c].astype(F32)
        for o in range(HY_ORDER):
            zb = zf.astype(BF16)
            acc_ref[...] = jnp.zeros_like(acc_ref)
            for d in range(-(nblk - 1), nblk):
                base = seq + TOEP * d - TOEP
                win = g_ref[o, c, :, pl.ds(base, 2 * TOEP)]
                w = pltpu.roll(jnp.broadcast_to(win, (TOEP, 2 * TOEP)), TOEP, 1, stride=1, stride_axis=0)
                w = w[:, :TOEP].astype(BF16)
                n = rows - batch * abs(d)
                if d >= 0:
                    acc_ref[batch * d:, :] += _dot(zb[:n], w)
                else:
                    acc_ref[:n, :] += _dot(zb[batch * (-d):], w)
            zf = gates[o][c].astype(F32) * (acc_ref[...] + bias_ref[o, c0 + c] * zf)
        o_ref[c] = zf.astype(BF16)
        return carry

    lax.fori_loop(0, cb, per_channel, 0)


def _hyena(bias, ut, g4, batch, nblk, cb):
    rows = batch * nblk
    nc = HY_WIDTH // cb
    kern = functools.partial(_hyena_kernel, batch=batch, nblk=nblk, cb=cb)
    blk = lambda off: pl.BlockSpec((cb, rows, TOEP), lambda i: (off * nc + i, 0, 0))
    return pl.pallas_call(
        kern,
        grid=(nc,),
        in_specs=[pl.BlockSpec(memory_space=pltpu.SMEM), blk(0), blk(1), blk(2),
                  pl.BlockSpec((HY_ORDER, cb, 1, g4.shape[3]), lambda i: (0, i, 0, 0))],
        out_specs=pl.BlockSpec((cb, rows, TOEP), lambda i: (i, 0, 0)),
        out_shape=jax.ShapeDtypeStruct((HY_WIDTH, rows, TOEP), BF16),
        scratch_shapes=[pltpu.VMEM((rows, TOEP), F32)],
        compiler_params=_cparams(("parallel",)),
        name="hyena",
    )(bias, ut, ut, ut, g4)


def _mix_kernel(x_ref, lg_ref, lb_ref, a_ref, yt_ref, hg_ref, wo_ref, mg_ref, mb_ref, h2_ref, h2b_ref, cat_ref):
    y = yt_ref[...].astype(F32).T
    cat_ref[:, :ATTN_WIDTH] = a_ref[...]
    gw = HY_WIDTH // HY_GROUPS
    for g in range(HY_GROUPS):
        sl = slice(g * gw, (g + 1) * gw)
        cat_ref[:, ATTN_WIDTH + g * gw:ATTN_WIDTH + (g + 1) * gw] = _rms(y[:, sl], hg_ref[:, sl]).astype(BF16)
    mix = _dot(cat_ref[...], wo_ref[...])
    h = _layer_norm(x_ref[...], lg_ref[...], lb_ref[...])
    h2 = _layer_norm(ALPHA * h + mix, mg_ref[...], mb_ref[...])
    h2_ref[...] = h2
    h2b_ref[...] = h2.astype(BF16)


def _mix(x2, lg, lb, a, yt2, hg, wo, mg, mb, batch, nblk):
    t = x2.shape[0]
    full = lambda w: pl.BlockSpec(w.shape, lambda i: (0,) * w.ndim)
    row = lambda w: pl.BlockSpec((TOEP, w), lambda i: (i, 0))
    return pl.pallas_call(
        _mix_kernel,
        grid=(t // TOEP,),
        in_specs=[row(D_MODEL), full(lg), full(lb), row(ATTN_WIDTH),
                  pl.BlockSpec((HY_WIDTH, TOEP), lambda i: (0, (i % nblk) * batch + i // nblk)),
                  full(hg), full(wo), full(mg), full(mb)],
        out_specs=[row(D_MODEL), row(D_MODEL)],
        out_shape=[jax.ShapeDtypeStruct((t, D_MODEL), F32), jax.ShapeDtypeStruct((t, D_MODEL), BF16)],
        scratch_shapes=[pltpu.VMEM((TOEP, D_MODEL), BF16)],
        compiler_params=_cparams(("parallel",)),
        name="mix",
    )(x2, lg, lb, a, yt2, hg, wo, mg, mb)


_CAND_COLS = tuple(PEER_TOPK // (r + 1) for r in range(PEER_TOPK))


def _top16(s, val_ref, idx_ref):
    kio = lax.broadcasted_iota(I32, s.shape, 0)
    vals = s
    for r in range(PEER_TOPK):
        m = jnp.max(vals, axis=0, keepdims=True)
        idx = jnp.min(jnp.where(vals == m, kio, PEER_NKEYS), axis=0, keepdims=True)
        val_ref[r:r + 1, :] = m
        idx_ref[r:r + 1, :] = idx
        vals = jnp.where(kio == idx, -jnp.inf, vals)


def _route_kernel(h_ref, wq_ref, keys_ref, isel_ref, jsel_ref, gate_ref,
                  v1_ref, i1_ref, v2_ref, i2_ref, code_ref, best_ref, gsel_ref):
    tt = h_ref.shape[0]
    qt = _dot_nt(wq_ref[...], h_ref[...])
    half = PEER_DK // 2
    neg = -jnp.inf
    for hd in range(PEER_HEADS):
        for p, (vr, ir) in enumerate(((v1_ref, i1_ref), (v2_ref, i2_ref))):
            qp = qt[hd * PEER_DK + p * half: hd * PEER_DK + (p + 1) * half, :].astype(BF16)
            _top16(_dot(keys_ref[hd, p], qp), vr, ir)
        cands, codes = [], []
        for r in range(8):
            ncol = 16 if r == 0 else 8
            cv = v1_ref[r:r + 1, :] + v2_ref[0:ncol, :]
            cio = lax.broadcasted_iota(I32, cv.shape, 0)
            cands.append(jnp.where(cio < _CAND_COLS[r], cv, neg))
            codes.append(i1_ref[r:r + 1, :] * PEER_NKEYS + i2_ref[0:ncol, :])
        cands.append(v1_ref[8:16, :] + v2_ref[0:1, :])
        codes.append(i1_ref[8:16, :] * PEER_NKEYS + i2_ref[0:1, :])
        cand = jnp.concatenate(cands, axis=0)
        code = jnp.concatenate(codes, axis=0)
        rio = lax.broadcasted_iota(I32, cand.shape, 0)
        for kk in range(PEER_TOPK):
            m = jnp.max(cand, axis=0, keepdims=True)
            pos = jnp.min(jnp.where(cand == m, rio, cand.shape[0]), axis=0, keepdims=True)
            sel = rio == pos
            best_ref[kk:kk + 1, :] = m
            code_ref[hd * PEER_TOPK + kk: hd * PEER_TOPK + kk + 1, :] = jnp.max(jnp.where(sel, code, -1), axis=0, keepdims=True)
            cand = jnp.where(sel, neg, cand)
        best = best_ref[...]
        e = jnp.exp(best - best[0:1, :])
        gsel_ref[hd * PEER_TOPK:(hd + 1) * PEER_TOPK, :] = e / jnp.sum(e, axis=0, keepdims=True)
    gate_ref[...] = gsel_ref[...].T
    code_t = code_ref[...].T
    isel_ref[...] = code_t >> 7
    jsel_ref[...] = code_t & (PEER_NKEYS - 1)


def _route(h2b, wqt, keys, tt):
    t = h2b.shape[0]
    full = lambda w: pl.BlockSpec(w.shape, lambda i: (0,) * w.ndim)
    sel = pl.BlockSpec((tt, N_SEL), lambda i: (i, 0))
    return pl.pallas_call(
        _route_kernel,
        grid=(t // tt,),
        in_specs=[pl.BlockSpec((tt, D_MODEL), lambda i: (i, 0)), full(wqt), full(keys)],
        out_specs=[sel, sel, sel],
        out_shape=[jax.ShapeDtypeStruct((t, N_SEL), I32), jax.ShapeDtypeStruct((t, N_SEL), I32),
                   jax.ShapeDtypeStruct((t, N_SEL), F32)],
        scratch_shapes=[pltpu.VMEM((PEER_TOPK, tt), F32), pltpu.VMEM((PEER_TOPK, tt), I32),
                        pltpu.VMEM((PEER_TOPK, tt), F32), pltpu.VMEM((PEER_TOPK, tt), I32),
                        pltpu.VMEM((N_SEL, tt), I32), pltpu.VMEM((PEER_TOPK, tt), F32),
                        pltpu.VMEM((N_SEL, tt), F32)],
        compiler_params=_cparams(("parallel",)),
        name="route",
    )(h2b, wqt, keys)


def _peer_up_kernel(x_ref, u_ref, isel_ref, jsel_ref, gate_ref, w_ref):
    e = pl.program_id(1)
    nsub = u_ref.shape[0] // PEER_NKEYS

    @pl.when(e == 0)
    def _():
        w_ref[...] = jnp.zeros_like(w_ref)

    a = _dot_nt(x_ref[...], u_ref[...])
    isel = isel_ref[...]
    jsel = jsel_ref[...]
    acc = w_ref[...]
    for ii in range(nsub):
        got = jnp.take_along_axis(a[:, ii * PEER_NKEYS:(ii + 1) * PEER_NKEYS], jsel, axis=1)
        acc = jnp.where(isel == e * nsub + ii, got, acc)
    w_ref[...] = acc

    @pl.when(e == pl.num_programs(1) - 1)
    def _():
        s = w_ref[...]
        gelu = 0.5 * s * (1.0 + lax.erf(s * (2.0 ** -0.5)))
        w_ref[...] = gate_ref[...] * gelu


def _peer_up(h2b, ub, isel, jsel, gate, tt, eb):
    t = h2b.shape[0]
    ne = ub.shape[0] // eb
    sel = pl.BlockSpec((tt, N_SEL), lambda i, e: (i, 0))
    return pl.pallas_call(
        _peer_up_kernel,
        grid=(t // tt, ne),
        in_specs=[pl.BlockSpec((tt, D_MODEL), lambda i, e: (i, 0)),
                  pl.BlockSpec((eb, D_MODEL), lambda i, e: (e, 0)), sel, sel, sel],
        out_specs=sel,
        out_shape=jax.ShapeDtypeStruct((t, N_SEL), F32),
        compiler_params=_cparams(("parallel", "arbitrary")),
        name="peer_up",
    )(h2b, ub, isel, jsel, gate)


def _peer_down_kernel(isel_ref, jsel_ref, w_ref, v_ref, h2_ref, lg_ref, lb_ref, o_ref, wd_ref, acc_ref, *, ib):
    e = pl.program_id(1)
    tt = isel_ref.shape[0]
    hi_mask = jnp.uint32(0xFFFF0000)

    @pl.when(e == 0)
    def _():
        acc_ref[...] = jnp.zeros_like(acc_ref)
        sub = lax.broadcasted_iota(I32, (PEER_NKEYS, N_SEL), 0)

        half = PEER_NKEYS // 2

        def per_group(gi, carry):
            base = pl.multiple_of(gi * TOKEN_GROUP, TOKEN_GROUP)
            isb = isel_ref[pl.ds(base, TOKEN_GROUP), :]
            jsb = jsel_ref[pl.ds(base, TOKEN_GROUP), :]
            wb = w_ref[pl.ds(base, TOKEN_GROUP), :]
            for r in range(TOKEN_GROUP):
                pt = jnp.where(sub == isb[r:r + 1, :], wb[r:r + 1, :], 0.0).astype(BF16)
                qt = jnp.where(sub == jsb[r:r + 1, :], 1.0, 0.0).astype(BF16)
                gb = pltpu.bitcast(_dot_nt(pt, qt), jnp.uint32)
                packed = (gb[half:] & hi_mask) | (gb[:half] >> 16)
                wd_ref[pl.ds(pl.multiple_of((base + r) * W_PITCH, 8), half), :] = packed
            return carry

        lax.fori_loop(0, tt // TOKEN_GROUP, per_group, 0)

    parts = []
    for ii in range(ib):
        pk = wd_ref[pl.ds(e * ib + ii, tt, stride=W_PITCH), :]
        parts.append(pltpu.bitcast(pk << 16, F32).astype(BF16))
        parts.append(pltpu.bitcast(pk & hi_mask, F32).astype(BF16))
    acc_ref[...] += _dot(jnp.concatenate(parts, axis=1), v_ref[...])

    @pl.when(e == pl.num_programs(1) - 1)
    def _():
        o_ref[...] = _layer_norm(ALPHA * h2_ref[...] + acc_ref[...], lg_ref[...], lb_ref[...])


def _peer_down(isel, jsel, w, vperm, h2, lg, lb, tt, ib):
    t = h2.shape[0]
    ne = (PEER_NKEYS // 2) // ib
    vb = ib * 2 * PEER_NKEYS
    sel = pl.BlockSpec((tt, N_SEL), lambda i, e: (i, 0))
    full = lambda a: pl.BlockSpec(a.shape, lambda i, e: (0,) * a.ndim)
    return pl.pallas_call(
        functools.partial(_peer_down_kernel, ib=ib),
        grid=(t // tt, ne),
        in_specs=[sel, sel, sel, pl.BlockSpec((vb, D_MODEL), lambda i, e: (e, 0)),
                  pl.BlockSpec((tt, D_MODEL), lambda i, e: (i, 0)), full(lg), full(lb)],
        out_specs=pl.BlockSpec((tt, D_MODEL), lambda i, e: (i, 0)),
        out_shape=jax.ShapeDtypeStruct((t, D_MODEL), F32),
        scratch_shapes=[pltpu.VMEM((tt * W_PITCH, N_SEL), jnp.uint32), pltpu.VMEM((tt, D_MODEL), F32)],
        compiler_params=_cparams(("parallel", "arbitrary")),
        name="peer_down",
    )(isel, jsel, w, vperm, h2, lg, lb)


def _rope_tables(seq):
    half = QK_ROPE_DIM // 2
    inv = 1.0 / (ROPE_THETA ** (jnp.arange(0, QK_ROPE_DIM, 2, dtype=F32) / QK_ROPE_DIM))
    ang = jnp.arange(seq, dtype=F32)[:, None] * inv[None, :]
    cos, sin = jnp.cos(ang), jnp.sin(ang)
    c32 = jnp.concatenate([cos, cos], axis=1)
    s32 = jnp.concatenate([-sin, sin], axis=1)
    z = lambda w: jnp.zeros((seq, w), F32)
    scale = (QK_NOPE_DIM + QK_ROPE_DIM) ** -0.5
    pad = HEAD_SLOT - QK_NOPE_DIM - QK_ROPE_DIM
    cq = jnp.concatenate([jnp.full((seq, QK_NOPE_DIM), scale, F32), scale * c32, z(pad)], axis=1)
    sq = jnp.concatenate([z(QK_NOPE_DIM), scale * s32, z(pad)], axis=1)
    ck = jnp.concatenate([z(QK_NOPE_DIM), c32, z(pad)], axis=1)
    sk = jnp.concatenate([z(QK_NOPE_DIM), s32, z(pad)], axis=1)
    return cq, sq, ck, sk


def _swap_halves(w):
    half = w.shape[-1] // 2
    return jnp.concatenate([w[..., half:], w[..., :half]], axis=-1)


def _slot(w, offset):
    return jnp.pad(w, ((0, 0), (offset, HEAD_SLOT - offset - w.shape[1])))


def _position_features(seq):
    t = jnp.linspace(0.0, 1.0, seq, dtype=F32)[:, None]
    bands = (HY_EMB_DIM - 1) // 2
    w = 2.0 * math.pi * jnp.arange(seq, dtype=F32) / seq
    f = jnp.linspace(1e-4, bands - 1, bands, dtype=F32)
    ang = w[:, None] * f[None, :]
    return jnp.concatenate([t, jnp.cos(ang), -jnp.sin(ang)], axis=-1)


def kernel(x, emb_ln_g, emb_ln_b, w_in, q_norm_g, w_uq, kv_norm_g, w_ukv, hy_short_w, hy_short_b, hy_filt_w1,
           hy_filt_b1, hy_filt_freq1, hy_filt_w2, hy_filt_b2, hy_filt_freq2, hy_filt_w3, hy_bias, attn_out_g,
           hy_out_g, w_o, ln_mix_g, ln_mix_b, peer_wq, peer_sub_keys, peer_u, peer_v, ln_ffn_g, ln_ffn_b):
    batch, seq, _ = x.shape
    assert w_in.shape[0] == DEPTH == 1 and seq % TOEP == 0
    t = batch * seq
    nblk = seq // TOEP
    r2 = lambda a: a.reshape(1, -1)
    x2 = x.reshape(t, D_MODEL)
    lg, lb = r2(emb_ln_g), r2(emb_ln_b)

    wi = w_in[0]
    w_kr = wi[:, OFF_CKV:OFF_KR]
    wall = jnp.concatenate([wi[:, :OFF_CKV], _slot(w_kr, QK_NOPE_DIM), _slot(_swap_halves(w_kr), QK_NOPE_DIM),
                            wi[:, OFF_KR:]], axis=1).astype(BF16)
    dq = QK_NOPE_DIM + QK_ROPE_DIM
    wuq = w_uq[0].reshape(Q_LORA_RANK, ATTN_HEADS, dq)
    wq = jnp.pad(wuq, ((0, 0), (0, 0), (0, HEAD_SLOT - dq))).reshape(Q_LORA_RANK, -1).astype(BF16)
    wqs = jnp.pad(_swap_halves(wuq[..., QK_NOPE_DIM:]),
                  ((0, 0), (0, 0), (QK_NOPE_DIM, HEAD_SLOT - dq))).reshape(Q_LORA_RANK, -1).astype(BF16)
    wukv = w_ukv[0].reshape(KV_LORA_RANK, ATTN_HEADS, QK_NOPE_DIM + V_HEAD_DIM)
    wk = jnp.pad(wukv[..., :QK_NOPE_DIM],
                 ((0, 0), (0, 0), (0, HEAD_SLOT - QK_NOPE_DIM))).reshape(KV_LORA_RANK, -1).astype(BF16)
    wv = wukv[..., QK_NOPE_DIM:].reshape(KV_LORA_RANK, -1).astype(BF16)
    cq, sq, ck, sk = _rope_tables(seq)

    tt_proj = min(512, seq)
    q, k, v, hy = _ln_proj(x2, lg, lb, wall, r2(q_norm_g[0]), wq, wqs, r2(kv_norm_g[0]), wk, wv,
                           cq, sq, ck, sk, seq, tt_proj)

    a = _attention(q, k, v, r2(attn_out_g[0]), batch, seq, min(256, seq))

    ut = _short_conv(hy.reshape(batch, seq, -1), hy_short_w[0], r2(hy_short_b[0]))
    ut = ut.reshape((HY_ORDER + 1) * HY_WIDTH, nblk * batch, TOEP)

    lag = jnp.abs(jnp.arange(2 * seq) - seq)
    feats = _position_features(seq)
    zt = jnp.pad(feats[jnp.minimum(lag, seq - 1)].T, ((0, HY_FILTER_HIDDEN - HY_EMB_DIM), (0, 0)))
    col = lambda a_: a_.reshape(-1, 1)
    w1t = jnp.pad(hy_filt_w1[0].T, ((0, 0), (0, HY_FILTER_HIDDEN - HY_EMB_DIM)))
    w3t = hy_filt_w3[0].T.reshape(HY_ORDER, 2, HY_WIDTH, HY_FILTER_HIDDEN)
    deltas = jnp.abs(jnp.linspace(math.log(HY_TARGET) / HY_SLOW_DECAY, math.log(HY_TARGET) / HY_FAST_DECAY,
                                  HY_WIDTH, dtype=F32))
    g = _filters(zt, w1t, col(hy_filt_b1[0]), col(hy_filt_freq1[0]), hy_filt_w2[0].T, col(hy_filt_b2[0]),
                 col(hy_filt_freq2[0]), w3t, col(deltas))
    yt = _hyena(hy_bias[0], ut, g.reshape(HY_ORDER, HY_WIDTH, 1, 2 * seq), batch, nblk, 8)

    h2, h2b = _mix(x2, lg, lb, a, yt.reshape(HY_WIDTH, nblk * batch * TOEP), r2(hy_out_g[0]),
                   w_o[0].astype(BF16), r2(ln_mix_g[0]), r2(ln_mix_b[0]), batch, nblk)

    isel, jsel, gate = _route(h2b, peer_wq[0].T.astype(BF16), peer_sub_keys[0].astype(BF16), min(256, t))
    w = _peer_up(h2b, peer_u[0].astype(BF16), isel, jsel, gate, min(1024, t), 1024)
    ib = 8
    half = PEER_NKEYS // 2
    vperm = peer_v[0].astype(BF16).reshape(2, half // ib, ib, PEER_NKEYS, D_MODEL)
    vperm = vperm.transpose(1, 2, 0, 3, 4).reshape(PEER_NKEYS * PEER_NKEYS, D_MODEL)
    out = _peer_down(isel, jsel, w, vperm, h2, r2(ln_ffn_g[0]), r2(ln_ffn_b[0]), min(512, t), ib)
    return out.reshape(batch, seq, D_MODEL)
```

```python
import functools
import math

import jax
import jax.numpy as jnp
from jax import lax
from jax.experimental import pallas as pl
from jax.experimental.pallas import tpu as pltpu

F32 = jnp.float32
BF16 = jnp.bfloat16
I32 = jnp.int32

D_MODEL = 1024
ATTN_HEADS = 8
QK_NOPE_DIM = 64
QK_ROPE_DIM = 32
V_HEAD_DIM = 64
Q_LORA_RANK = 256
KV_LORA_RANK = 128
ATTN_WIDTH = ATTN_HEADS * V_HEAD_DIM
ROPE_THETA = 10000.0
HY_WIDTH = D_MODEL - ATTN_WIDTH
HY_ORDER = 2
HY_GROUPS = 8
HY_SHORT = 3
HY_EMB_DIM = 33
HY_FILTER_HIDDEN = 64
HY_FAST_DECAY = 0.3
HY_SLOW_DECAY = 1.5
HY_TARGET = 1e-2
OFF_CQ = Q_LORA_RANK
OFF_CKV = OFF_CQ + KV_LORA_RANK
OFF_KR = OFF_CKV + QK_ROPE_DIM
PEER_HEADS = 8
PEER_NKEYS = 128
PEER_DK = 128
PEER_TOPK = 16
DEPTH = 1
ALPHA = (2 * DEPTH) ** 0.25
LN_EPS = 1e-5
RMS_EPS = 1e-6

LANES = 128
HEAD_SLOT = 128
TOEP = 256
N_SEL = PEER_HEADS * PEER_TOPK
W_PITCH = 72
TOKEN_GROUP = 16
GATHER_ROWS = 8
VMEM_LIMIT = 56 * 1024 * 1024


def _cparams(sem):
    return pltpu.CompilerParams(dimension_semantics=sem, vmem_limit_bytes=VMEM_LIMIT)


def _layer_norm(x, g, b):
    mu = jnp.mean(x, axis=-1, keepdims=True)
    xc = x - mu
    var = jnp.mean(xc * xc, axis=-1, keepdims=True)
    return xc * lax.rsqrt(var + LN_EPS) * g + b


def _rms(x, g):
    return x * lax.rsqrt(jnp.mean(x * x, axis=-1, keepdims=True) + RMS_EPS) * g


def _dot(a, b):
    return jnp.dot(a, b, preferred_element_type=F32)


def _dot_nt(a, b):
    return lax.dot_general(a, b, (((1,), (1,)), ((), ())), preferred_element_type=F32)


N_MLA_COLS = Q_LORA_RANK + KV_LORA_RANK + 2 * HEAD_SLOT


def _ln_proj_kernel(x_ref, lg_ref, lb_ref, wall_ref, gq_ref, wq_ref, wqs_ref, gkv_ref, wk_ref, wv_ref,
                    cq_ref, sq_ref, ck_ref, sk_ref, q_ref, k_ref, vt_ref, hy_ref):
    h = _layer_norm(x_ref[...], lg_ref[...], lb_ref[...])
    proj = _dot(h.astype(BF16), wall_ref[...])
    c_q = proj[:, :OFF_CQ]
    c_kv = proj[:, OFF_CQ:OFF_CKV]
    kr = proj[:, OFF_CKV:OFF_CKV + HEAD_SLOT]
    kr_sw = proj[:, OFF_CKV + HEAD_SLOT:N_MLA_COLS]
    hy_ref[...] = proj[:, N_MLA_COLS:].astype(BF16)

    nq = _rms(c_q, gq_ref[...]).astype(BF16)
    qa = _dot(nq, wq_ref[...])
    qb = _dot(nq, wqs_ref[...])
    nkv = _rms(c_kv, gkv_ref[...]).astype(BF16)
    kn = _dot(nkv, wk_ref[...])
    vt_ref[...] = _dot_nt(wv_ref[...], nkv).astype(BF16)
    k_pe = kr * ck_ref[...] + kr_sw * sk_ref[...]
    cq, sq = cq_ref[...], sq_ref[...]
    for hd in range(ATTN_HEADS):
        sl = slice(hd * HEAD_SLOT, (hd + 1) * HEAD_SLOT)
        q_ref[:, sl] = (qa[:, sl] * cq + qb[:, sl] * sq).astype(BF16)
        k_ref[:, sl] = (kn[:, sl] + k_pe).astype(BF16)


def _ln_proj(x2, lg, lb, wall, gq, wq, wqs, gkv, wk, wv, cq, sq, ck, sk, seq, tt):
    t = x2.shape[0]
    npos = seq // tt
    full = lambda a: pl.BlockSpec(a.shape, lambda i: (0,) * a.ndim)
    tab = pl.BlockSpec((tt, HEAD_SLOT), lambda i: (i % npos, 0))
    row = lambda w: pl.BlockSpec((tt, w), lambda i: (i, 0))
    hw = wall.shape[1] - N_MLA_COLS
    return pl.pallas_call(
        _ln_proj_kernel,
        grid=(t // tt,),
        in_specs=[row(D_MODEL), full(lg), full(lb), full(wall), full(gq), full(wq), full(wqs), full(gkv),
                  full(wk), full(wv), tab, tab, tab, tab],
        out_specs=[row(ATTN_HEADS * HEAD_SLOT), row(ATTN_HEADS * HEAD_SLOT),
                   pl.BlockSpec((ATTN_WIDTH, tt), lambda i: (0, i)), row(hw)],
        out_shape=[jax.ShapeDtypeStruct((t, ATTN_HEADS * HEAD_SLOT), BF16),
                   jax.ShapeDtypeStruct((t, ATTN_HEADS * HEAD_SLOT), BF16),
                   jax.ShapeDtypeStruct((ATTN_WIDTH, t), BF16),
                   jax.ShapeDtypeStruct((t, hw), BF16)],
        compiler_params=_cparams(("parallel",)),
        name="ln_proj",
    )(x2, lg, lb, wall, gq, wq, wqs, gkv, wk, wv, cq, sq, ck, sk)


def _attn_kernel(q_ref, k_ref, vt_ref, g_ref, o_ref, s0_ref, m0_ref, s1_ref, m1_ref, *, nsteps):
    n = pl.program_id(0)

    def scores(s_ref, m_ref):
        for hh in range(2):
            sl = slice(hh * HEAD_SLOT, (hh + 1) * HEAD_SLOT)
            s = _dot_nt(k_ref[:, sl], q_ref[:, sl])
            s_ref[hh] = s
            m_ref[hh] = jnp.max(s, axis=0, keepdims=True)

    def finish(s_ref, m_ref):
        outs = []
        ones = jnp.ones((16, vt_ref.shape[1]), BF16)
        for hh in range(2):
            vs = slice(hh * V_HEAD_DIM, (hh + 1) * V_HEAD_DIM)
            p = jnp.exp2(s_ref[hh] - m_ref[hh]).astype(BF16)
            ov = _dot(jnp.concatenate([vt_ref[vs, :], ones], axis=0), p)
            o = ov[:V_HEAD_DIM] / ov[V_HEAD_DIM:V_HEAD_DIM + 1]
            ms = jnp.mean(o * o, axis=0, keepdims=True)
            outs.append(o * lax.rsqrt(ms + RMS_EPS) * g_ref[vs, :])
        o_ref[...] = jnp.concatenate(outs, axis=0).T.astype(BF16)

    @pl.when(n == 0)
    def _():
        s1_ref[...] = jnp.zeros_like(s1_ref)
        m1_ref[...] = jnp.zeros_like(m1_ref)

    @pl.when((n < nsteps) & (n % 2 == 0))
    def _():
        finish(s1_ref, m1_ref)
        scores(s0_ref, m0_ref)

    @pl.when((n < nsteps) & (n % 2 == 1))
    def _():
        finish(s0_ref, m0_ref)
        scores(s1_ref, m1_ref)

    @pl.when(n == nsteps)
    def _():
        if nsteps % 2 == 0:
            finish(s1_ref, m1_ref)
        else:
            finish(s0_ref, m0_ref)


def _attention(q, k, vt, g, batch, seq, tq):
    t = q.shape[0]
    nq = seq // tq
    npair = ATTN_HEADS // 2
    nsteps = batch * npair * nq

    def tile(n):
        n = jnp.clip(n, 0, nsteps - 1)
        return n // (npair * nq), (n // nq) % npair, n % nq

    def q_map(n):
        b, p, i = tile(n)
        return b * nq + i, p

    def k_map(n):
        b, p, _ = tile(n)
        return b, p

    def vt_map(n):
        b, p, _ = tile(n - 1)
        return p, b

    def g_map(n):
        return tile(n - 1)[1], 0

    def o_map(n):
        b, p, i = tile(n - 1)
        return b * nq + i, p

    return pl.pallas_call(
        functools.partial(_attn_kernel, nsteps=nsteps),
        grid=(nsteps + 1,),
        in_specs=[pl.BlockSpec((tq, 2 * HEAD_SLOT), q_map),
                  pl.BlockSpec((seq, 2 * HEAD_SLOT), k_map),
                  pl.BlockSpec((2 * V_HEAD_DIM, seq), vt_map),
                  pl.BlockSpec((2 * V_HEAD_DIM, 1), g_map)],
        out_specs=pl.BlockSpec((tq, 2 * V_HEAD_DIM), o_map),
        out_shape=jax.ShapeDtypeStruct((t, ATTN_WIDTH), BF16),
        scratch_shapes=[pltpu.VMEM((2, seq, tq), F32), pltpu.VMEM((2, 1, tq), F32),
                        pltpu.VMEM((2, seq, tq), F32), pltpu.VMEM((2, 1, tq), F32)],
        compiler_params=_cparams(("arbitrary",)),
        name="attention",
    )(q, k, vt, g)


def _short_conv_kernel(prev_ref, x_ref, next_ref, w_ref, b_ref, o_ref):
    j = pl.program_id(1)
    nj = pl.num_programs(1)
    x = x_ref[0].astype(F32)
    rows = x.shape[0]
    before = jnp.where(j > 0, prev_ref[0, 7:8, :].astype(F32), 0.0)
    after = jnp.where(j < nj - 1, next_ref[0, 0:1, :].astype(F32), 0.0)
    rid = lax.broadcasted_iota(I32, x.shape, 0)
    xm = jnp.where(rid == 0, before, pltpu.roll(x, 1, 0))
    xp = jnp.where(rid == rows - 1, after, pltpu.roll(x, rows - 1, 0))
    u = b_ref[...] + xm * w_ref[0:1, :] + x * w_ref[1:2, :] + xp * w_ref[2:3, :]
    o_ref[...] = u.T.astype(BF16)


def _short_conv(hy3, w, b):
    batch, seq, width = hy3.shape
    nj = seq // TOEP
    sub = TOEP // 8
    last8 = seq // 8 - 1
    return pl.pallas_call(
        _short_conv_kernel,
        grid=(batch, nj),
        in_specs=[pl.BlockSpec((1, 8, width), lambda bb, j: (bb, jnp.maximum(j * sub - 1, 0), 0)),
                  pl.BlockSpec((1, TOEP, width), lambda bb, j: (bb, j, 0)),
                  pl.BlockSpec((1, 8, width), lambda bb, j: (bb, jnp.minimum((j + 1) * sub, last8), 0)),
                  pl.BlockSpec(w.shape, lambda bb, j: (0, 0)),
                  pl.BlockSpec(b.shape, lambda bb, j: (0, 0))],
        out_specs=pl.BlockSpec((width, TOEP), lambda bb, j: (0, j * batch + bb)),
        out_shape=jax.ShapeDtypeStruct((width, nj * batch * TOEP), BF16),
        compiler_params=_cparams(("parallel", "parallel")),
        name="short_conv",
    )(hy3, hy3, hy3, w, b)


def _filter_kernel(z_ref, w1_ref, b1_ref, f1_ref, w2_ref, b2_ref, f2_ref, w3_ref, dl_ref, g_ref):
    hp = lax.Precision.HIGHEST
    z = z_ref[...]
    n = z.shape[1]
    half = n // 2
    h1 = jnp.sin(f1_ref[...] * (jnp.dot(w1_ref[...], z, precision=hp, preferred_element_type=F32) + b1_ref[...]))
    h2 = jnp.sin(f2_ref[...] * (jnp.dot(w2_ref[...], h1, precision=hp, preferred_element_type=F32) + b2_ref[...]))
    decay = jnp.exp(-z[0:1, :] * dl_ref[...])
    lane = lax.broadcasted_iota(I32, (1, n), 1)
    masks = (lane >= half, (lane >= 1) & (lane <= half))
    out = None
    for d in range(2):
        hd = jnp.dot(w3_ref[0, d], h2, precision=hp, preferred_element_type=F32) * decay
        hd = jnp.where(masks[d], hd, 0.0)
        hd = hd / (jnp.sum(jnp.abs(hd), axis=1, keepdims=True) + 1e-6)
        out = hd if out is None else out + hd
    g_ref[0] = out


def _filters(zt, w1t, b1, f1, w2t, b2, f2, w3t, dl):
    n = zt.shape[1]
    cb = 128
    full = lambda a: pl.BlockSpec(a.shape, lambda o, c: (0,) * a.ndim)
    return pl.pallas_call(
        _filter_kernel,
        grid=(HY_ORDER, HY_WIDTH // cb),
        in_specs=[full(zt), full(w1t), full(b1), full(f1), full(w2t), full(b2), full(f2),
                  pl.BlockSpec((1, 2, cb, HY_FILTER_HIDDEN), lambda o, c: (o, 0, c, 0)),
                  pl.BlockSpec((cb, 1), lambda o, c: (c, 0))],
        out_specs=pl.BlockSpec((1, cb, n), lambda o, c: (o, c, 0)),
        out_shape=jax.ShapeDtypeStruct((HY_ORDER, HY_WIDTH, n), F32),
        compiler_params=_cparams(("parallel", "parallel")),
        name="filters",
    )(zt, w1t, b1, f1, w2t, b2, f2, w3t, dl)


def _hyena_kernel(bias_ref, v_ref, x1_ref, x2_ref, g_ref, o_ref, acc_ref, *, batch, nblk, cb):
    rows = batch * nblk
    seq = nblk * TOEP
    c0 = pl.program_id(0) * cb
    gates = (x1_ref, x2_ref)

    def per_channel(c, carry):
        zf = v_ref[c].astype(F32)
        for o in range(HY_ORDER):
            zb = zf.astype(BF16)
            acc_ref[...] = jnp.zeros_like(acc_ref)
            for d in range(-(nblk - 1), nblk):
                base = seq + TOEP * d - TOEP
                win = g_ref[o, c, :, pl.ds(base, 2 * TOEP)]
                w = pltpu.roll(jnp.broadcast_to(win, (TOEP, 2 * TOEP)), TOEP, 1, stride=1, stride_axis=0)
                w = w[:, :TOEP].astype(BF16)
                n = rows - batch * abs(d)
                if d >= 0:
                    acc_ref[batch * d:, :] += _dot(zb[:n], w)
                else:
                    acc_ref[:n, :] += _dot(zb[batch * (-d):], w)
            zf = gates[o][c].astype(F32) * (acc_ref[...] + bias_ref[o, c0 + c] * zf)
        o_ref[c] = zf.astype(BF16)
        return carry

    lax.fori_loop(0, cb, per_channel, 0)


def _hyena(bias, ut, g4, batch, nblk, cb):
    rows = batch * nblk
    nc = HY_WIDTH // cb
    kern = functools.partial(_hyena_kernel, batch=batch, nblk=nblk, cb=cb)
    blk = lambda off: pl.BlockSpec((cb, rows, TOEP), lambda i: (off * nc + i, 0, 0))
    return pl.pallas_call(
        kern,
        grid=(nc,),
        in_specs=[pl.BlockSpec(memory_space=pltpu.SMEM), blk(0), blk(1), blk(2),
                  pl.BlockSpec((HY_ORDER, cb, 1, g4.shape[3]), lambda i: (0, i, 0, 0))],
        out_specs=pl.BlockSpec((cb, rows, TOEP), lambda i: (i, 0, 0)),
        out_shape=jax.ShapeDtypeStruct((HY_WIDTH, rows, TOEP), BF16),
        scratch_shapes=[pltpu.VMEM((rows, TOEP), F32)],
        compiler_params=_cparams(("parallel",)),
        name="hyena",
    )(bias, ut, ut, ut, g4)


def _mix_kernel(x_ref, lg_ref, lb_ref, a_ref, yt_ref, hg_ref, wo_ref, mg_ref, mb_ref, h2_ref, h2b_ref, cat_ref):
    y = yt_ref[...].astype(F32).T
    cat_ref[:, :ATTN_WIDTH] = a_ref[...]
    gw = HY_WIDTH // HY_GROUPS
    for g in range(HY_GROUPS):
        sl = slice(g * gw, (g + 1) * gw)
        cat_ref[:, ATTN_WIDTH + g * gw:ATTN_WIDTH + (g + 1) * gw] = _rms(y[:, sl], hg_ref[:, sl]).astype(BF16)
    mix = _dot(cat_ref[...], wo_ref[...])
    h = _layer_norm(x_ref[...], lg_ref[...], lb_ref[...])
    h2 = _layer_norm(ALPHA * h + mix, mg_ref[...], mb_ref[...])
    h2_ref[...] = h2
    h2b_ref[...] = h2.astype(BF16)


def _mix(x2, lg, lb, a, yt2, hg, wo, mg, mb, batch, nblk):
    t = x2.shape[0]
    full = lambda w: pl.BlockSpec(w.shape, lambda i: (0,) * w.ndim)
    row = lambda w: pl.BlockSpec((TOEP, w), lambda i: (i, 0))
    return pl.pallas_call(
        _mix_kernel,
        grid=(t // TOEP,),
        in_specs=[row(D_MODEL), full(lg), full(lb), row(ATTN_WIDTH),
                  pl.BlockSpec((HY_WIDTH, TOEP), lambda i: (0, (i % nblk) * batch + i // nblk)),
                  full(hg), full(wo), full(mg), full(mb)],
        out_specs=[row(D_MODEL), row(D_MODEL)],
        out_shape=[jax.ShapeDtypeStruct((t, D_MODEL), F32), jax.ShapeDtypeStruct((t, D_MODEL), BF16)],
        scratch_shapes=[pltpu.VMEM((TOEP, D_MODEL), BF16)],
        compiler_params=_cparams(("parallel",)),
        name="mix",
    )(x2, lg, lb, a, yt2, hg, wo, mg, mb)


_CAND_COLS = tuple(PEER_TOPK // (r + 1) for r in range(PEER_TOPK))


def _top16(s, val_ref, idx_ref):
    kio = lax.broadcasted_iota(I32, s.shape, 0)
    vals = s
    for r in range(PEER_TOPK):
        m = jnp.max(vals, axis=0, keepdims=True)
        idx = jnp.min(jnp.where(vals == m, kio, PEER_NKEYS), axis=0, keepdims=True)
        val_ref[r:r + 1, :] = m
        idx_ref[r:r + 1, :] = idx
        vals = jnp.where(kio == idx, -jnp.inf, vals)


def _route_kernel(h_ref, wq_ref, keys_ref, isel_ref, jsel_ref, gate_ref,
                  v1_ref, i1_ref, v2_ref, i2_ref, code_ref, best_ref, gsel_ref):
    tt = h_ref.shape[0]
    qt = _dot_nt(wq_ref[...], h_ref[...])
    half = PEER_DK // 2
    neg = -jnp.inf
    for hd in range(PEER_HEADS):
        for p, (vr, ir) in enumerate(((v1_ref, i1_ref), (v2_ref, i2_ref))):
            qp = qt[hd * PEER_DK + p * half: hd * PEER_DK + (p + 1) * half, :].astype(BF16)
            _top16(_dot(keys_ref[hd, p], qp), vr, ir)
        cands, codes = [], []
        for r in range(8):
            ncol = 16 if r == 0 else 8
            cv = v1_ref[r:r + 1, :] + v2_ref[0:ncol, :]
            cio = lax.broadcasted_iota(I32, cv.shape, 0)
            cands.append(jnp.where(cio < _CAND_COLS[r], cv, neg))
            codes.append(i1_ref[r:r + 1, :] * PEER_NKEYS + i2_ref[0:ncol, :])
        cands.append(v1_ref[8:16, :] + v2_ref[0:1, :])
        codes.append(i1_ref[8:16, :] * PEER_NKEYS + i2_ref[0:1, :])
        cand = jnp.concatenate(cands, axis=0)
        code = jnp.concatenate(codes, axis=0)
        rio = lax.broadcasted_iota(I32, cand.shape, 0)
        for kk in range(PEER_TOPK):
            m = jnp.max(cand, axis=0, keepdims=True)
            pos = jnp.min(jnp.where(cand == m, rio, cand.shape[0]), axis=0, keepdims=True)
            sel = rio == pos
            best_ref[kk:kk + 1, :] = m
            code_ref[hd * PEER_TOPK + kk: hd * PEER_TOPK + kk + 1, :] = jnp.max(jnp.where(sel, code, -1), axis=0, keepdims=True)
            cand = jnp.where(sel, neg, cand)
        best = best_ref[...]
        e = jnp.exp(best - best[0:1, :])
        gsel_ref[hd * PEER_TOPK:(hd + 1) * PEER_TOPK, :] = e / jnp.sum(e, axis=0, keepdims=True)
    gate_ref[...] = gsel_ref[...].T
    code_t = code_ref[...].T
    isel_ref[...] = code_t >> 7
    jsel_ref[...] = code_t & (PEER_NKEYS - 1)


def _route(h2b, wqt, keys, tt):
    t = h2b.shape[0]
    full = lambda w: pl.BlockSpec(w.shape, lambda i: (0,) * w.ndim)
    sel = pl.BlockSpec((tt, N_SEL), lambda i: (i, 0))
    return pl.pallas_call(
        _route_kernel,
        grid=(t // tt,),
        in_specs=[pl.BlockSpec((tt, D_MODEL), lambda i: (i, 0)), full(wqt), full(keys)],
        out_specs=[sel, sel, sel],
        out_shape=[jax.ShapeDtypeStruct((t, N_SEL), I32), jax.ShapeDtypeStruct((t, N_SEL), I32),
                   jax.ShapeDtypeStruct((t, N_SEL), F32)],
        scratch_shapes=[pltpu.VMEM((PEER_TOPK, tt), F32), pltpu.VMEM((PEER_TOPK, tt), I32),
                        pltpu.VMEM((PEER_TOPK, tt), F32), pltpu.VMEM((PEER_TOPK, tt), I32),
                        pltpu.VMEM((N_SEL, tt), I32), pltpu.VMEM((PEER_TOPK, tt), F32),
                        pltpu.VMEM((N_SEL, tt), F32)],
        compiler_params=_cparams(("parallel",)),
        name="route",
    )(h2b, wqt, keys)


def _peer_up_kernel(x_ref, u_ref, isel_ref, jsel_ref, gate_ref, w_ref, a0_ref, a1_ref, *, ne):
    i = pl.program_id(0)
    e = pl.program_id(1)
    tt = x_ref.shape[0]
    nsub = u_ref.shape[0] // PEER_NKEYS

    def pick(e_src, src):
        for r0 in range(0, tt, GATHER_ROWS):
            rows = slice(r0, r0 + GATHER_ROWS)
            isel = isel_ref[rows, :]
            jsel = jsel_ref[rows, :]
            acc = w_ref[rows, :]
            for ii in range(nsub):
                got = jnp.take_along_axis(src[rows, ii * PEER_NKEYS:(ii + 1) * PEER_NKEYS], jsel, axis=1)
                acc = jnp.where(isel == e_src * nsub + ii, got, acc)
            w_ref[rows, :] = acc

    @pl.when((i == 0) & (e == 0))
    def _():
        a1_ref[...] = jnp.zeros_like(a1_ref)

    @pl.when(e == 0)
    def _():
        w_ref[...] = jnp.zeros_like(w_ref)

    @pl.when((e < ne) & (e % 2 == 0))
    def _():
        a0_ref[...] = _dot_nt(x_ref[...], u_ref[...])
        pick(e - 1, a1_ref)

    @pl.when((e < ne) & (e % 2 == 1))
    def _():
        a1_ref[...] = _dot_nt(x_ref[...], u_ref[...])
        pick(e - 1, a0_ref)

    @pl.when(e == ne)
    def _():
        pick(e - 1, a1_ref if ne % 2 == 0 else a0_ref)
        s = w_ref[...]
        gelu = 0.5 * s * (1.0 + lax.erf(s * (2.0 ** -0.5)))
        w_ref[...] = gate_ref[...] * gelu


def _peer_up(h2b, ub, isel, jsel, gate, tt, eb):
    t = h2b.shape[0]
    ne = ub.shape[0] // eb
    sel = pl.BlockSpec((tt, N_SEL), lambda i, e: (i, 0))
    return pl.pallas_call(
        functools.partial(_peer_up_kernel, ne=ne),
        grid=(t // tt, ne + 1),
        in_specs=[pl.BlockSpec((tt, D_MODEL), lambda i, e: (i, 0)),
                  pl.BlockSpec((eb, D_MODEL), lambda i, e: (jnp.minimum(e, ne - 1), 0)), sel, sel, sel],
        out_specs=sel,
        out_shape=jax.ShapeDtypeStruct((t, N_SEL), F32),
        scratch_shapes=[pltpu.VMEM((tt, eb), F32), pltpu.VMEM((tt, eb), F32)],
        compiler_params=_cparams(("arbitrary", "arbitrary")),
        name="peer_up",
    )(h2b, ub, isel, jsel, gate)


def _peer_down_kernel(isel_ref, jsel_ref, w_ref, v_ref, h2_ref, lg_ref, lb_ref, o_ref, wd_ref, acc_ref, *, ib):
    e = pl.program_id(1)
    tt = isel_ref.shape[0]
    hi_mask = jnp.uint32(0xFFFF0000)

    @pl.when(e == 0)
    def _():
        acc_ref[...] = jnp.zeros_like(acc_ref)
        sub = lax.broadcasted_iota(I32, (PEER_NKEYS, N_SEL), 0)

        half = PEER_NKEYS // 2

        def per_group(gi, carry):
            base = pl.multiple_of(gi * TOKEN_GROUP, TOKEN_GROUP)
            isb = isel_ref[pl.ds(base, TOKEN_GROUP), :]
            jsb = jsel_ref[pl.ds(base, TOKEN_GROUP), :]
            wb = w_ref[pl.ds(base, TOKEN_GROUP), :]
            for r in range(TOKEN_GROUP):
                pt = jnp.where(sub == isb[r:r + 1, :], wb[r:r + 1, :], 0.0).astype(BF16)
                qt = jnp.where(sub == jsb[r:r + 1, :], 1.0, 0.0).astype(BF16)
                gb = pltpu.bitcast(_dot_nt(pt, qt), jnp.uint32)
                packed = (gb[half:] & hi_mask) | (gb[:half] >> 16)
                wd_ref[pl.ds(pl.multiple_of((base + r) * W_PITCH, 8), half), :] = packed
            return carry

        lax.fori_loop(0, tt // TOKEN_GROUP, per_group, 0)

    parts = []
    for ii in range(ib):
        pk = wd_ref[pl.ds(e * ib + ii, tt, stride=W_PITCH), :]
        parts.append(pltpu.bitcast(pk << 16, F32).astype(BF16))
        parts.append(pltpu.bitcast(pk & hi_mask, F32).astype(BF16))
    acc_ref[...] += _dot(jnp.concatenate(parts, axis=1), v_ref[...])

    @pl.when(e == pl.num_programs(1) - 1)
    def _():
        o_ref[...] = _layer_norm(ALPHA * h2_ref[...] + acc_ref[...], lg_ref[...], lb_ref[...])


def _peer_down(isel, jsel, w, vperm, h2, lg, lb, tt, ib):
    t = h2.shape[0]
    ne = (PEER_NKEYS // 2) // ib
    vb = ib * 2 * PEER_NKEYS
    sel = pl.BlockSpec((tt, N_SEL), lambda i, e: (i, 0))
    full = lambda a: pl.BlockSpec(a.shape, lambda i, e: (0,) * a.ndim)
    return pl.pallas_call(
        functools.partial(_peer_down_kernel, ib=ib),
        grid=(t // tt, ne),
        in_specs=[sel, sel, sel, pl.BlockSpec((vb, D_MODEL), lambda i, e: (e, 0)),
                  pl.BlockSpec((tt, D_MODEL), lambda i, e: (i, 0)), full(lg), full(lb)],
        out_specs=pl.BlockSpec((tt, D_MODEL), lambda i, e: (i, 0)),
        out_shape=jax.ShapeDtypeStruct((t, D_MODEL), F32),
        scratch_shapes=[pltpu.VMEM((tt * W_PITCH, N_SEL), jnp.uint32), pltpu.VMEM((tt, D_MODEL), F32)],
        compiler_params=_cparams(("parallel", "arbitrary")),
        name="peer_down",
    )(isel, jsel, w, vperm, h2, lg, lb)


def _rope_tables(seq):
    half = QK_ROPE_DIM // 2
    inv = 1.0 / (ROPE_THETA ** (jnp.arange(0, QK_ROPE_DIM, 2, dtype=F32) / QK_ROPE_DIM))
    ang = jnp.arange(seq, dtype=F32)[:, None] * inv[None, :]
    cos, sin = jnp.cos(ang), jnp.sin(ang)
    c32 = jnp.concatenate([cos, cos], axis=1)
    s32 = jnp.concatenate([-sin, sin], axis=1)
    z = lambda w: jnp.zeros((seq, w), F32)
    scale = (QK_NOPE_DIM + QK_ROPE_DIM) ** -0.5 * math.log2(math.e)
    pad = HEAD_SLOT - QK_NOPE_DIM - QK_ROPE_DIM
    cq = jnp.concatenate([jnp.full((seq, QK_NOPE_DIM), scale, F32), scale * c32, z(pad)], axis=1)
    sq = jnp.concatenate([z(QK_NOPE_DIM), scale * s32, z(pad)], axis=1)
    ck = jnp.concatenate([z(QK_NOPE_DIM), c32, z(pad)], axis=1)
    sk = jnp.concatenate([z(QK_NOPE_DIM), s32, z(pad)], axis=1)
    return cq, sq, ck, sk


def _swap_halves(w):
    half = w.shape[-1] // 2
    return jnp.concatenate([w[..., half:], w[..., :half]], axis=-1)


def _slot(w, offset):
    return jnp.pad(w, ((0, 0), (offset, HEAD_SLOT - offset - w.shape[1])))


def _position_features(seq):
    t = jnp.linspace(0.0, 1.0, seq, dtype=F32)[:, None]
    bands = (HY_EMB_DIM - 1) // 2
    w = 2.0 * math.pi * jnp.arange(seq, dtype=F32) / seq
    f = jnp.linspace(1e-4, bands - 1, bands, dtype=F32)
    ang = w[:, None] * f[None, :]
    return jnp.concatenate([t, jnp.cos(ang), -jnp.sin(ang)], axis=-1)


def kernel(x, emb_ln_g, emb_ln_b, w_in, q_norm_g, w_uq, kv_norm_g, w_ukv, hy_short_w, hy_short_b, hy_filt_w1,
           hy_filt_b1, hy_filt_freq1, hy_filt_w2, hy_filt_b2, hy_filt_freq2, hy_filt_w3, hy_bias, attn_out_g,
           hy_out_g, w_o, ln_mix_g, ln_mix_b, peer_wq, peer_sub_keys, peer_u, peer_v, ln_ffn_g, ln_ffn_b):
    batch, seq, _ = x.shape
    assert w_in.shape[0] == DEPTH == 1 and seq % TOEP == 0
    t = batch * seq
    nblk = seq // TOEP
    r2 = lambda a: a.reshape(1, -1)
    x2 = x.reshape(t, D_MODEL)
    lg, lb = r2(emb_ln_g), r2(emb_ln_b)

    wi = w_in[0]
    w_kr = wi[:, OFF_CKV:OFF_KR]
    wall = jnp.concatenate([wi[:, :OFF_CKV], _slot(w_kr, QK_NOPE_DIM), _slot(_swap_halves(w_kr), QK_NOPE_DIM),
                            wi[:, OFF_KR:]], axis=1).astype(BF16)
    dq = QK_NOPE_DIM + QK_ROPE_DIM
    wuq = w_uq[0].reshape(Q_LORA_RANK, ATTN_HEADS, dq)
    wq = jnp.pad(wuq, ((0, 0), (0, 0), (0, HEAD_SLOT - dq))).reshape(Q_LORA_RANK, -1).astype(BF16)
    wqs = jnp.pad(_swap_halves(wuq[..., QK_NOPE_DIM:]),
                  ((0, 0), (0, 0), (QK_NOPE_DIM, HEAD_SLOT - dq))).reshape(Q_LORA_RANK, -1).astype(BF16)
    wukv = w_ukv[0].reshape(KV_LORA_RANK, ATTN_HEADS, QK_NOPE_DIM + V_HEAD_DIM)
    wk = jnp.pad(wukv[..., :QK_NOPE_DIM],
                 ((0, 0), (0, 0), (0, HEAD_SLOT - QK_NOPE_DIM))).reshape(KV_LORA_RANK, -1).astype(BF16)
    wv = wukv[..., QK_NOPE_DIM:].reshape(KV_LORA_RANK, -1).T.astype(BF16)
    cq, sq, ck, sk = _rope_tables(seq)

    tt_proj = min(512, seq)
    q, k, vt, hy = _ln_proj(x2, lg, lb, wall, r2(q_norm_g[0]), wq, wqs, r2(kv_norm_g[0]), wk, wv,
                           cq, sq, ck, sk, seq, tt_proj)

    a = _attention(q, k, vt, attn_out_g[0].reshape(-1, 1), batch, seq, min(256, seq))

    ut = _short_conv(hy.reshape(batch, seq, -1), hy_short_w[0], r2(hy_short_b[0]))
    ut = ut.reshape((HY_ORDER + 1) * HY_WIDTH, nblk * batch, TOEP)

    lag = jnp.abs(jnp.arange(2 * seq) - seq)
    feats = _position_features(seq)
    zt = jnp.pad(feats[jnp.minimum(lag, seq - 1)].T, ((0, HY_FILTER_HIDDEN - HY_EMB_DIM), (0, 0)))
    col = lambda a_: a_.reshape(-1, 1)
    w1t = jnp.pad(hy_filt_w1[0].T, ((0, 0), (0, HY_FILTER_HIDDEN - HY_EMB_DIM)))
    w3t = hy_filt_w3[0].T.reshape(HY_ORDER, 2, HY_WIDTH, HY_FILTER_HIDDEN)
    deltas = jnp.abs(jnp.linspace(math.log(HY_TARGET) / HY_SLOW_DECAY, math.log(HY_TARGET) / HY_FAST_DECAY,
                                  HY_WIDTH, dtype=F32))
    g = _filters(zt, w1t, col(hy_filt_b1[0]), col(hy_filt_freq1[0]), hy_filt_w2[0].T, col(hy_filt_b2[0]),
                 col(hy_filt_freq2[0]), w3t, col(deltas))
    yt = _hyena(hy_bias[0], ut, g.reshape(HY_ORDER, HY_WIDTH, 1, 2 * seq), batch, nblk, 8)

    h2, h2b = _mix(x2, lg, lb, a, yt.reshape(HY_WIDTH, nblk * batch * TOEP), r2(hy_out_g[0]),
                   w_o[0].astype(BF16), r2(ln_mix_g[0]), r2(ln_mix_b[0]), batch, nblk)

    isel, jsel, gate = _route(h2b, peer_wq[0].T.astype(BF16), peer_sub_keys[0].astype(BF16), min(256, t))
    w = _peer_up(h2b, peer_u[0].astype(BF16), isel, jsel, gate, min(1024, t), 1024)
    ib = 8
    half = PEER_NKEYS // 2
    vperm = peer_v[0].astype(BF16).reshape(2, half // ib, ib, PEER_NKEYS, D_MODEL)
    vperm = vperm.transpose(1, 2, 0, 3, 4).reshape(PEER_NKEYS * PEER_NKEYS, D_MODEL)
    out = _peer_down(isel, jsel, w, vperm, h2, r2(ln_ffn_g[0]), r2(ln_ffn_b[0]), min(512, t), ib)
    return out.reshape(batch, seq, D_MODEL)
```

```python
import functools
import math

import jax
import jax.numpy as jnp
from jax import lax
from jax.experimental import pallas as pl
from jax.experimental.pallas import tpu as pltpu

F32 = jnp.float32
BF16 = jnp.bfloat16
I32 = jnp.int32

D_MODEL = 1024
ATTN_HEADS = 8
QK_NOPE_DIM = 64
QK_ROPE_DIM = 32
V_HEAD_DIM = 64
Q_LORA_RANK = 256
KV_LORA_RANK = 128
ATTN_WIDTH = ATTN_HEADS * V_HEAD_DIM
ROPE_THETA = 10000.0
HY_WIDTH = D_MODEL - ATTN_WIDTH
HY_ORDER = 2
HY_GROUPS = 8
HY_SHORT = 3
HY_EMB_DIM = 33
HY_FILTER_HIDDEN = 64
HY_FAST_DECAY = 0.3
HY_SLOW_DECAY = 1.5
HY_TARGET = 1e-2
OFF_CQ = Q_LORA_RANK
OFF_CKV = OFF_CQ + KV_LORA_RANK
OFF_KR = OFF_CKV + QK_ROPE_DIM
PEER_HEADS = 8
PEER_NKEYS = 128
PEER_DK = 128
PEER_TOPK = 16
DEPTH = 1
ALPHA = (2 * DEPTH) ** 0.25
LN_EPS = 1e-5
RMS_EPS = 1e-6

LANES = 128
HEAD_SLOT = 128
TOEP = 256
N_SEL = PEER_HEADS * PEER_TOPK
W_PITCH = 72
TOKEN_GROUP = 16
GATHER_ROWS = 8
STEP_SLICES = 4
VMEM_LIMIT = 56 * 1024 * 1024


def _cparams(sem):
    return pltpu.CompilerParams(dimension_semantics=sem, vmem_limit_bytes=VMEM_LIMIT)


def _layer_norm(x, g, b):
    mu = jnp.mean(x, axis=-1, keepdims=True)
    xc = x - mu
    var = jnp.mean(xc * xc, axis=-1, keepdims=True)
    return xc * lax.rsqrt(var + LN_EPS) * g + b


def _rms(x, g):
    return x * lax.rsqrt(jnp.mean(x * x, axis=-1, keepdims=True) + RMS_EPS) * g


def _dot(a, b):
    return jnp.dot(a, b, preferred_element_type=F32)


def _dot_nt(a, b):
    return lax.dot_general(a, b, (((1,), (1,)), ((), ())), preferred_element_type=F32)


N_MLA_COLS = Q_LORA_RANK + KV_LORA_RANK + 2 * HEAD_SLOT


def _ln_proj_kernel(x_ref, lg_ref, lb_ref, wall_ref, gq_ref, wq_ref, wqs_ref, gkv_ref, wk_ref, wv_ref,
                    cq_ref, sq_ref, ck_ref, sk_ref, q_ref, k_ref, vt_ref, hy_ref):
    h = _layer_norm(x_ref[...], lg_ref[...], lb_ref[...])
    proj = _dot(h.astype(BF16), wall_ref[...])
    c_q = proj[:, :OFF_CQ]
    c_kv = proj[:, OFF_CQ:OFF_CKV]
    kr = proj[:, OFF_CKV:OFF_CKV + HEAD_SLOT]
    kr_sw = proj[:, OFF_CKV + HEAD_SLOT:N_MLA_COLS]
    hy_ref[...] = proj[:, N_MLA_COLS:].astype(BF16)

    nq = _rms(c_q, gq_ref[...]).astype(BF16)
    qa = _dot(nq, wq_ref[...])
    qb = _dot(nq, wqs_ref[...])
    nkv = _rms(c_kv, gkv_ref[...]).astype(BF16)
    kn = _dot(nkv, wk_ref[...])
    vt_ref[...] = _dot_nt(wv_ref[...], nkv).astype(BF16)
    k_pe = kr * ck_ref[...] + kr_sw * sk_ref[...]
    cq, sq = cq_ref[...], sq_ref[...]
    for hd in range(ATTN_HEADS):
        sl = slice(hd * HEAD_SLOT, (hd + 1) * HEAD_SLOT)
        q_ref[:, sl] = (qa[:, sl] * cq + qb[:, sl] * sq).astype(BF16)
        k_ref[:, sl] = (kn[:, sl] + k_pe).astype(BF16)


def _ln_proj(x2, lg, lb, wall, gq, wq, wqs, gkv, wk, wv, cq, sq, ck, sk, seq, tt):
    t = x2.shape[0]
    npos = seq // tt
    full = lambda a: pl.BlockSpec(a.shape, lambda i: (0,) * a.ndim)
    tab = pl.BlockSpec((tt, HEAD_SLOT), lambda i: (i % npos, 0))
    row = lambda w: pl.BlockSpec((tt, w), lambda i: (i, 0))
    hw = wall.shape[1] - N_MLA_COLS
    return pl.pallas_call(
        _ln_proj_kernel,
        grid=(t // tt,),
        in_specs=[row(D_MODEL), full(lg), full(lb), full(wall), full(gq), full(wq), full(wqs), full(gkv),
                  full(wk), full(wv), tab, tab, tab, tab],
        out_specs=[row(ATTN_HEADS * HEAD_SLOT), row(ATTN_HEADS * HEAD_SLOT),
                   pl.BlockSpec((ATTN_WIDTH, tt), lambda i: (0, i)), row(hw)],
        out_shape=[jax.ShapeDtypeStruct((t, ATTN_HEADS * HEAD_SLOT), BF16),
                   jax.ShapeDtypeStruct((t, ATTN_HEADS * HEAD_SLOT), BF16),
                   jax.ShapeDtypeStruct((ATTN_WIDTH, t), BF16),
                   jax.ShapeDtypeStruct((t, hw), BF16)],
        compiler_params=_cparams(("parallel",)),
        name="ln_proj",
    )(x2, lg, lb, wall, gq, wq, wqs, gkv, wk, wv, cq, sq, ck, sk)


def _attn_kernel(q_ref, k_ref, vt_ref, g_ref, o_ref, s0_ref, m0_ref, s1_ref, m1_ref, *, nsteps):
    n = pl.program_id(0)

    def scores(s_ref, m_ref):
        for hh in range(2):
            sl = slice(hh * HEAD_SLOT, (hh + 1) * HEAD_SLOT)
            s = _dot_nt(k_ref[:, sl], q_ref[:, sl])
            s_ref[hh] = s
            m_ref[hh] = jnp.max(s, axis=0, keepdims=True)

    def finish(s_ref, m_ref):
        outs = []
        ones = jnp.ones((16, vt_ref.shape[1]), BF16)
        for hh in range(2):
            vs = slice(hh * V_HEAD_DIM, (hh + 1) * V_HEAD_DIM)
            p = jnp.exp2(s_ref[hh] - m_ref[hh]).astype(BF16)
            ov = _dot(jnp.concatenate([vt_ref[vs, :], ones], axis=0), p)
            o = ov[:V_HEAD_DIM] / ov[V_HEAD_DIM:V_HEAD_DIM + 1]
            ms = jnp.mean(o * o, axis=0, keepdims=True)
            outs.append(o * lax.rsqrt(ms + RMS_EPS) * g_ref[vs, :])
        o_ref[...] = jnp.concatenate(outs, axis=0).T.astype(BF16)

    @pl.when(n == 0)
    def _():
        s1_ref[...] = jnp.zeros_like(s1_ref)
        m1_ref[...] = jnp.zeros_like(m1_ref)

    @pl.when((n < nsteps) & (n % 2 == 0))
    def _():
        finish(s1_ref, m1_ref)
        scores(s0_ref, m0_ref)

    @pl.when((n < nsteps) & (n % 2 == 1))
    def _():
        finish(s0_ref, m0_ref)
        scores(s1_ref, m1_ref)

    @pl.when(n == nsteps)
    def _():
        if nsteps % 2 == 0:
            finish(s1_ref, m1_ref)
        else:
            finish(s0_ref, m0_ref)


def _attention(q, k, vt, g, batch, seq, tq):
    t = q.shape[0]
    nq = seq // tq
    npair = ATTN_HEADS // 2
    nsteps = batch * npair * nq

    def tile(n):
        n = jnp.clip(n, 0, nsteps - 1)
        return n // (npair * nq), (n // nq) % npair, n % nq

    def q_map(n):
        b, p, i = tile(n)
        return b * nq + i, p

    def k_map(n):
        b, p, _ = tile(n)
        return b, p

    def vt_map(n):
        b, p, _ = tile(n - 1)
        return p, b

    def g_map(n):
        return tile(n - 1)[1], 0

    def o_map(n):
        b, p, i = tile(n - 1)
        return b * nq + i, p

    return pl.pallas_call(
        functools.partial(_attn_kernel, nsteps=nsteps),
        grid=(nsteps + 1,),
        in_specs=[pl.BlockSpec((tq, 2 * HEAD_SLOT), q_map),
                  pl.BlockSpec((seq, 2 * HEAD_SLOT), k_map),
                  pl.BlockSpec((2 * V_HEAD_DIM, seq), vt_map),
                  pl.BlockSpec((2 * V_HEAD_DIM, 1), g_map)],
        out_specs=pl.BlockSpec((tq, 2 * V_HEAD_DIM), o_map),
        out_shape=jax.ShapeDtypeStruct((t, ATTN_WIDTH), BF16),
        scratch_shapes=[pltpu.VMEM((2, seq, tq), F32), pltpu.VMEM((2, 1, tq), F32),
                        pltpu.VMEM((2, seq, tq), F32), pltpu.VMEM((2, 1, tq), F32)],
        compiler_params=_cparams(("arbitrary",)),
        name="attention",
    )(q, k, vt, g)


def _short_conv_kernel(prev_ref, x_ref, next_ref, w_ref, b_ref, o_ref):
    j = pl.program_id(1)
    nj = pl.num_programs(1)
    x = x_ref[0].astype(F32)
    rows = x.shape[0]
    before = jnp.where(j > 0, prev_ref[0, 7:8, :].astype(F32), 0.0)
    after = jnp.where(j < nj - 1, next_ref[0, 0:1, :].astype(F32), 0.0)
    rid = lax.broadcasted_iota(I32, x.shape, 0)
    xm = jnp.where(rid == 0, before, pltpu.roll(x, 1, 0))
    xp = jnp.where(rid == rows - 1, after, pltpu.roll(x, rows - 1, 0))
    u = b_ref[...] + xm * w_ref[0:1, :] + x * w_ref[1:2, :] + xp * w_ref[2:3, :]
    o_ref[...] = u.T.astype(BF16)


def _short_conv(hy3, w, b):
    batch, seq, width = hy3.shape
    nj = seq // TOEP
    sub = TOEP // 8
    last8 = seq // 8 - 1
    return pl.pallas_call(
        _short_conv_kernel,
        grid=(batch, nj),
        in_specs=[pl.BlockSpec((1, 8, width), lambda bb, j: (bb, jnp.maximum(j * sub - 1, 0), 0)),
                  pl.BlockSpec((1, TOEP, width), lambda bb, j: (bb, j, 0)),
                  pl.BlockSpec((1, 8, width), lambda bb, j: (bb, jnp.minimum((j + 1) * sub, last8), 0)),
                  pl.BlockSpec(w.shape, lambda bb, j: (0, 0)),
                  pl.BlockSpec(b.shape, lambda bb, j: (0, 0))],
        out_specs=pl.BlockSpec((width, TOEP), lambda bb, j: (0, j * batch + bb)),
        out_shape=jax.ShapeDtypeStruct((width, nj * batch * TOEP), BF16),
        compiler_params=_cparams(("parallel", "parallel")),
        name="short_conv",
    )(hy3, hy3, hy3, w, b)


def _filter_kernel(z_ref, w1_ref, b1_ref, f1_ref, w2_ref, b2_ref, f2_ref, w3_ref, dl_ref, g_ref):
    hp = lax.Precision.HIGHEST
    z = z_ref[...]
    n = z.shape[1]
    half = n // 2
    h1 = jnp.sin(f1_ref[...] * (jnp.dot(w1_ref[...], z, precision=hp, preferred_element_type=F32) + b1_ref[...]))
    h2 = jnp.sin(f2_ref[...] * (jnp.dot(w2_ref[...], h1, precision=hp, preferred_element_type=F32) + b2_ref[...]))
    decay = jnp.exp(-z[0:1, :] * dl_ref[...])
    lane = lax.broadcasted_iota(I32, (1, n), 1)
    masks = (lane >= half, (lane >= 1) & (lane <= half))
    out = None
    for d in range(2):
        hd = jnp.dot(w3_ref[0, d], h2, precision=hp, preferred_element_type=F32) * decay
        hd = jnp.where(masks[d], hd, 0.0)
        hd = hd / (jnp.sum(jnp.abs(hd), axis=1, keepdims=True) + 1e-6)
        out = hd if out is None else out + hd
    g_ref[0] = out


def _filters(zt, w1t, b1, f1, w2t, b2, f2, w3t, dl):
    n = zt.shape[1]
    cb = 128
    full = lambda a: pl.BlockSpec(a.shape, lambda o, c: (0,) * a.ndim)
    return pl.pallas_call(
        _filter_kernel,
        grid=(HY_ORDER, HY_WIDTH // cb),
        in_specs=[full(zt), full(w1t), full(b1), full(f1), full(w2t), full(b2), full(f2),
                  pl.BlockSpec((1, 2, cb, HY_FILTER_HIDDEN), lambda o, c: (o, 0, c, 0)),
                  pl.BlockSpec((cb, 1), lambda o, c: (c, 0))],
        out_specs=pl.BlockSpec((1, cb, n), lambda o, c: (o, c, 0)),
        out_shape=jax.ShapeDtypeStruct((HY_ORDER, HY_WIDTH, n), F32),
        compiler_params=_cparams(("parallel", "parallel")),
        name="filters",
    )(zt, w1t, b1, f1, w2t, b2, f2, w3t, dl)


def _hyena_kernel(bias_ref, v_ref, x1_ref, x2_ref, g_ref, o_ref, acc_ref, *, batch, nblk, cb):
    rows = batch * nblk
    seq = nblk * TOEP
    c0 = pl.program_id(0) * cb
    gates = (x1_ref, x2_ref)

    def per_channel(c, carry):
        zf = v_ref[c].astype(F32)
        for o in range(HY_ORDER):
            zb = zf.astype(BF16)
            acc_ref[...] = jnp.zeros_like(acc_ref)
            for d in range(-(nblk - 1), nblk):
                base = seq + TOEP * d - TOEP
                win = g_ref[o, c, :, pl.ds(base, 2 * TOEP)]
                w = pltpu.roll(jnp.broadcast_to(win, (TOEP, 2 * TOEP)), TOEP, 1, stride=1, stride_axis=0)
                w = w[:, :TOEP].astype(BF16)
                n = rows - batch * abs(d)
                if d >= 0:
                    acc_ref[batch * d:, :] += _dot(zb[:n], w)
                else:
                    acc_ref[:n, :] += _dot(zb[batch * (-d):], w)
            zf = gates[o][c].astype(F32) * (acc_ref[...] + bias_ref[o, c0 + c] * zf)
        o_ref[c] = zf.astype(BF16)
        return carry

    lax.fori_loop(0, cb, per_channel, 0)


def _hyena(bias, ut, g4, batch, nblk, cb):
    rows = batch * nblk
    nc = HY_WIDTH // cb
    kern = functools.partial(_hyena_kernel, batch=batch, nblk=nblk, cb=cb)
    blk = lambda off: pl.BlockSpec((cb, rows, TOEP), lambda i: (off * nc + i, 0, 0))
    return pl.pallas_call(
        kern,
        grid=(nc,),
        in_specs=[pl.BlockSpec(memory_space=pltpu.SMEM), blk(0), blk(1), blk(2),
                  pl.BlockSpec((HY_ORDER, cb, 1, g4.shape[3]), lambda i: (0, i, 0, 0))],
        out_specs=pl.BlockSpec((cb, rows, TOEP), lambda i: (i, 0, 0)),
        out_shape=jax.ShapeDtypeStruct((HY_WIDTH, rows, TOEP), BF16),
        scratch_shapes=[pltpu.VMEM((rows, TOEP), F32)],
        compiler_params=_cparams(("parallel",)),
        name="hyena",
    )(bias, ut, ut, ut, g4)


def _mix_kernel(x_ref, lg_ref, lb_ref, a_ref, yt_ref, hg_ref, wo_ref, mg_ref, mb_ref, wq_ref,
                h2_ref, h2b_ref, qt_ref, cat_ref):
    y = yt_ref[...].astype(F32).T
    cat_ref[:, :ATTN_WIDTH] = a_ref[...]
    gw = HY_WIDTH // HY_GROUPS
    for g in range(HY_GROUPS):
        sl = slice(g * gw, (g + 1) * gw)
        cat_ref[:, ATTN_WIDTH + g * gw:ATTN_WIDTH + (g + 1) * gw] = _rms(y[:, sl], hg_ref[:, sl]).astype(BF16)
    mix = _dot(cat_ref[...], wo_ref[...])
    h = _layer_norm(x_ref[...], lg_ref[...], lb_ref[...])
    h2 = _layer_norm(ALPHA * h + mix, mg_ref[...], mb_ref[...])
    h2_ref[...] = h2
    h2b = h2.astype(BF16)
    h2b_ref[...] = h2b
    qt_ref[...] = _dot_nt(wq_ref[...], h2b).astype(BF16)


def _mix(x2, lg, lb, a, yt2, hg, wo, mg, mb, wqt, batch, nblk):
    t = x2.shape[0]
    full = lambda w: pl.BlockSpec(w.shape, lambda i: (0,) * w.ndim)
    row = lambda w: pl.BlockSpec((TOEP, w), lambda i: (i, 0))
    return pl.pallas_call(
        _mix_kernel,
        grid=(t // TOEP,),
        in_specs=[row(D_MODEL), full(lg), full(lb), row(ATTN_WIDTH),
                  pl.BlockSpec((HY_WIDTH, TOEP), lambda i: (0, (i % nblk) * batch + i // nblk)),
                  full(hg), full(wo), full(mg), full(mb), full(wqt)],
        out_specs=[row(D_MODEL), row(D_MODEL), pl.BlockSpec((wqt.shape[0], TOEP), lambda i: (0, i))],
        out_shape=[jax.ShapeDtypeStruct((t, D_MODEL), F32), jax.ShapeDtypeStruct((t, D_MODEL), BF16),
                   jax.ShapeDtypeStruct((wqt.shape[0], t), BF16)],
        scratch_shapes=[pltpu.VMEM((TOEP, D_MODEL), BF16)],
        compiler_params=_cparams(("parallel",)),
        name="mix",
    )(x2, lg, lb, a, yt2, hg, wo, mg, mb, wqt)


_CAND_COLS = tuple(PEER_TOPK // (r + 1) for r in range(PEER_TOPK))
N_KEYSETS = 2 * PEER_HEADS


def _top16(s, val_ref, idx_ref, lanes):
    kio = lax.broadcasted_iota(I32, s.shape, 0)
    vals = s
    for r in range(PEER_TOPK):
        m = jnp.max(vals, axis=0, keepdims=True)
        idx = jnp.min(jnp.where(vals == m, kio, PEER_NKEYS), axis=0, keepdims=True)
        val_ref[r:r + 1, lanes] = m
        idx_ref[r:r + 1, lanes] = idx
        vals = jnp.where(kio == idx, -jnp.inf, vals)


def _pair_top16(v1_ref, i1_ref, v2_ref, i2_ref, best_ref, code_ref, lanes):
    neg = -jnp.inf
    cands, codes = [], []
    for r in range(8):
        ncol = 16 if r == 0 else 8
        cv = v1_ref[r:r + 1, lanes] + v2_ref[0:ncol, lanes]
        cio = lax.broadcasted_iota(I32, cv.shape, 0)
        cands.append(jnp.where(cio < _CAND_COLS[r], cv, neg))
        codes.append(i1_ref[r:r + 1, lanes] * PEER_NKEYS + i2_ref[0:ncol, lanes])
    cands.append(v1_ref[8:16, lanes] + v2_ref[0:1, lanes])
    codes.append(i1_ref[8:16, lanes] * PEER_NKEYS + i2_ref[0:1, lanes])
    cand = jnp.concatenate(cands, axis=0)
    code = jnp.concatenate(codes, axis=0)
    rio = lax.broadcasted_iota(I32, cand.shape, 0)
    for kk in range(PEER_TOPK):
        m = jnp.max(cand, axis=0, keepdims=True)
        pos = jnp.min(jnp.where(cand == m, rio, cand.shape[0]), axis=0, keepdims=True)
        sel = rio == pos
        best_ref[kk:kk + 1, lanes] = m
        code_ref[kk:kk + 1, lanes] = jnp.max(jnp.where(sel, code, -1), axis=0, keepdims=True)
        cand = jnp.where(sel, neg, cand)


def _peer_up_kernel(x_ref, u_ref, q_ref, keys_ref, w_ref, isel_ref, jsel_ref,
                    a0_ref, a1_ref, v1_ref, i1_ref, v2_ref, i2_ref, best_ref, codeh_ref, codet_ref, gatet_ref,
                    iseln_ref, jseln_ref, gaten_ref, *, ne):
    i = pl.program_id(0)
    e = pl.program_id(1)
    tt = x_ref.shape[0]
    nsub = u_ref.shape[1] // PEER_NKEYS
    prev = (i + 1) % 2
    cur = i % 2
    chunks = [slice(c, c + LANES) for c in range(0, tt, LANES)]

    def pick(e_src, src, r_lo, r_hi):
        isel_p, jsel_p = iseln_ref.at[prev], jseln_ref.at[prev]
        for r0 in range(r_lo, r_hi, GATHER_ROWS):
            rows = slice(r0, r0 + GATHER_ROWS)
            isel = isel_p[rows, :]
            jsel = jsel_p[rows, :]
            acc = w_ref[rows, :]
            for ii in range(nsub):
                got = jnp.take_along_axis(src[rows, ii * PEER_NKEYS:(ii + 1) * PEER_NKEYS], jsel, axis=1)
                acc = jnp.where(isel == e_src * nsub + ii, got, acc)
            w_ref[rows, :] = acc
        return acc

    def step(dst, src, val_ref, idx_ref, pair):
        assert len(chunks) == 2 * STEP_SLICES
        s = _dot(keys_ref[0], q_ref[...])
        eb = u_ref.shape[1]
        never = e < 0
        bounds = [0] + [2 * c + 1 for c in range(STEP_SLICES)] + [len(chunks)]
        deps = ()
        for c in range(STEP_SLICES + 1):
            nxt = deps
            if c < STEP_SLICES:
                cols = slice(c * eb // STEP_SLICES, (c + 1) * eb // STEP_SLICES)
                res = _dot(x_ref[...], u_ref[:, cols])
                dst[:, cols] = res
                acc = pick(e - 1, src, c * tt // STEP_SLICES, (c + 1) * tt // STEP_SLICES)
                nxt = (res[:PEER_NKEYS, :LANES], jnp.broadcast_to(acc[0:1, :], (PEER_NKEYS, LANES)))
            for lanes in chunks[bounds[c]:bounds[c + 1]]:
                sc = s[:, lanes]
                for dep in deps:
                    sc = jnp.where(never, dep, sc)
                _top16(sc, val_ref, idx_ref, lanes)
                if pair:
                    _pair_top16(v1_ref, i1_ref, v2_ref, i2_ref, best_ref, codeh_ref, lanes)
            deps = nxt

    @pl.when((i == 0) & (e == 0))
    def _():
        a1_ref[...] = jnp.zeros_like(a1_ref)
        iseln_ref[...] = jnp.zeros_like(iseln_ref)
        jseln_ref[...] = jnp.zeros_like(jseln_ref)
        gaten_ref[...] = jnp.zeros_like(gaten_ref)

    @pl.when(e == 0)
    def _():
        w_ref[...] = jnp.zeros_like(w_ref)

    @pl.when((e < ne) & (e % 2 == 0))
    def _():
        step(a0_ref, a1_ref, v1_ref, i1_ref, False)

    @pl.when((e < ne) & (e % 2 == 1))
    def _():
        step(a1_ref, a0_ref, v2_ref, i2_ref, True)
        best = best_ref[...]
        ex = jnp.exp(best - best[0:1, :])
        row0 = pl.multiple_of((e // 2) * PEER_TOPK, PEER_TOPK)
        gatet_ref[pl.ds(row0, PEER_TOPK), :] = ex / jnp.sum(ex, axis=0, keepdims=True)
        codet_ref[pl.ds(row0, PEER_TOPK), :] = codeh_ref[...]

    @pl.when(e == ne)
    def _():
        pick(e - 1, a1_ref if ne % 2 == 0 else a0_ref, 0, tt)
        s = w_ref[...]
        gelu = 0.5 * s * (1.0 + lax.erf(s * (2.0 ** -0.5)))
        w_ref[...] = gaten_ref[prev] * gelu
        code_n = codet_ref[...].T
        isel = code_n >> 7
        jsel = code_n & (PEER_NKEYS - 1)
        isel_ref[...] = isel
        jsel_ref[...] = jsel
        iseln_ref[cur] = isel
        jseln_ref[cur] = jsel
        gaten_ref[cur] = gatet_ref[...].T


def _peer_up(h2b, ut, qt, keys, tt, eb):
    t = h2b.shape[0]
    ne = ut.shape[1] // eb
    assert ne == N_KEYSETS
    ntile = t // tt
    half = PEER_DK // 2
    tok = lambda i, e: (jnp.maximum(i - 1, 0), 0)
    rt = lambda i, e: (jnp.minimum(i, ntile - 1), 0)
    vm = lambda shape, dt: pltpu.VMEM(shape, dt)
    return pl.pallas_call(
        functools.partial(_peer_up_kernel, ne=ne),
        grid=(ntile + 1, ne + 1),
        in_specs=[pl.BlockSpec((tt, D_MODEL), tok),
                  pl.BlockSpec((D_MODEL, eb), lambda i, e: (0, jnp.minimum(e, ne - 1))),
                  pl.BlockSpec((half, tt), lambda i, e: (jnp.minimum(e, ne - 1), jnp.minimum(i, ntile - 1))),
                  pl.BlockSpec((1, PEER_NKEYS, half), lambda i, e: (jnp.minimum(e, ne - 1), 0, 0))],
        out_specs=[pl.BlockSpec((tt, N_SEL), tok), pl.BlockSpec((tt, N_SEL), rt), pl.BlockSpec((tt, N_SEL), rt)],
        out_shape=[jax.ShapeDtypeStruct((t, N_SEL), F32), jax.ShapeDtypeStruct((t, N_SEL), I32),
                   jax.ShapeDtypeStruct((t, N_SEL), I32)],
        scratch_shapes=[vm((tt, eb), F32), vm((tt, eb), F32),
                        vm((PEER_TOPK, tt), F32), vm((PEER_TOPK, tt), I32),
                        vm((PEER_TOPK, tt), F32), vm((PEER_TOPK, tt), I32),
                        vm((PEER_TOPK, tt), F32), vm((PEER_TOPK, tt), I32),
                        vm((N_SEL, tt), I32), vm((N_SEL, tt), F32),
                        vm((2, tt, N_SEL), I32), vm((2, tt, N_SEL), I32), vm((2, tt, N_SEL), F32)],
        compiler_params=_cparams(("arbitrary", "arbitrary")),
        name="peer_up",
    )(h2b, ut, qt, keys)


def _peer_down_kernel(isel_ref, jsel_ref, w_ref, v_ref, h2_ref, lg_ref, lb_ref, o_ref, wd_ref, acc_ref, *, ib):
    e = pl.program_id(1)
    tt = isel_ref.shape[0]
    hi_mask = jnp.uint32(0xFFFF0000)

    @pl.when(e == 0)
    def _():
        acc_ref[...] = jnp.zeros_like(acc_ref)
        sub = lax.broadcasted_iota(I32, (PEER_NKEYS, N_SEL), 0)

        half = PEER_NKEYS // 2

        def per_group(gi, carry):
            base = pl.multiple_of(gi * TOKEN_GROUP, TOKEN_GROUP)
            isb = isel_ref[pl.ds(base, TOKEN_GROUP), :]
            jsb = jsel_ref[pl.ds(base, TOKEN_GROUP), :]
            wb = w_ref[pl.ds(base, TOKEN_GROUP), :]
            for r in range(TOKEN_GROUP):
                pt = jnp.where(sub == isb[r:r + 1, :], wb[r:r + 1, :], 0.0).astype(BF16)
                qt = jnp.where(sub == jsb[r:r + 1, :], 1.0, 0.0).astype(BF16)
                gb = pltpu.bitcast(_dot_nt(pt, qt), jnp.uint32)
                packed = (gb[half:] & hi_mask) | (gb[:half] >> 16)
                wd_ref[pl.ds(pl.multiple_of((base + r) * W_PITCH, 8), half), :] = packed
            return carry

        lax.fori_loop(0, tt // TOKEN_GROUP, per_group, 0)

    parts = []
    for ii in range(ib):
        pk = wd_ref[pl.ds(e * ib + ii, tt, stride=W_PITCH), :]
        parts.append(pltpu.bitcast(pk << 16, F32).astype(BF16))
        parts.append(pltpu.bitcast(pk & hi_mask, F32).astype(BF16))
    acc_ref[...] += _dot(jnp.concatenate(parts, axis=1), v_ref[...])

    @pl.when(e == pl.num_programs(1) - 1)
    def _():
        o_ref[...] = _layer_norm(ALPHA * h2_ref[...] + acc_ref[...], lg_ref[...], lb_ref[...])


def _peer_down(isel, jsel, w, vperm, h2, lg, lb, tt, ib):
    t = h2.shape[0]
    ne = (PEER_NKEYS // 2) // ib
    vb = ib * 2 * PEER_NKEYS
    sel = pl.BlockSpec((tt, N_SEL), lambda i, e: (i, 0))
    full = lambda a: pl.BlockSpec(a.shape, lambda i, e: (0,) * a.ndim)
    return pl.pallas_call(
        functools.partial(_peer_down_kernel, ib=ib),
        grid=(t // tt, ne),
        in_specs=[sel, sel, sel, pl.BlockSpec((vb, D_MODEL), lambda i, e: (e, 0)),
                  pl.BlockSpec((tt, D_MODEL), lambda i, e: (i, 0)), full(lg), full(lb)],
        out_specs=pl.BlockSpec((tt, D_MODEL), lambda i, e: (i, 0)),
        out_shape=jax.ShapeDtypeStruct((t, D_MODEL), F32),
        scratch_shapes=[pltpu.VMEM((tt * W_PITCH, N_SEL), jnp.uint32), pltpu.VMEM((tt, D_MODEL), F32)],
        compiler_params=_cparams(("parallel", "arbitrary")),
        name="peer_down",
    )(isel, jsel, w, vperm, h2, lg, lb)


def _rope_tables(seq):
    half = QK_ROPE_DIM // 2
    inv = 1.0 / (ROPE_THETA ** (jnp.arange(0, QK_ROPE_DIM, 2, dtype=F32) / QK_ROPE_DIM))
    ang = jnp.arange(seq, dtype=F32)[:, None] * inv[None, :]
    cos, sin = jnp.cos(ang), jnp.sin(ang)
    c32 = jnp.concatenate([cos, cos], axis=1)
    s32 = jnp.concatenate([-sin, sin], axis=1)
    z = lambda w: jnp.zeros((seq, w), F32)
    scale = (QK_NOPE_DIM + QK_ROPE_DIM) ** -0.5 * math.log2(math.e)
    pad = HEAD_SLOT - QK_NOPE_DIM - QK_ROPE_DIM
    cq = jnp.concatenate([jnp.full((seq, QK_NOPE_DIM), scale, F32), scale * c32, z(pad)], axis=1)
    sq = jnp.concatenate([z(QK_NOPE_DIM), scale * s32, z(pad)], axis=1)
    ck = jnp.concatenate([z(QK_NOPE_DIM), c32, z(pad)], axis=1)
    sk = jnp.concatenate([z(QK_NOPE_DIM), s32, z(pad)], axis=1)
    return cq, sq, ck, sk


def _swap_halves(w):
    half = w.shape[-1] // 2
    return jnp.concatenate([w[..., half:], w[..., :half]], axis=-1)


def _slot(w, offset):
    return jnp.pad(w, ((0, 0), (offset, HEAD_SLOT - offset - w.shape[1])))


def _position_features(seq):
    t = jnp.linspace(0.0, 1.0, seq, dtype=F32)[:, None]
    bands = (HY_EMB_DIM - 1) // 2
    w = 2.0 * math.pi * jnp.arange(seq, dtype=F32) / seq
    f = jnp.linspace(1e-4, bands - 1, bands, dtype=F32)
    ang = w[:, None] * f[None, :]
    return jnp.concatenate([t, jnp.cos(ang), -jnp.sin(ang)], axis=-1)


def kernel(x, emb_ln_g, emb_ln_b, w_in, q_norm_g, w_uq, kv_norm_g, w_ukv, hy_short_w, hy_short_b, hy_filt_w1,
           hy_filt_b1, hy_filt_freq1, hy_filt_w2, hy_filt_b2, hy_filt_freq2, hy_filt_w3, hy_bias, attn_out_g,
           hy_out_g, w_o, ln_mix_g, ln_mix_b, peer_wq, peer_sub_keys, peer_u, peer_v, ln_ffn_g, ln_ffn_b):
    batch, seq, _ = x.shape
    assert w_in.shape[0] == DEPTH == 1 and seq % TOEP == 0
    t = batch * seq
    nblk = seq // TOEP
    r2 = lambda a: a.reshape(1, -1)
    x2 = x.reshape(t, D_MODEL)
    lg, lb = r2(emb_ln_g), r2(emb_ln_b)

    wi = w_in[0]
    w_kr = wi[:, OFF_CKV:OFF_KR]
    wall = jnp.concatenate([wi[:, :OFF_CKV], _slot(w_kr, QK_NOPE_DIM), _slot(_swap_halves(w_kr), QK_NOPE_DIM),
                            wi[:, OFF_KR:]], axis=1).astype(BF16)
    dq = QK_NOPE_DIM + QK_ROPE_DIM
    wuq = w_uq[0].reshape(Q_LORA_RANK, ATTN_HEADS, dq)
    wq = jnp.pad(wuq, ((0, 0), (0, 0), (0, HEAD_SLOT - dq))).reshape(Q_LORA_RANK, -1).astype(BF16)
    wqs = jnp.pad(_swap_halves(wuq[..., QK_NOPE_DIM:]),
                  ((0, 0), (0, 0), (QK_NOPE_DIM, HEAD_SLOT - dq))).reshape(Q_LORA_RANK, -1).astype(BF16)
    wukv = w_ukv[0].reshape(KV_LORA_RANK, ATTN_HEADS, QK_NOPE_DIM + V_HEAD_DIM)
    wk = jnp.pad(wukv[..., :QK_NOPE_DIM],
                 ((0, 0), (0, 0), (0, HEAD_SLOT - QK_NOPE_DIM))).reshape(KV_LORA_RANK, -1).astype(BF16)
    wv = wukv[..., QK_NOPE_DIM:].reshape(KV_LORA_RANK, -1).T.astype(BF16)
    cq, sq, ck, sk = _rope_tables(seq)

    tt_proj = min(512, seq)
    q, k, vt, hy = _ln_proj(x2, lg, lb, wall, r2(q_norm_g[0]), wq, wqs, r2(kv_norm_g[0]), wk, wv,
                           cq, sq, ck, sk, seq, tt_proj)

    a = _attention(q, k, vt, attn_out_g[0].reshape(-1, 1), batch, seq, min(256, seq))

    ut = _short_conv(hy.reshape(batch, seq, -1), hy_short_w[0], r2(hy_short_b[0]))
    ut = ut.reshape((HY_ORDER + 1) * HY_WIDTH, nblk * batch, TOEP)

    lag = jnp.abs(jnp.arange(2 * seq) - seq)
    feats = _position_features(seq)
    zt = jnp.pad(feats[jnp.minimum(lag, seq - 1)].T, ((0, HY_FILTER_HIDDEN - HY_EMB_DIM), (0, 0)))
    col = lambda a_: a_.reshape(-1, 1)
    w1t = jnp.pad(hy_filt_w1[0].T, ((0, 0), (0, HY_FILTER_HIDDEN - HY_EMB_DIM)))
    w3t = hy_filt_w3[0].T.reshape(HY_ORDER, 2, HY_WIDTH, HY_FILTER_HIDDEN)
    deltas = jnp.abs(jnp.linspace(math.log(HY_TARGET) / HY_SLOW_DECAY, math.log(HY_TARGET) / HY_FAST_DECAY,
                                  HY_WIDTH, dtype=F32))
    g = _filters(zt, w1t, col(hy_filt_b1[0]), col(hy_filt_freq1[0]), hy_filt_w2[0].T, col(hy_filt_b2[0]),
                 col(hy_filt_freq2[0]), w3t, col(deltas))
    yt = _hyena(hy_bias[0], ut, g.reshape(HY_ORDER, HY_WIDTH, 1, 2 * seq), batch, nblk, 8)

    h2, h2b, qt = _mix(x2, lg, lb, a, yt.reshape(HY_WIDTH, nblk * batch * TOEP), r2(hy_out_g[0]),
                       w_o[0].astype(BF16), r2(ln_mix_g[0]), r2(ln_mix_b[0]), peer_wq[0].T.astype(BF16), batch, nblk)

    keys = peer_sub_keys[0].astype(BF16).reshape(N_KEYSETS, PEER_NKEYS, PEER_DK // 2)
    w, isel, jsel = _peer_up(h2b, peer_u[0].astype(BF16).T, qt, keys, min(1024, t), 1024)
    ib = 8
    half = PEER_NKEYS // 2
    vperm = peer_v[0].astype(BF16).reshape(2, half // ib, ib, PEER_NKEYS, D_MODEL)
    vperm = vperm.transpose(1, 2, 0, 3, 4).reshape(PEER_NKEYS * PEER_NKEYS, D_MODEL)
    out = _peer_down(isel, jsel, w, vperm, h2, r2(ln_ffn_g[0]), r2(ln_ffn_b[0]), min(512, t), ib)
    return out.reshape(batch, seq, D_MODEL)
```

```python
import functools
import math

import jax
import jax.numpy as jnp
from jax import lax
from jax.experimental import pallas as pl
from jax.experimental.pallas import tpu as pltpu

F32 = jnp.float32
BF16 = jnp.bfloat16
I32 = jnp.int32

D_MODEL = 1024
ATTN_HEADS = 8
QK_NOPE_DIM = 64
QK_ROPE_DIM = 32
V_HEAD_DIM = 64
Q_LORA_RANK = 256
KV_LORA_RANK = 128
ATTN_WIDTH = ATTN_HEADS * V_HEAD_DIM
ROPE_THETA = 10000.0
HY_WIDTH = D_MODEL - ATTN_WIDTH
HY_ORDER = 2
HY_GROUPS = 8
HY_SHORT = 3
HY_EMB_DIM = 33
HY_FILTER_HIDDEN = 64
HY_FAST_DECAY = 0.3
HY_SLOW_DECAY = 1.5
HY_TARGET = 1e-2
OFF_CQ = Q_LORA_RANK
OFF_CKV = OFF_CQ + KV_LORA_RANK
OFF_KR = OFF_CKV + QK_ROPE_DIM
PEER_HEADS = 8
PEER_NKEYS = 128
PEER_DK = 128
PEER_TOPK = 16
DEPTH = 1
ALPHA = (2 * DEPTH) ** 0.25
LN_EPS = 1e-5
RMS_EPS = 1e-6

LANES = 128
HEAD_SLOT = 128
TOEP = 256
N_SEL = PEER_HEADS * PEER_TOPK
W_PITCH = 72
TOKEN_GROUP = 16
GATHER_ROWS = 8
STEP_SLICES = 4
VMEM_LIMIT = 56 * 1024 * 1024


def _cparams(sem):
    return pltpu.CompilerParams(dimension_semantics=sem, vmem_limit_bytes=VMEM_LIMIT)


def _layer_norm(x, g, b):
    mu = jnp.mean(x, axis=-1, keepdims=True)
    xc = x - mu
    var = jnp.mean(xc * xc, axis=-1, keepdims=True)
    return xc * lax.rsqrt(var + LN_EPS) * g + b


def _rms(x, g):
    return x * lax.rsqrt(jnp.mean(x * x, axis=-1, keepdims=True) + RMS_EPS) * g


def _dot(a, b):
    return jnp.dot(a, b, preferred_element_type=F32)


def _dot_nt(a, b):
    return lax.dot_general(a, b, (((1,), (1,)), ((), ())), preferred_element_type=F32)


N_MLA_COLS = Q_LORA_RANK + KV_LORA_RANK + 2 * HEAD_SLOT


def _ln_proj_kernel(x_ref, lg_ref, lb_ref, wall_ref, gq_ref, wq_ref, wqs_ref, gkv_ref, wk_ref, wv_ref,
                    cq_ref, sq_ref, ck_ref, sk_ref, q_ref, k_ref, vt_ref, hy_ref):
    h = _layer_norm(x_ref[...], lg_ref[...], lb_ref[...])
    proj = _dot(h.astype(BF16), wall_ref[...])
    c_q = proj[:, :OFF_CQ]
    c_kv = proj[:, OFF_CQ:OFF_CKV]
    kr = proj[:, OFF_CKV:OFF_CKV + HEAD_SLOT]
    kr_sw = proj[:, OFF_CKV + HEAD_SLOT:N_MLA_COLS]
    hy_ref[...] = proj[:, N_MLA_COLS:].astype(BF16)

    nq = _rms(c_q, gq_ref[...]).astype(BF16)
    qa = _dot(nq, wq_ref[...])
    qb = _dot(nq, wqs_ref[...])
    nkv = _rms(c_kv, gkv_ref[...]).astype(BF16)
    kn = _dot(nkv, wk_ref[...])
    vt_ref[...] = _dot_nt(wv_ref[...], nkv).astype(BF16)
    k_pe = kr * ck_ref[...] + kr_sw * sk_ref[...]
    cq, sq = cq_ref[...], sq_ref[...]
    for hd in range(ATTN_HEADS):
        sl = slice(hd * HEAD_SLOT, (hd + 1) * HEAD_SLOT)
        q_ref[:, sl] = (qa[:, sl] * cq + qb[:, sl] * sq).astype(BF16)
        k_ref[:, sl] = (kn[:, sl] + k_pe).astype(BF16)


def _ln_proj(x2, lg, lb, wall, gq, wq, wqs, gkv, wk, wv, cq, sq, ck, sk, seq, tt):
    t = x2.shape[0]
    npos = seq // tt
    full = lambda a: pl.BlockSpec(a.shape, lambda i: (0,) * a.ndim)
    tab = pl.BlockSpec((tt, HEAD_SLOT), lambda i: (i % npos, 0))
    row = lambda w: pl.BlockSpec((tt, w), lambda i: (i, 0))
    hw = wall.shape[1] - N_MLA_COLS
    return pl.pallas_call(
        _ln_proj_kernel,
        grid=(t // tt,),
        in_specs=[row(D_MODEL), full(lg), full(lb), full(wall), full(gq), full(wq), full(wqs), full(gkv),
                  full(wk), full(wv), tab, tab, tab, tab],
        out_specs=[row(ATTN_HEADS * HEAD_SLOT), row(ATTN_HEADS * HEAD_SLOT),
                   pl.BlockSpec((ATTN_WIDTH, tt), lambda i: (0, i)), row(hw)],
        out_shape=[jax.ShapeDtypeStruct((t, ATTN_HEADS * HEAD_SLOT), BF16),
                   jax.ShapeDtypeStruct((t, ATTN_HEADS * HEAD_SLOT), BF16),
                   jax.ShapeDtypeStruct((ATTN_WIDTH, t), BF16),
                   jax.ShapeDtypeStruct((t, hw), BF16)],
        compiler_params=_cparams(("parallel",)),
        name="ln_proj",
    )(x2, lg, lb, wall, gq, wq, wqs, gkv, wk, wv, cq, sq, ck, sk)


def _attn_kernel(q_ref, k_ref, vt_ref, g_ref, o_ref, s0_ref, m0_ref, s1_ref, m1_ref, *, nsteps):
    n = pl.program_id(0)

    def scores(s_ref, m_ref):
        for hh in range(2):
            sl = slice(hh * HEAD_SLOT, (hh + 1) * HEAD_SLOT)
            s = _dot_nt(k_ref[:, sl], q_ref[:, sl])
            s_ref[hh] = s
            m_ref[hh] = jnp.max(s, axis=0, keepdims=True)

    def finish(s_ref, m_ref):
        outs = []
        ones = jnp.ones((16, vt_ref.shape[1]), BF16)
        for hh in range(2):
            vs = slice(hh * V_HEAD_DIM, (hh + 1) * V_HEAD_DIM)
            p = jnp.exp2(s_ref[hh] - m_ref[hh]).astype(BF16)
            ov = _dot(jnp.concatenate([vt_ref[vs, :], ones], axis=0), p)
            o = ov[:V_HEAD_DIM] / ov[V_HEAD_DIM:V_HEAD_DIM + 1]
            ms = jnp.mean(o * o, axis=0, keepdims=True)
            outs.append(o * lax.rsqrt(ms + RMS_EPS) * g_ref[vs, :])
        o_ref[...] = jnp.concatenate(outs, axis=0).T.astype(BF16)

    @pl.when(n == 0)
    def _():
        s1_ref[...] = jnp.zeros_like(s1_ref)
        m1_ref[...] = jnp.zeros_like(m1_ref)

    @pl.when((n < nsteps) & (n % 2 == 0))
    def _():
        finish(s1_ref, m1_ref)
        scores(s0_ref, m0_ref)

    @pl.when((n < nsteps) & (n % 2 == 1))
    def _():
        finish(s0_ref, m0_ref)
        scores(s1_ref, m1_ref)

    @pl.when(n == nsteps)
    def _():
        if nsteps % 2 == 0:
            finish(s1_ref, m1_ref)
        else:
            finish(s0_ref, m0_ref)


def _attention(q, k, vt, g, batch, seq, tq):
    t = q.shape[0]
    nq = seq // tq
    npair = ATTN_HEADS // 2
    nsteps = batch * npair * nq

    def tile(n):
        n = jnp.clip(n, 0, nsteps - 1)
        return n // (npair * nq), (n // nq) % npair, n % nq

    def q_map(n):
        b, p, i = tile(n)
        return b * nq + i, p

    def k_map(n):
        b, p, _ = tile(n)
        return b, p

    def vt_map(n):
        b, p, _ = tile(n - 1)
        return p, b

    def g_map(n):
        return tile(n - 1)[1], 0

    def o_map(n):
        b, p, i = tile(n - 1)
        return b * nq + i, p

    return pl.pallas_call(
        functools.partial(_attn_kernel, nsteps=nsteps),
        grid=(nsteps + 1,),
        in_specs=[pl.BlockSpec((tq, 2 * HEAD_SLOT), q_map),
                  pl.BlockSpec((seq, 2 * HEAD_SLOT), k_map),
                  pl.BlockSpec((2 * V_HEAD_DIM, seq), vt_map),
                  pl.BlockSpec((2 * V_HEAD_DIM, 1), g_map)],
        out_specs=pl.BlockSpec((tq, 2 * V_HEAD_DIM), o_map),
        out_shape=jax.ShapeDtypeStruct((t, ATTN_WIDTH), BF16),
        scratch_shapes=[pltpu.VMEM((2, seq, tq), F32), pltpu.VMEM((2, 1, tq), F32),
                        pltpu.VMEM((2, seq, tq), F32), pltpu.VMEM((2, 1, tq), F32)],
        compiler_params=_cparams(("arbitrary",)),
        name="attention",
    )(q, k, vt, g)


def _short_conv_kernel(prev_ref, x_ref, next_ref, w_ref, b_ref, o_ref):
    j = pl.program_id(1)
    nj = pl.num_programs(1)
    x = x_ref[0].astype(F32)
    rows = x.shape[0]
    before = jnp.where(j > 0, prev_ref[0, 7:8, :].astype(F32), 0.0)
    after = jnp.where(j < nj - 1, next_ref[0, 0:1, :].astype(F32), 0.0)
    rid = lax.broadcasted_iota(I32, x.shape, 0)
    xm = jnp.where(rid == 0, before, pltpu.roll(x, 1, 0))
    xp = jnp.where(rid == rows - 1, after, pltpu.roll(x, rows - 1, 0))
    u = b_ref[...] + xm * w_ref[0:1, :] + x * w_ref[1:2, :] + xp * w_ref[2:3, :]
    o_ref[...] = u.T.astype(BF16)


def _short_conv(hy3, w, b):
    batch, seq, width = hy3.shape
    nj = seq // TOEP
    sub = TOEP // 8
    last8 = seq // 8 - 1
    return pl.pallas_call(
        _short_conv_kernel,
        grid=(batch, nj),
        in_specs=[pl.BlockSpec((1, 8, width), lambda bb, j: (bb, jnp.maximum(j * sub - 1, 0), 0)),
                  pl.BlockSpec((1, TOEP, width), lambda bb, j: (bb, j, 0)),
                  pl.BlockSpec((1, 8, width), lambda bb, j: (bb, jnp.minimum((j + 1) * sub, last8), 0)),
                  pl.BlockSpec(w.shape, lambda bb, j: (0, 0)),
                  pl.BlockSpec(b.shape, lambda bb, j: (0, 0))],
        out_specs=pl.BlockSpec((width, TOEP), lambda bb, j: (0, j * batch + bb)),
        out_shape=jax.ShapeDtypeStruct((width, nj * batch * TOEP), BF16),
        compiler_params=_cparams(("parallel", "parallel")),
        name="short_conv",
    )(hy3, hy3, hy3, w, b)


def _filter_kernel(z_ref, w1_ref, b1_ref, f1_ref, w2_ref, b2_ref, f2_ref, w3_ref, dl_ref, g_ref):
    hp = lax.Precision.HIGHEST
    z = z_ref[...]
    n = z.shape[1]
    half = n // 2
    h1 = jnp.sin(f1_ref[...] * (jnp.dot(w1_ref[...], z, precision=hp, preferred_element_type=F32) + b1_ref[...]))
    h2 = jnp.sin(f2_ref[...] * (jnp.dot(w2_ref[...], h1, precision=hp, preferred_element_type=F32) + b2_ref[...]))
    decay = jnp.exp(-z[0:1, :] * dl_ref[...])
    lane = lax.broadcasted_iota(I32, (1, n), 1)
    masks = (lane >= half, (lane >= 1) & (lane <= half))
    out = None
    for d in range(2):
        hd = jnp.dot(w3_ref[0, d], h2, precision=hp, preferred_element_type=F32) * decay
        hd = jnp.where(masks[d], hd, 0.0)
        hd = hd / (jnp.sum(jnp.abs(hd), axis=1, keepdims=True) + 1e-6)
        out = hd if out is None else out + hd
    g_ref[0] = out


def _filters(zt, w1t, b1, f1, w2t, b2, f2, w3t, dl):
    n = zt.shape[1]
    cb = 128
    full = lambda a: pl.BlockSpec(a.shape, lambda o, c: (0,) * a.ndim)
    return pl.pallas_call(
        _filter_kernel,
        grid=(HY_ORDER, HY_WIDTH // cb),
        in_specs=[full(zt), full(w1t), full(b1), full(f1), full(w2t), full(b2), full(f2),
                  pl.BlockSpec((1, 2, cb, HY_FILTER_HIDDEN), lambda o, c: (o, 0, c, 0)),
                  pl.BlockSpec((cb, 1), lambda o, c: (c, 0))],
        out_specs=pl.BlockSpec((1, cb, n), lambda o, c: (o, c, 0)),
        out_shape=jax.ShapeDtypeStruct((HY_ORDER, HY_WIDTH, n), F32),
        compiler_params=_cparams(("parallel", "parallel")),
        name="filters",
    )(zt, w1t, b1, f1, w2t, b2, f2, w3t, dl)


def _hyena_kernel(bias_ref, v_ref, x1_ref, x2_ref, g_ref, o_ref, acc_ref, *, batch, nblk, cb):
    rows = batch * nblk
    seq = nblk * TOEP
    c0 = pl.program_id(0) * cb
    gates = (x1_ref, x2_ref)

    def per_channel(c, carry):
        zf = v_ref[c].astype(F32)
        for o in range(HY_ORDER):
            zb = zf.astype(BF16)
            acc_ref[...] = jnp.zeros_like(acc_ref)
            for d in range(-(nblk - 1), nblk):
                base = seq + TOEP * d - TOEP
                win = g_ref[o, c, :, pl.ds(base, 2 * TOEP)]
                w = pltpu.roll(jnp.broadcast_to(win, (TOEP, 2 * TOEP)), TOEP, 1, stride=1, stride_axis=0)
                w = w[:, :TOEP].astype(BF16)
                n = rows - batch * abs(d)
                if d >= 0:
                    acc_ref[batch * d:, :] += _dot(zb[:n], w)
                else:
                    acc_ref[:n, :] += _dot(zb[batch * (-d):], w)
            zf = gates[o][c].astype(F32) * (acc_ref[...] + bias_ref[o, c0 + c] * zf)
        o_ref[c] = zf.astype(BF16)
        return carry

    lax.fori_loop(0, cb, per_channel, 0)


def _hyena(bias, ut, g4, batch, nblk, cb):
    rows = batch * nblk
    nc = HY_WIDTH // cb
    kern = functools.partial(_hyena_kernel, batch=batch, nblk=nblk, cb=cb)
    blk = lambda off: pl.BlockSpec((cb, rows, TOEP), lambda i: (off * nc + i, 0, 0))
    return pl.pallas_call(
        kern,
        grid=(nc,),
        in_specs=[pl.BlockSpec(memory_space=pltpu.SMEM), blk(0), blk(1), blk(2),
                  pl.BlockSpec((HY_ORDER, cb, 1, g4.shape[3]), lambda i: (0, i, 0, 0))],
        out_specs=pl.BlockSpec((cb, rows, TOEP), lambda i: (i, 0, 0)),
        out_shape=jax.ShapeDtypeStruct((HY_WIDTH, rows, TOEP), BF16),
        scratch_shapes=[pltpu.VMEM((rows, TOEP), F32)],
        compiler_params=_cparams(("parallel",)),
        name="hyena",
    )(bias, ut, ut, ut, g4)


def _mix_kernel(x_ref, lg_ref, lb_ref, a_ref, yt_ref, hg_ref, wo_ref, mg_ref, mb_ref, wq_ref,
                h2_ref, h2b_ref, qt_ref, cat_ref):
    y = yt_ref[...].astype(F32).T
    cat_ref[:, :ATTN_WIDTH] = a_ref[...]
    gw = HY_WIDTH // HY_GROUPS
    for g in range(HY_GROUPS):
        sl = slice(g * gw, (g + 1) * gw)
        cat_ref[:, ATTN_WIDTH + g * gw:ATTN_WIDTH + (g + 1) * gw] = _rms(y[:, sl], hg_ref[:, sl]).astype(BF16)
    mix = _dot(cat_ref[...], wo_ref[...])
    h = _layer_norm(x_ref[...], lg_ref[...], lb_ref[...])
    h2 = _layer_norm(ALPHA * h + mix, mg_ref[...], mb_ref[...])
    h2_ref[...] = h2
    h2b = h2.astype(BF16)
    h2b_ref[...] = h2b
    qt_ref[...] = _dot_nt(wq_ref[...], h2b).astype(BF16)


def _mix(x2, lg, lb, a, yt2, hg, wo, mg, mb, wqt, batch, nblk):
    t = x2.shape[0]
    full = lambda w: pl.BlockSpec(w.shape, lambda i: (0,) * w.ndim)
    row = lambda w: pl.BlockSpec((TOEP, w), lambda i: (i, 0))
    return pl.pallas_call(
        _mix_kernel,
        grid=(t // TOEP,),
        in_specs=[row(D_MODEL), full(lg), full(lb), row(ATTN_WIDTH),
                  pl.BlockSpec((HY_WIDTH, TOEP), lambda i: (0, (i % nblk) * batch + i // nblk)),
                  full(hg), full(wo), full(mg), full(mb), full(wqt)],
        out_specs=[row(D_MODEL), row(D_MODEL), pl.BlockSpec((wqt.shape[0], TOEP), lambda i: (0, i))],
        out_shape=[jax.ShapeDtypeStruct((t, D_MODEL), F32), jax.ShapeDtypeStruct((t, D_MODEL), BF16),
                   jax.ShapeDtypeStruct((wqt.shape[0], t), BF16)],
        scratch_shapes=[pltpu.VMEM((TOEP, D_MODEL), BF16)],
        compiler_params=_cparams(("parallel",)),
        name="mix",
    )(x2, lg, lb, a, yt2, hg, wo, mg, mb, wqt)


N_KEYSETS = 2 * PEER_HEADS


SORT_GROUP = 4


def _top16(s, val_ref, idx_ref, lanes):
    neg = -jnp.inf
    sub = lax.broadcasted_iota(I32, (8, s.shape[1]), 0)
    ntile = s.shape[0] // 8
    groups = []
    for g0 in range(0, ntile, SORT_GROUP):
        t = [s[8 * v:8 * v + 8, :] for v in range(g0, g0 + SORT_GROUP)]
        ix = [sub + 8 * v for v in range(g0, g0 + SORT_GROUP)]
        for end in range(SORT_GROUP - 1, 0, -1):
            for p in range(end):
                swap = t[p + 1] > t[p]
                t[p], t[p + 1] = jnp.where(swap, t[p + 1], t[p]), jnp.where(swap, t[p], t[p + 1])
                ix[p], ix[p + 1] = jnp.where(swap, ix[p + 1], ix[p]), jnp.where(swap, ix[p], ix[p + 1])
        groups.append((t, ix))
    for r in range(PEER_TOPK):
        heads = [t[0] for t, _ in groups]
        while len(heads) > 1:
            heads = [jnp.maximum(heads[k], heads[k + 1]) for k in range(0, len(heads), 2)]
        m = jnp.max(heads[0], axis=0, keepdims=True)
        cand = [jnp.where(t[0] == m, ix[0], PEER_NKEYS) for t, ix in groups]
        while len(cand) > 1:
            cand = [jnp.minimum(cand[k], cand[k + 1]) for k in range(0, len(cand), 2)]
        idx = jnp.min(cand[0], axis=0, keepdims=True)
        val_ref[r:r + 1, lanes] = m
        idx_ref[r:r + 1, lanes] = idx
        for t, ix in groups:
            hit = ix[0] == idx
            for p in range(SORT_GROUP - 1):
                t[p] = jnp.where(hit, t[p + 1], t[p])
                ix[p] = jnp.where(hit, ix[p + 1], ix[p])
            t[-1] = jnp.where(hit, neg, t[-1])


_CODE_BITS = 14


def _pair_top16(v1_ref, i1_ref, v2_ref, i2_ref, best_ref, code_ref, lanes):
    neg = -jnp.inf
    width = v1_ref[0:8, lanes].shape[1]
    sub = lax.broadcasted_iota(I32, (8, width), 0)
    bc = lambda ref, r: jnp.broadcast_to(ref[r:r + 1, lanes], (8, width))

    def tiles(a_ref, b_ref, combine):
        b_lo = b_ref[0:8, lanes]
        a_hi = pltpu.roll(a_ref[8:16, lanes], 2, 0)
        out = [combine(bc(a_ref, 0), b_lo), combine(bc(a_ref, 0), b_ref[8:16, lanes]), combine(bc(a_ref, 1), b_lo)]
        out.append(jnp.where(sub < 5, combine(bc(a_ref, 2), b_lo), combine(bc(a_ref, 4), pltpu.roll(b_lo, 5, 0))))
        out.append(jnp.where(sub < 4, combine(bc(a_ref, 3), b_lo),
                             jnp.where(sub < 6, combine(bc(a_ref, 5), pltpu.roll(b_lo, 4, 0)),
                                       combine(bc(a_ref, 6), pltpu.roll(b_lo, 6, 0)))))
        out.append(jnp.where(sub < 2, combine(bc(a_ref, 7), b_lo), combine(a_hi, bc(b_ref, 0))))
        out.append(combine(a_hi, bc(b_ref, 0)))
        return out

    pos = [sub, sub + 8, sub + 16,
           jnp.where(sub < 5, sub + 32, sub + (64 - 5)),
           jnp.where(sub < 4, sub + 48, jnp.where(sub < 6, sub + (80 - 4), sub + (96 - 6))),
           jnp.where(sub < 2, sub + 112, (sub + 6) * 16),
           (sub + 14) * 16]
    cand = tiles(v1_ref, v2_ref, lambda a, b: a + b)
    cand[-1] = jnp.where(sub < 2, cand[-1], neg)
    code = tiles(i1_ref, i2_ref, lambda a, b: a * PEER_NKEYS + b)
    key = [p * (1 << _CODE_BITS) + c for p, c in zip(pos, code)]
    big = jnp.int32(1 << 30)

    def tree(xs, op):
        xs = list(xs)
        while len(xs) > 1:
            xs = [op(xs[k], xs[k + 1]) for k in range(0, len(xs) - 1, 2)] + ([xs[-1]] if len(xs) % 2 else [])
        return xs[0]

    for kk in range(PEER_TOPK):
        m = jnp.max(tree(cand, jnp.maximum), axis=0, keepdims=True)
        kmin = jnp.min(tree([jnp.where(c == m, k, big) for c, k in zip(cand, key)], jnp.minimum), axis=0, keepdims=True)
        best_ref[kk:kk + 1, lanes] = m
        code_ref[kk:kk + 1, lanes] = kmin & ((1 << _CODE_BITS) - 1)
        cand = [jnp.where(k == kmin, neg, c) for c, k in zip(cand, key)]


def _peer_up_kernel(x_ref, u_ref, q_ref, keys_ref, w_ref, isel_ref, jsel_ref,
                    a0_ref, a1_ref, v1_ref, i1_ref, v2_ref, i2_ref, best_ref, codeh_ref, codet_ref, gatet_ref,
                    iseln_ref, jseln_ref, gaten_ref, *, ne):
    i = pl.program_id(0)
    e = pl.program_id(1)
    tt = x_ref.shape[0]
    nsub = u_ref.shape[1] // PEER_NKEYS
    prev = (i + 1) % 2
    cur = i % 2
    chunks = [slice(c, c + LANES) for c in range(0, tt, LANES)]

    def pick(e_src, src, r_lo, r_hi):
        isel_p, jsel_p = iseln_ref.at[prev], jseln_ref.at[prev]
        for r0 in range(r_lo, r_hi, GATHER_ROWS):
            rows = slice(r0, r0 + GATHER_ROWS)
            isel = isel_p[rows, :]
            jsel = jsel_p[rows, :]
            acc = w_ref[rows, :]
            for ii in range(nsub):
                got = jnp.take_along_axis(src[rows, ii * PEER_NKEYS:(ii + 1) * PEER_NKEYS], jsel, axis=1)
                acc = jnp.where(isel == e_src * nsub + ii, got, acc)
            w_ref[rows, :] = acc
        return acc

    def step(dst, src, val_ref, idx_ref, pair):
        assert len(chunks) == 2 * STEP_SLICES
        s = _dot(keys_ref[0], q_ref[...])
        eb = u_ref.shape[1]
        never = e < 0
        bounds = [0] + [2 * c + 1 for c in range(STEP_SLICES)] + [len(chunks)]
        deps = ()
        for c in range(STEP_SLICES + 1):
            nxt = deps
            if c < STEP_SLICES:
                cols = slice(c * eb // STEP_SLICES, (c + 1) * eb // STEP_SLICES)
                res = _dot(x_ref[...], u_ref[:, cols])
                dst[:, cols] = res
                acc = pick(e - 1, src, c * tt // STEP_SLICES, (c + 1) * tt // STEP_SLICES)
                nxt = (res[:PEER_NKEYS, :LANES], jnp.broadcast_to(acc[0:1, :], (PEER_NKEYS, LANES)))
            for lanes in chunks[bounds[c]:bounds[c + 1]]:
                sc = s[:, lanes]
                for dep in deps:
                    sc = jnp.where(never, dep, sc)
                _top16(sc, val_ref, idx_ref, lanes)
                if pair:
                    _pair_top16(v1_ref, i1_ref, v2_ref, i2_ref, best_ref, codeh_ref, lanes)
            deps = nxt

    @pl.when((i == 0) & (e == 0))
    def _():
        a1_ref[...] = jnp.zeros_like(a1_ref)
        iseln_ref[...] = jnp.zeros_like(iseln_ref)
        jseln_ref[...] = jnp.zeros_like(jseln_ref)
        gaten_ref[...] = jnp.zeros_like(gaten_ref)

    @pl.when(e == 0)
    def _():
        w_ref[...] = jnp.zeros_like(w_ref)

    @pl.when((e < ne) & (e % 2 == 0))
    def _():
        step(a0_ref, a1_ref, v1_ref, i1_ref, False)

    @pl.when((e < ne) & (e % 2 == 1))
    def _():
        step(a1_ref, a0_ref, v2_ref, i2_ref, True)
        best = best_ref[...]
        ex = jnp.exp(best - best[0:1, :])
        row0 = pl.multiple_of((e // 2) * PEER_TOPK, PEER_TOPK)
        gatet_ref[pl.ds(row0, PEER_TOPK), :] = ex / jnp.sum(ex, axis=0, keepdims=True)
        codet_ref[pl.ds(row0, PEER_TOPK), :] = codeh_ref[...]

    @pl.when(e == ne)
    def _():
        pick(e - 1, a1_ref if ne % 2 == 0 else a0_ref, 0, tt)
        s = w_ref[...]
        gelu = 0.5 * s * (1.0 + lax.erf(s * (2.0 ** -0.5)))
        w_ref[...] = gaten_ref[prev] * gelu
        code_n = codet_ref[...].T
        isel = code_n >> 7
        jsel = code_n & (PEER_NKEYS - 1)
        isel_ref[...] = isel
        jsel_ref[...] = jsel
        iseln_ref[cur] = isel
        jseln_ref[cur] = jsel
        gaten_ref[cur] = gatet_ref[...].T


def _peer_up(h2b, ut, qt, keys, tt, eb):
    t = h2b.shape[0]
    ne = ut.shape[1] // eb
    assert ne == N_KEYSETS
    ntile = t // tt
    half = PEER_DK // 2
    tok = lambda i, e: (jnp.maximum(i - 1, 0), 0)
    rt = lambda i, e: (jnp.minimum(i, ntile - 1), 0)
    vm = lambda shape, dt: pltpu.VMEM(shape, dt)
    return pl.pallas_call(
        functools.partial(_peer_up_kernel, ne=ne),
        grid=(ntile + 1, ne + 1),
        in_specs=[pl.BlockSpec((tt, D_MODEL), tok),
                  pl.BlockSpec((D_MODEL, eb), lambda i, e: (0, jnp.minimum(e, ne - 1))),
                  pl.BlockSpec((half, tt), lambda i, e: (jnp.minimum(e, ne - 1), jnp.minimum(i, ntile - 1))),
                  pl.BlockSpec((1, PEER_NKEYS, half), lambda i, e: (jnp.minimum(e, ne - 1), 0, 0))],
        out_specs=[pl.BlockSpec((tt, N_SEL), tok), pl.BlockSpec((tt, N_SEL), rt), pl.BlockSpec((tt, N_SEL), rt)],
        out_shape=[jax.ShapeDtypeStruct((t, N_SEL), F32), jax.ShapeDtypeStruct((t, N_SEL), I32),
                   jax.ShapeDtypeStruct((t, N_SEL), I32)],
        scratch_shapes=[vm((tt, eb), F32), vm((tt, eb), F32),
                        vm((PEER_TOPK, tt), F32), vm((PEER_TOPK, tt), I32),
                        vm((PEER_TOPK, tt), F32), vm((PEER_TOPK, tt), I32),
                        vm((PEER_TOPK, tt), F32), vm((PEER_TOPK, tt), I32),
                        vm((N_SEL, tt), I32), vm((N_SEL, tt), F32),
                        vm((2, tt, N_SEL), I32), vm((2, tt, N_SEL), I32), vm((2, tt, N_SEL), F32)],
        compiler_params=_cparams(("arbitrary", "arbitrary")),
        name="peer_up",
    )(h2b, ut, qt, keys)


def _peer_down_kernel(isel_ref, jsel_ref, w_ref, v_ref, h2_ref, lg_ref, lb_ref, o_ref, wd_ref, acc_ref, *, ib):
    e = pl.program_id(1)
    tt = isel_ref.shape[0]
    hi_mask = jnp.uint32(0xFFFF0000)

    @pl.when(e == 0)
    def _():
        acc_ref[...] = jnp.zeros_like(acc_ref)
        sub = lax.broadcasted_iota(I32, (PEER_NKEYS, N_SEL), 0)

        half = PEER_NKEYS // 2

        def per_group(gi, carry):
            base = pl.multiple_of(gi * TOKEN_GROUP, TOKEN_GROUP)
            isb = isel_ref[pl.ds(base, TOKEN_GROUP), :]
            jsb = jsel_ref[pl.ds(base, TOKEN_GROUP), :]
            wb = w_ref[pl.ds(base, TOKEN_GROUP), :]
            for r in range(TOKEN_GROUP):
                pt = jnp.where(sub == isb[r:r + 1, :], wb[r:r + 1, :], 0.0).astype(BF16)
                qt = jnp.where(sub == jsb[r:r + 1, :], 1.0, 0.0).astype(BF16)
                gb = pltpu.bitcast(_dot_nt(pt, qt), jnp.uint32)
                packed = (gb[half:] & hi_mask) | (gb[:half] >> 16)
                wd_ref[pl.ds(pl.multiple_of((base + r) * W_PITCH, 8), half), :] = packed
            return carry

        lax.fori_loop(0, tt // TOKEN_GROUP, per_group, 0)

    parts = []
    for ii in range(ib):
        pk = wd_ref[pl.ds(e * ib + ii, tt, stride=W_PITCH), :]
        parts.append(pltpu.bitcast(pk << 16, F32).astype(BF16))
        parts.append(pltpu.bitcast(pk & hi_mask, F32).astype(BF16))
    acc_ref[...] += _dot(jnp.concatenate(parts, axis=1), v_ref[...])

    @pl.when(e == pl.num_programs(1) - 1)
    def _():
        o_ref[...] = _layer_norm(ALPHA * h2_ref[...] + acc_ref[...], lg_ref[...], lb_ref[...])


def _peer_down(isel, jsel, w, vperm, h2, lg, lb, tt, ib):
    t = h2.shape[0]
    ne = (PEER_NKEYS // 2) // ib
    vb = ib * 2 * PEER_NKEYS
    sel = pl.BlockSpec((tt, N_SEL), lambda i, e: (i, 0))
    full = lambda a: pl.BlockSpec(a.shape, lambda i, e: (0,) * a.ndim)
    return pl.pallas_call(
        functools.partial(_peer_down_kernel, ib=ib),
        grid=(t // tt, ne),
        in_specs=[sel, sel, sel, pl.BlockSpec((vb, D_MODEL), lambda i, e: (e, 0)),
                  pl.BlockSpec((tt, D_MODEL), lambda i, e: (i, 0)), full(lg), full(lb)],
        out_specs=pl.BlockSpec((tt, D_MODEL), lambda i, e: (i, 0)),
        out_shape=jax.ShapeDtypeStruct((t, D_MODEL), F32),
        scratch_shapes=[pltpu.VMEM((tt * W_PITCH, N_SEL), jnp.uint32), pltpu.VMEM((tt, D_MODEL), F32)],
        compiler_params=_cparams(("parallel", "arbitrary")),
        name="peer_down",
    )(isel, jsel, w, vperm, h2, lg, lb)


def _rope_tables(seq):
    half = QK_ROPE_DIM // 2
    inv = 1.0 / (ROPE_THETA ** (jnp.arange(0, QK_ROPE_DIM, 2, dtype=F32) / QK_ROPE_DIM))
    ang = jnp.arange(seq, dtype=F32)[:, None] * inv[None, :]
    cos, sin = jnp.cos(ang), jnp.sin(ang)
    c32 = jnp.concatenate([cos, cos], axis=1)
    s32 = jnp.concatenate([-sin, sin], axis=1)
    z = lambda w: jnp.zeros((seq, w), F32)
    scale = (QK_NOPE_DIM + QK_ROPE_DIM) ** -0.5 * math.log2(math.e)
    pad = HEAD_SLOT - QK_NOPE_DIM - QK_ROPE_DIM
    cq = jnp.concatenate([jnp.full((seq, QK_NOPE_DIM), scale, F32), scale * c32, z(pad)], axis=1)
    sq = jnp.concatenate([z(QK_NOPE_DIM), scale * s32, z(pad)], axis=1)
    ck = jnp.concatenate([z(QK_NOPE_DIM), c32, z(pad)], axis=1)
    sk = jnp.concatenate([z(QK_NOPE_DIM), s32, z(pad)], axis=1)
    return cq, sq, ck, sk


def _swap_halves(w):
    half = w.shape[-1] // 2
    return jnp.concatenate([w[..., half:], w[..., :half]], axis=-1)


def _slot(w, offset):
    return jnp.pad(w, ((0, 0), (offset, HEAD_SLOT - offset - w.shape[1])))


def _position_features(seq):
    t = jnp.linspace(0.0, 1.0, seq, dtype=F32)[:, None]
    bands = (HY_EMB_DIM - 1) // 2
    w = 2.0 * math.pi * jnp.arange(seq, dtype=F32) / seq
    f = jnp.linspace(1e-4, bands - 1, bands, dtype=F32)
    ang = w[:, None] * f[None, :]
    return jnp.concatenate([t, jnp.cos(ang), -jnp.sin(ang)], axis=-1)


def kernel(x, emb_ln_g, emb_ln_b, w_in, q_norm_g, w_uq, kv_norm_g, w_ukv, hy_short_w, hy_short_b, hy_filt_w1,
           hy_filt_b1, hy_filt_freq1, hy_filt_w2, hy_filt_b2, hy_filt_freq2, hy_filt_w3, hy_bias, attn_out_g,
           hy_out_g, w_o, ln_mix_g, ln_mix_b, peer_wq, peer_sub_keys, peer_u, peer_v, ln_ffn_g, ln_ffn_b):
    batch, seq, _ = x.shape
    assert w_in.shape[0] == DEPTH == 1 and seq % TOEP == 0
    t = batch * seq
    nblk = seq // TOEP
    r2 = lambda a: a.reshape(1, -1)
    x2 = x.reshape(t, D_MODEL)
    lg, lb = r2(emb_ln_g), r2(emb_ln_b)

    wi = w_in[0]
    w_kr = wi[:, OFF_CKV:OFF_KR]
    wall = jnp.concatenate([wi[:, :OFF_CKV], _slot(w_kr, QK_NOPE_DIM), _slot(_swap_halves(w_kr), QK_NOPE_DIM),
                            wi[:, OFF_KR:]], axis=1).astype(BF16)
    dq = QK_NOPE_DIM + QK_ROPE_DIM
    wuq = w_uq[0].reshape(Q_LORA_RANK, ATTN_HEADS, dq)
    wq = jnp.pad(wuq, ((0, 0), (0, 0), (0, HEAD_SLOT - dq))).reshape(Q_LORA_RANK, -1).astype(BF16)
    wqs = jnp.pad(_swap_halves(wuq[..., QK_NOPE_DIM:]),
                  ((0, 0), (0, 0), (QK_NOPE_DIM, HEAD_SLOT - dq))).reshape(Q_LORA_RANK, -1).astype(BF16)
    wukv = w_ukv[0].reshape(KV_LORA_RANK, ATTN_HEADS, QK_NOPE_DIM + V_HEAD_DIM)
    wk = jnp.pad(wukv[..., :QK_NOPE_DIM],
                 ((0, 0), (0, 0), (0, HEAD_SLOT - QK_NOPE_DIM))).reshape(KV_LORA_RANK, -1).astype(BF16)
    wv = wukv[..., QK_NOPE_DIM:].reshape(KV_LORA_RANK, -1).T.astype(BF16)
    cq, sq, ck, sk = _rope_tables(seq)

    tt_proj = min(512, seq)
    q, k, vt, hy = _ln_proj(x2, lg, lb, wall, r2(q_norm_g[0]), wq, wqs, r2(kv_norm_g[0]), wk, wv,
                           cq, sq, ck, sk, seq, tt_proj)

    a = _attention(q, k, vt, attn_out_g[0].reshape(-1, 1), batch, seq, min(256, seq))

    ut = _short_conv(hy.reshape(batch, seq, -1), hy_short_w[0], r2(hy_short_b[0]))
    ut = ut.reshape((HY_ORDER + 1) * HY_WIDTH, nblk * batch, TOEP)

    lag = jnp.abs(jnp.arange(2 * seq) - seq)
    feats = _position_features(seq)
    zt = jnp.pad(feats[jnp.minimum(lag, seq - 1)].T, ((0, HY_FILTER_HIDDEN - HY_EMB_DIM), (0, 0)))
    col = lambda a_: a_.reshape(-1, 1)
    w1t = jnp.pad(hy_filt_w1[0].T, ((0, 0), (0, HY_FILTER_HIDDEN - HY_EMB_DIM)))
    w3t = hy_filt_w3[0].T.reshape(HY_ORDER, 2, HY_WIDTH, HY_FILTER_HIDDEN)
    deltas = jnp.abs(jnp.linspace(math.log(HY_TARGET) / HY_SLOW_DECAY, math.log(HY_TARGET) / HY_FAST_DECAY,
                                  HY_WIDTH, dtype=F32))
    g = _filters(zt, w1t, col(hy_filt_b1[0]), col(hy_filt_freq1[0]), hy_filt_w2[0].T, col(hy_filt_b2[0]),
                 col(hy_filt_freq2[0]), w3t, col(deltas))
    yt = _hyena(hy_bias[0], ut, g.reshape(HY_ORDER, HY_WIDTH, 1, 2 * seq), batch, nblk, 8)

    h2, h2b, qt = _mix(x2, lg, lb, a, yt.reshape(HY_WIDTH, nblk * batch * TOEP), r2(hy_out_g[0]),
                       w_o[0].astype(BF16), r2(ln_mix_g[0]), r2(ln_mix_b[0]), peer_wq[0].T.astype(BF16), batch, nblk)

    keys = peer_sub_keys[0].astype(BF16).reshape(N_KEYSETS, PEER_NKEYS, PEER_DK // 2)
    w, isel, jsel = _peer_up(h2b, peer_u[0].astype(BF16).T, qt, keys, min(1024, t), 1024)
    ib = 8
    half = PEER_NKEYS // 2
    vperm = peer_v[0].astype(BF16).reshape(2, half // ib, ib, PEER_NKEYS, D_MODEL)
    vperm = vperm.transpose(1, 2, 0, 3, 4).reshape(PEER_NKEYS * PEER_NKEYS, D_MODEL)
    out = _peer_down(isel, jsel, w, vperm, h2, r2(ln_ffn_g[0]), r2(ln_ffn_b[0]), min(512, t), ib)
    return out.reshape(batch, seq, D_MODEL)
```

```python
import functools
import math

import jax
import jax.numpy as jnp
from jax import lax
from jax.experimental import pallas as pl
from jax.experimental.pallas import tpu as pltpu

F32 = jnp.float32
BF16 = jnp.bfloat16
I32 = jnp.int32

D_MODEL = 1024
ATTN_HEADS = 8
QK_NOPE_DIM = 64
QK_ROPE_DIM = 32
V_HEAD_DIM = 64
Q_LORA_RANK = 256
KV_LORA_RANK = 128
ATTN_WIDTH = ATTN_HEADS * V_HEAD_DIM
ROPE_THETA = 10000.0
HY_WIDTH = D_MODEL - ATTN_WIDTH
HY_ORDER = 2
HY_GROUPS = 8
HY_SHORT = 3
HY_EMB_DIM = 33
HY_FILTER_HIDDEN = 64
HY_FAST_DECAY = 0.3
HY_SLOW_DECAY = 1.5
HY_TARGET = 1e-2
OFF_CQ = Q_LORA_RANK
OFF_CKV = OFF_CQ + KV_LORA_RANK
OFF_KR = OFF_CKV + QK_ROPE_DIM
PEER_HEADS = 8
PEER_NKEYS = 128
PEER_DK = 128
PEER_TOPK = 16
DEPTH = 1
ALPHA = (2 * DEPTH) ** 0.25
LN_EPS = 1e-5
RMS_EPS = 1e-6

LANES = 128
HEAD_SLOT = 128
TOEP = 256
N_SEL = PEER_HEADS * PEER_TOPK
W_PITCH = 72
GATHER_ROWS = 8
STEP_SLICES = 4
VMEM_LIMIT = 56 * 1024 * 1024


def _cparams(sem):
    return pltpu.CompilerParams(dimension_semantics=sem, vmem_limit_bytes=VMEM_LIMIT)


def _layer_norm(x, g, b):
    mu = jnp.mean(x, axis=-1, keepdims=True)
    xc = x - mu
    var = jnp.mean(xc * xc, axis=-1, keepdims=True)
    return xc * lax.rsqrt(var + LN_EPS) * g + b


def _rms(x, g):
    return x * lax.rsqrt(jnp.mean(x * x, axis=-1, keepdims=True) + RMS_EPS) * g


def _dot(a, b):
    return jnp.dot(a, b, preferred_element_type=F32)


def _dot_nt(a, b):
    return lax.dot_general(a, b, (((1,), (1,)), ((), ())), preferred_element_type=F32)


N_MLA_COLS = Q_LORA_RANK + KV_LORA_RANK + 2 * HEAD_SLOT


def _ln_proj_kernel(x_ref, lg_ref, lb_ref, wall_ref, gq_ref, wq_ref, wqs_ref, gkv_ref, wk_ref, wv_ref,
                    cq_ref, sq_ref, ck_ref, sk_ref, q_ref, k_ref, vt_ref, hy_ref):
    h = _layer_norm(x_ref[...], lg_ref[...], lb_ref[...])
    proj = _dot(h.astype(BF16), wall_ref[...])
    c_q = proj[:, :OFF_CQ]
    c_kv = proj[:, OFF_CQ:OFF_CKV]
    kr = proj[:, OFF_CKV:OFF_CKV + HEAD_SLOT]
    kr_sw = proj[:, OFF_CKV + HEAD_SLOT:N_MLA_COLS]
    hy_ref[...] = proj[:, N_MLA_COLS:].astype(BF16)

    nq = _rms(c_q, gq_ref[...]).astype(BF16)
    qa = _dot(nq, wq_ref[...])
    qb = _dot(nq, wqs_ref[...])
    nkv = _rms(c_kv, gkv_ref[...]).astype(BF16)
    kn = _dot(nkv, wk_ref[...])
    vt_ref[...] = _dot_nt(wv_ref[...], nkv).astype(BF16)
    k_pe = kr * ck_ref[...] + kr_sw * sk_ref[...]
    cq, sq = cq_ref[...], sq_ref[...]
    for hd in range(ATTN_HEADS):
        sl = slice(hd * HEAD_SLOT, (hd + 1) * HEAD_SLOT)
        q_ref[:, sl] = (qa[:, sl] * cq + qb[:, sl] * sq).astype(BF16)
        k_ref[:, sl] = (kn[:, sl] + k_pe).astype(BF16)


def _ln_proj(x2, lg, lb, wall, gq, wq, wqs, gkv, wk, wv, cq, sq, ck, sk, seq, tt):
    t = x2.shape[0]
    npos = seq // tt
    full = lambda a: pl.BlockSpec(a.shape, lambda i: (0,) * a.ndim)
    tab = pl.BlockSpec((tt, HEAD_SLOT), lambda i: (i % npos, 0))
    row = lambda w: pl.BlockSpec((tt, w), lambda i: (i, 0))
    hw = wall.shape[1] - N_MLA_COLS
    return pl.pallas_call(
        _ln_proj_kernel,
        grid=(t // tt,),
        in_specs=[row(D_MODEL), full(lg), full(lb), full(wall), full(gq), full(wq), full(wqs), full(gkv),
                  full(wk), full(wv), tab, tab, tab, tab],
        out_specs=[row(ATTN_HEADS * HEAD_SLOT), row(ATTN_HEADS * HEAD_SLOT),
                   pl.BlockSpec((ATTN_WIDTH, tt), lambda i: (0, i)), row(hw)],
        out_shape=[jax.ShapeDtypeStruct((t, ATTN_HEADS * HEAD_SLOT), BF16),
                   jax.ShapeDtypeStruct((t, ATTN_HEADS * HEAD_SLOT), BF16),
                   jax.ShapeDtypeStruct((ATTN_WIDTH, t), BF16),
                   jax.ShapeDtypeStruct((t, hw), BF16)],
        compiler_params=_cparams(("parallel",)),
        name="ln_proj",
    )(x2, lg, lb, wall, gq, wq, wqs, gkv, wk, wv, cq, sq, ck, sk)


def _attn_kernel(q_ref, k_ref, vt_ref, g_ref, o_ref, s0_ref, m0_ref, s1_ref, m1_ref, *, nsteps):
    n = pl.program_id(0)

    def scores(s_ref, m_ref):
        for hh in range(2):
            sl = slice(hh * HEAD_SLOT, (hh + 1) * HEAD_SLOT)
            s = _dot_nt(k_ref[:, sl], q_ref[:, sl])
            s_ref[hh] = s
            m_ref[hh] = jnp.max(s, axis=0, keepdims=True)

    def finish(s_ref, m_ref):
        outs = []
        ones = jnp.ones((16, vt_ref.shape[1]), BF16)
        for hh in range(2):
            vs = slice(hh * V_HEAD_DIM, (hh + 1) * V_HEAD_DIM)
            p = jnp.exp2(s_ref[hh] - m_ref[hh]).astype(BF16)
            ov = _dot(jnp.concatenate([vt_ref[vs, :], ones], axis=0), p)
            o = ov[:V_HEAD_DIM] / ov[V_HEAD_DIM:V_HEAD_DIM + 1]
            ms = jnp.mean(o * o, axis=0, keepdims=True)
            outs.append(o * lax.rsqrt(ms + RMS_EPS) * g_ref[vs, :])
        o_ref[...] = jnp.concatenate(outs, axis=0).T.astype(BF16)

    @pl.when(n == 0)
    def _():
        s1_ref[...] = jnp.zeros_like(s1_ref)
        m1_ref[...] = jnp.zeros_like(m1_ref)

    @pl.when((n < nsteps) & (n % 2 == 0))
    def _():
        finish(s1_ref, m1_ref)
        scores(s0_ref, m0_ref)

    @pl.when((n < nsteps) & (n % 2 == 1))
    def _():
        finish(s0_ref, m0_ref)
        scores(s1_ref, m1_ref)

    @pl.when(n == nsteps)
    def _():
        if nsteps % 2 == 0:
            finish(s1_ref, m1_ref)
        else:
            finish(s0_ref, m0_ref)


def _attention(q, k, vt, g, batch, seq, tq):
    t = q.shape[0]
    nq = seq // tq
    npair = ATTN_HEADS // 2
    nsteps = batch * npair * nq

    def tile(n):
        n = jnp.clip(n, 0, nsteps - 1)
        return n // (npair * nq), (n // nq) % npair, n % nq

    def q_map(n):
        b, p, i = tile(n)
        return b * nq + i, p

    def k_map(n):
        b, p, _ = tile(n)
        return b, p

    def vt_map(n):
        b, p, _ = tile(n - 1)
        return p, b

    def g_map(n):
        return tile(n - 1)[1], 0

    def o_map(n):
        b, p, i = tile(n - 1)
        return b * nq + i, p

    return pl.pallas_call(
        functools.partial(_attn_kernel, nsteps=nsteps),
        grid=(nsteps + 1,),
        in_specs=[pl.BlockSpec((tq, 2 * HEAD_SLOT), q_map),
                  pl.BlockSpec((seq, 2 * HEAD_SLOT), k_map),
                  pl.BlockSpec((2 * V_HEAD_DIM, seq), vt_map),
                  pl.BlockSpec((2 * V_HEAD_DIM, 1), g_map)],
        out_specs=pl.BlockSpec((tq, 2 * V_HEAD_DIM), o_map),
        out_shape=jax.ShapeDtypeStruct((t, ATTN_WIDTH), BF16),
        scratch_shapes=[pltpu.VMEM((2, seq, tq), F32), pltpu.VMEM((2, 1, tq), F32),
                        pltpu.VMEM((2, seq, tq), F32), pltpu.VMEM((2, 1, tq), F32)],
        compiler_params=_cparams(("arbitrary",)),
        name="attention",
    )(q, k, vt, g)


def _short_conv_kernel(prev_ref, x_ref, next_ref, w_ref, b_ref, o_ref):
    j = pl.program_id(1)
    nj = pl.num_programs(1)
    x = x_ref[0].astype(F32)
    rows = x.shape[0]
    before = jnp.where(j > 0, prev_ref[0, 7:8, :].astype(F32), 0.0)
    after = jnp.where(j < nj - 1, next_ref[0, 0:1, :].astype(F32), 0.0)
    rid = lax.broadcasted_iota(I32, x.shape, 0)
    xm = jnp.where(rid == 0, before, pltpu.roll(x, 1, 0))
    xp = jnp.where(rid == rows - 1, after, pltpu.roll(x, rows - 1, 0))
    u = b_ref[...] + xm * w_ref[0:1, :] + x * w_ref[1:2, :] + xp * w_ref[2:3, :]
    o_ref[...] = u.T.astype(BF16)


def _short_conv(hy3, w, b):
    batch, seq, width = hy3.shape
    nj = seq // TOEP
    sub = TOEP // 8
    last8 = seq // 8 - 1
    return pl.pallas_call(
        _short_conv_kernel,
        grid=(batch, nj),
        in_specs=[pl.BlockSpec((1, 8, width), lambda bb, j: (bb, jnp.maximum(j * sub - 1, 0), 0)),
                  pl.BlockSpec((1, TOEP, width), lambda bb, j: (bb, j, 0)),
                  pl.BlockSpec((1, 8, width), lambda bb, j: (bb, jnp.minimum((j + 1) * sub, last8), 0)),
                  pl.BlockSpec(w.shape, lambda bb, j: (0, 0)),
                  pl.BlockSpec(b.shape, lambda bb, j: (0, 0))],
        out_specs=pl.BlockSpec((width, TOEP), lambda bb, j: (0, j * batch + bb)),
        out_shape=jax.ShapeDtypeStruct((width, nj * batch * TOEP), BF16),
        compiler_params=_cparams(("parallel", "parallel")),
        name="short_conv",
    )(hy3, hy3, hy3, w, b)


def _filter_kernel(z_ref, w1_ref, b1_ref, f1_ref, w2_ref, b2_ref, f2_ref, w3_ref, dl_ref, g_ref):
    hp = lax.Precision.HIGHEST
    z = z_ref[...]
    n = z.shape[1]
    half = n // 2
    h1 = jnp.sin(f1_ref[...] * (jnp.dot(w1_ref[...], z, precision=hp, preferred_element_type=F32) + b1_ref[...]))
    h2 = jnp.sin(f2_ref[...] * (jnp.dot(w2_ref[...], h1, precision=hp, preferred_element_type=F32) + b2_ref[...]))
    decay = jnp.exp(-z[0:1, :] * dl_ref[...])
    lane = lax.broadcasted_iota(I32, (1, n), 1)
    masks = (lane >= half, (lane >= 1) & (lane <= half))
    out = None
    for d in range(2):
        hd = jnp.dot(w3_ref[0, d], h2, precision=hp, preferred_element_type=F32) * decay
        hd = jnp.where(masks[d], hd, 0.0)
        hd = hd / (jnp.sum(jnp.abs(hd), axis=1, keepdims=True) + 1e-6)
        out = hd if out is None else out + hd
    g_ref[0] = out


def _filters(zt, w1t, b1, f1, w2t, b2, f2, w3t, dl):
    n = zt.shape[1]
    cb = 128
    full = lambda a: pl.BlockSpec(a.shape, lambda o, c: (0,) * a.ndim)
    return pl.pallas_call(
        _filter_kernel,
        grid=(HY_ORDER, HY_WIDTH // cb),
        in_specs=[full(zt), full(w1t), full(b1), full(f1), full(w2t), full(b2), full(f2),
                  pl.BlockSpec((1, 2, cb, HY_FILTER_HIDDEN), lambda o, c: (o, 0, c, 0)),
                  pl.BlockSpec((cb, 1), lambda o, c: (c, 0))],
        out_specs=pl.BlockSpec((1, cb, n), lambda o, c: (o, c, 0)),
        out_shape=jax.ShapeDtypeStruct((HY_ORDER, HY_WIDTH, n), F32),
        compiler_params=_cparams(("parallel", "parallel")),
        name="filters",
    )(zt, w1t, b1, f1, w2t, b2, f2, w3t, dl)


def _hyena_kernel(bias_ref, v_ref, x1_ref, x2_ref, g_ref, o_ref, acc_ref, *, batch, nblk, cb):
    rows = batch * nblk
    seq = nblk * TOEP
    c0 = pl.program_id(0) * cb
    gates = (x1_ref, x2_ref)

    def per_channel(c, carry):
        zf = v_ref[c].astype(F32)
        for o in range(HY_ORDER):
            zb = zf.astype(BF16)
            acc_ref[...] = jnp.zeros_like(acc_ref)
            for d in range(-(nblk - 1), nblk):
                base = seq + TOEP * d - TOEP
                win = g_ref[o, c, :, pl.ds(base, 2 * TOEP)]
                w = pltpu.roll(jnp.broadcast_to(win, (TOEP, 2 * TOEP)), TOEP, 1, stride=1, stride_axis=0)
                w = w[:, :TOEP].astype(BF16)
                n = rows - batch * abs(d)
                if d >= 0:
                    acc_ref[batch * d:, :] += _dot(zb[:n], w)
                else:
                    acc_ref[:n, :] += _dot(zb[batch * (-d):], w)
            zf = gates[o][c].astype(F32) * (acc_ref[...] + bias_ref[o, c0 + c] * zf)
        o_ref[c] = zf.astype(BF16)
        return carry

    lax.fori_loop(0, cb, per_channel, 0)


def _hyena(bias, ut, g4, batch, nblk, cb):
    rows = batch * nblk
    nc = HY_WIDTH // cb
    kern = functools.partial(_hyena_kernel, batch=batch, nblk=nblk, cb=cb)
    blk = lambda off: pl.BlockSpec((cb, rows, TOEP), lambda i: (off * nc + i, 0, 0))
    return pl.pallas_call(
        kern,
        grid=(nc,),
        in_specs=[pl.BlockSpec(memory_space=pltpu.SMEM), blk(0), blk(1), blk(2),
                  pl.BlockSpec((HY_ORDER, cb, 1, g4.shape[3]), lambda i: (0, i, 0, 0))],
        out_specs=pl.BlockSpec((cb, rows, TOEP), lambda i: (i, 0, 0)),
        out_shape=jax.ShapeDtypeStruct((HY_WIDTH, rows, TOEP), BF16),
        scratch_shapes=[pltpu.VMEM((rows, TOEP), F32)],
        compiler_params=_cparams(("parallel",)),
        name="hyena",
    )(bias, ut, ut, ut, g4)


def _mix_kernel(x_ref, lg_ref, lb_ref, a_ref, yt_ref, hg_ref, wo_ref, mg_ref, mb_ref, wq_ref,
                h2_ref, h2b_ref, qt_ref, cat_ref):
    y = yt_ref[...].astype(F32).T
    cat_ref[:, :ATTN_WIDTH] = a_ref[...]
    gw = HY_WIDTH // HY_GROUPS
    for g in range(HY_GROUPS):
        sl = slice(g * gw, (g + 1) * gw)
        cat_ref[:, ATTN_WIDTH + g * gw:ATTN_WIDTH + (g + 1) * gw] = _rms(y[:, sl], hg_ref[:, sl]).astype(BF16)
    mix = _dot(cat_ref[...], wo_ref[...])
    h = _layer_norm(x_ref[...], lg_ref[...], lb_ref[...])
    h2 = _layer_norm(ALPHA * h + mix, mg_ref[...], mb_ref[...])
    h2_ref[...] = h2
    h2b = h2.astype(BF16)
    h2b_ref[...] = h2b
    qt_ref[...] = _dot_nt(wq_ref[...], h2b).astype(BF16)


def _mix(x2, lg, lb, a, yt2, hg, wo, mg, mb, wqt, batch, nblk):
    t = x2.shape[0]
    full = lambda w: pl.BlockSpec(w.shape, lambda i: (0,) * w.ndim)
    row = lambda w: pl.BlockSpec((TOEP, w), lambda i: (i, 0))
    return pl.pallas_call(
        _mix_kernel,
        grid=(t // TOEP,),
        in_specs=[row(D_MODEL), full(lg), full(lb), row(ATTN_WIDTH),
                  pl.BlockSpec((HY_WIDTH, TOEP), lambda i: (0, (i % nblk) * batch + i // nblk)),
                  full(hg), full(wo), full(mg), full(mb), full(wqt)],
        out_specs=[row(D_MODEL), row(D_MODEL), pl.BlockSpec((wqt.shape[0], TOEP), lambda i: (0, i))],
        out_shape=[jax.ShapeDtypeStruct((t, D_MODEL), F32), jax.ShapeDtypeStruct((t, D_MODEL), BF16),
                   jax.ShapeDtypeStruct((wqt.shape[0], t), BF16)],
        scratch_shapes=[pltpu.VMEM((TOEP, D_MODEL), BF16)],
        compiler_params=_cparams(("parallel",)),
        name="mix",
    )(x2, lg, lb, a, yt2, hg, wo, mg, mb, wqt)


N_KEYSETS = 2 * PEER_HEADS


SORT_GROUP = 4


def _top16(s, val_ref, idx_ref, lanes):
    neg = -jnp.inf
    sub = lax.broadcasted_iota(I32, (8, s.shape[1]), 0)
    ntile = s.shape[0] // 8
    groups = []
    for g0 in range(0, ntile, SORT_GROUP):
        t = [s[8 * v:8 * v + 8, :] for v in range(g0, g0 + SORT_GROUP)]
        ix = [sub + 8 * v for v in range(g0, g0 + SORT_GROUP)]
        for end in range(SORT_GROUP - 1, 0, -1):
            for p in range(end):
                swap = t[p + 1] > t[p]
                t[p], t[p + 1] = jnp.where(swap, t[p + 1], t[p]), jnp.where(swap, t[p], t[p + 1])
                ix[p], ix[p + 1] = jnp.where(swap, ix[p + 1], ix[p]), jnp.where(swap, ix[p], ix[p + 1])
        groups.append((t, ix))
    for r in range(PEER_TOPK):
        heads = [t[0] for t, _ in groups]
        while len(heads) > 1:
            heads = [jnp.maximum(heads[k], heads[k + 1]) for k in range(0, len(heads), 2)]
        m = jnp.max(heads[0], axis=0, keepdims=True)
        cand = [jnp.where(t[0] == m, ix[0], PEER_NKEYS) for t, ix in groups]
        while len(cand) > 1:
            cand = [jnp.minimum(cand[k], cand[k + 1]) for k in range(0, len(cand), 2)]
        idx = jnp.min(cand[0], axis=0, keepdims=True)
        val_ref[r:r + 1, lanes] = m
        idx_ref[r:r + 1, lanes] = idx
        for t, ix in groups:
            hit = ix[0] == idx
            for p in range(SORT_GROUP - 1):
                t[p] = jnp.where(hit, t[p + 1], t[p])
                ix[p] = jnp.where(hit, ix[p + 1], ix[p])
            t[-1] = jnp.where(hit, neg, t[-1])


_CODE_BITS = 14


def _pair_top16(v1_ref, i1_ref, v2_ref, i2_ref, best_ref, code_ref, lanes):
    neg = -jnp.inf
    width = v1_ref[0:8, lanes].shape[1]
    sub = lax.broadcasted_iota(I32, (8, width), 0)
    bc = lambda ref, r: jnp.broadcast_to(ref[r:r + 1, lanes], (8, width))

    def tiles(a_ref, b_ref, combine):
        b_lo = b_ref[0:8, lanes]
        a_hi = pltpu.roll(a_ref[8:16, lanes], 2, 0)
        out = [combine(bc(a_ref, 0), b_lo), combine(bc(a_ref, 0), b_ref[8:16, lanes]), combine(bc(a_ref, 1), b_lo)]
        out.append(jnp.where(sub < 5, combine(bc(a_ref, 2), b_lo), combine(bc(a_ref, 4), pltpu.roll(b_lo, 5, 0))))
        out.append(jnp.where(sub < 4, combine(bc(a_ref, 3), b_lo),
                             jnp.where(sub < 6, combine(bc(a_ref, 5), pltpu.roll(b_lo, 4, 0)),
                                       combine(bc(a_ref, 6), pltpu.roll(b_lo, 6, 0)))))
        out.append(jnp.where(sub < 2, combine(bc(a_ref, 7), b_lo), combine(a_hi, bc(b_ref, 0))))
        out.append(combine(a_hi, bc(b_ref, 0)))
        return out

    pos = [sub, sub + 8, sub + 16,
           jnp.where(sub < 5, sub + 32, sub + (64 - 5)),
           jnp.where(sub < 4, sub + 48, jnp.where(sub < 6, sub + (80 - 4), sub + (96 - 6))),
           jnp.where(sub < 2, sub + 112, (sub + 6) * 16),
           (sub + 14) * 16]
    cand = tiles(v1_ref, v2_ref, lambda a, b: a + b)
    cand[-1] = jnp.where(sub < 2, cand[-1], neg)
    code = tiles(i1_ref, i2_ref, lambda a, b: a * PEER_NKEYS + b)
    key = [p * (1 << _CODE_BITS) + c for p, c in zip(pos, code)]
    big = jnp.int32(1 << 30)

    def tree(xs, op):
        xs = list(xs)
        while len(xs) > 1:
            xs = [op(xs[k], xs[k + 1]) for k in range(0, len(xs) - 1, 2)] + ([xs[-1]] if len(xs) % 2 else [])
        return xs[0]

    for kk in range(PEER_TOPK):
        m = jnp.max(tree(cand, jnp.maximum), axis=0, keepdims=True)
        kmin = jnp.min(tree([jnp.where(c == m, k, big) for c, k in zip(cand, key)], jnp.minimum), axis=0, keepdims=True)
        best_ref[kk:kk + 1, lanes] = m
        code_ref[kk:kk + 1, lanes] = kmin & ((1 << _CODE_BITS) - 1)
        cand = [jnp.where(k == kmin, neg, c) for c, k in zip(cand, key)]


def _peer_up_kernel(x_ref, u_ref, q_ref, keys_ref, w_ref, isel_ref, jsel_ref,
                    a0_ref, a1_ref, v1_ref, i1_ref, v2_ref, i2_ref, best_ref, codeh_ref, codet_ref, gatet_ref,
                    iseln_ref, jseln_ref, gaten_ref, *, ne):
    i = pl.program_id(0)
    e = pl.program_id(1)
    tt = x_ref.shape[0]
    nsub = u_ref.shape[1] // PEER_NKEYS
    prev = (i + 1) % 2
    cur = i % 2
    chunks = [slice(c, c + LANES) for c in range(0, tt, LANES)]

    def pick(e_src, src, r_lo, r_hi):
        isel_p, jsel_p = iseln_ref.at[prev], jseln_ref.at[prev]
        for r0 in range(r_lo, r_hi, GATHER_ROWS):
            rows = slice(r0, r0 + GATHER_ROWS)
            isel = isel_p[rows, :]
            jsel = jsel_p[rows, :]
            acc = w_ref[rows, :]
            for ii in range(nsub):
                got = jnp.take_along_axis(src[rows, ii * PEER_NKEYS:(ii + 1) * PEER_NKEYS], jsel, axis=1)
                acc = jnp.where(isel == e_src * nsub + ii, got, acc)
            w_ref[rows, :] = acc
        return acc

    def step(dst, src, val_ref, idx_ref, pair):
        assert len(chunks) == 2 * STEP_SLICES
        s = _dot(keys_ref[0], q_ref[...])
        eb = u_ref.shape[1]
        never = e < 0
        bounds = [0] + [2 * c + 1 for c in range(STEP_SLICES)] + [len(chunks)]
        deps = ()
        for c in range(STEP_SLICES + 1):
            nxt = deps
            if c < STEP_SLICES:
                cols = slice(c * eb // STEP_SLICES, (c + 1) * eb // STEP_SLICES)
                res = _dot(x_ref[...], u_ref[:, cols])
                dst[:, cols] = res
                acc = pick(e - 1, src, c * tt // STEP_SLICES, (c + 1) * tt // STEP_SLICES)
                nxt = (res[:PEER_NKEYS, :LANES], jnp.broadcast_to(acc[0:1, :], (PEER_NKEYS, LANES)))
            for lanes in chunks[bounds[c]:bounds[c + 1]]:
                sc = s[:, lanes]
                for dep in deps:
                    sc = jnp.where(never, dep, sc)
                _top16(sc, val_ref, idx_ref, lanes)
                if pair:
                    _pair_top16(v1_ref, i1_ref, v2_ref, i2_ref, best_ref, codeh_ref, lanes)
            deps = nxt

    @pl.when((i == 0) & (e == 0))
    def _():
        a1_ref[...] = jnp.zeros_like(a1_ref)
        iseln_ref[...] = jnp.zeros_like(iseln_ref)
        jseln_ref[...] = jnp.zeros_like(jseln_ref)
        gaten_ref[...] = jnp.zeros_like(gaten_ref)

    @pl.when(e == 0)
    def _():
        w_ref[...] = jnp.zeros_like(w_ref)

    @pl.when((e < ne) & (e % 2 == 0))
    def _():
        step(a0_ref, a1_ref, v1_ref, i1_ref, False)

    @pl.when((e < ne) & (e % 2 == 1))
    def _():
        step(a1_ref, a0_ref, v2_ref, i2_ref, True)
        best = best_ref[...]
        ex = jnp.exp(best - best[0:1, :])
        row0 = pl.multiple_of((e // 2) * PEER_TOPK, PEER_TOPK)
        gatet_ref[pl.ds(row0, PEER_TOPK), :] = ex / jnp.sum(ex, axis=0, keepdims=True)
        codet_ref[pl.ds(row0, PEER_TOPK), :] = codeh_ref[...]

    @pl.when(e == ne)
    def _():
        pick(e - 1, a1_ref if ne % 2 == 0 else a0_ref, 0, tt)
        s = w_ref[...]
        gelu = 0.5 * s * (1.0 + lax.erf(s * (2.0 ** -0.5)))
        w_ref[...] = gaten_ref[prev] * gelu
        code_n = codet_ref[...].T
        isel = code_n >> 7
        jsel = code_n & (PEER_NKEYS - 1)
        isel_ref[...] = isel
        jsel_ref[...] = jsel
        iseln_ref[cur] = isel
        jseln_ref[cur] = jsel
        gaten_ref[cur] = gatet_ref[...].T


def _peer_up(h2b, ut, qt, keys, tt, eb):
    t = h2b.shape[0]
    ne = ut.shape[1] // eb
    assert ne == N_KEYSETS
    ntile = t // tt
    half = PEER_DK // 2
    tok = lambda i, e: (jnp.maximum(i - 1, 0), 0)
    rt = lambda i, e: (jnp.minimum(i, ntile - 1), 0)
    vm = lambda shape, dt: pltpu.VMEM(shape, dt)
    return pl.pallas_call(
        functools.partial(_peer_up_kernel, ne=ne),
        grid=(ntile + 1, ne + 1),
        in_specs=[pl.BlockSpec((tt, D_MODEL), tok),
                  pl.BlockSpec((D_MODEL, eb), lambda i, e: (0, jnp.minimum(e, ne - 1))),
                  pl.BlockSpec((half, tt), lambda i, e: (jnp.minimum(e, ne - 1), jnp.minimum(i, ntile - 1))),
                  pl.BlockSpec((1, PEER_NKEYS, half), lambda i, e: (jnp.minimum(e, ne - 1), 0, 0))],
        out_specs=[pl.BlockSpec((tt, N_SEL), tok), pl.BlockSpec((tt, N_SEL), rt), pl.BlockSpec((tt, N_SEL), rt)],
        out_shape=[jax.ShapeDtypeStruct((t, N_SEL), F32), jax.ShapeDtypeStruct((t, N_SEL), I32),
                   jax.ShapeDtypeStruct((t, N_SEL), I32)],
        scratch_shapes=[vm((tt, eb), F32), vm((tt, eb), F32),
                        vm((PEER_TOPK, tt), F32), vm((PEER_TOPK, tt), I32),
                        vm((PEER_TOPK, tt), F32), vm((PEER_TOPK, tt), I32),
                        vm((PEER_TOPK, tt), F32), vm((PEER_TOPK, tt), I32),
                        vm((N_SEL, tt), I32), vm((N_SEL, tt), F32),
                        vm((2, tt, N_SEL), I32), vm((2, tt, N_SEL), I32), vm((2, tt, N_SEL), F32)],
        compiler_params=_cparams(("arbitrary", "arbitrary")),
        name="peer_up",
    )(h2b, ut, qt, keys)


def _peer_down_kernel(isel_ref, jsel_ref, w_ref, v_ref, h2_ref, lg_ref, lb_ref, o_ref, wd0_ref, wd1_ref, acc_ref,
                      *, ib, ne):
    i = pl.program_id(0)
    e = pl.program_id(1)
    tt = isel_ref.shape[0]
    per_step = tt // ne
    half = PEER_NKEYS // 2
    hi_mask = jnp.uint32(0xFFFF0000)

    def scatter(dst):
        sub = lax.broadcasted_iota(I32, (PEER_NKEYS, N_SEL), 0)
        for g0 in range(0, per_step, 8):
            base = pl.multiple_of(e * per_step + g0, 8)
            isb = isel_ref[pl.ds(base, 8), :]
            jsb = jsel_ref[pl.ds(base, 8), :]
            wb = w_ref[pl.ds(base, 8), :]
            for r in range(8):
                pt = jnp.where(sub == isb[r:r + 1, :], wb[r:r + 1, :], 0.0).astype(BF16)
                qt = jnp.where(sub == jsb[r:r + 1, :], 1.0, 0.0).astype(BF16)
                gb = pltpu.bitcast(_dot_nt(pt, qt), jnp.uint32)
                packed = (gb[half:] & hi_mask) | (gb[:half] >> 16)
                dst[pl.ds(pl.multiple_of((base + r) * W_PITCH, 8), half), :] = packed

    def down(src):
        parts = []
        for ii in range(ib):
            pk = src[pl.ds(e * ib + ii, tt, stride=W_PITCH), :]
            parts.append(pltpu.bitcast(pk << 16, F32).astype(BF16))
            parts.append(pltpu.bitcast(pk & hi_mask, F32).astype(BF16))
        acc_ref[...] += _dot(jnp.concatenate(parts, axis=1), v_ref[...])

    @pl.when((i == 0) & (e == 0))
    def _():
        wd1_ref[...] = jnp.zeros_like(wd1_ref)

    @pl.when(e == 0)
    def _():
        acc_ref[...] = jnp.zeros_like(acc_ref)

    @pl.when(i % 2 == 0)
    def _():
        scatter(wd0_ref)
        down(wd1_ref)

    @pl.when(i % 2 == 1)
    def _():
        scatter(wd1_ref)
        down(wd0_ref)

    @pl.when(e == ne - 1)
    def _():
        o_ref[...] = _layer_norm(ALPHA * h2_ref[...] + acc_ref[...], lg_ref[...], lb_ref[...])


def _peer_down(isel, jsel, w, vperm, h2, lg, lb, tt, ib):
    t = h2.shape[0]
    ne = (PEER_NKEYS // 2) // ib
    ntile = t // tt
    vb = ib * 2 * PEER_NKEYS
    assert tt % (8 * ne) == 0
    sel = pl.BlockSpec((tt, N_SEL), lambda i, e: (jnp.minimum(i, ntile - 1), 0))
    tok = pl.BlockSpec((tt, D_MODEL), lambda i, e: (jnp.maximum(i - 1, 0), 0))
    full = lambda a: pl.BlockSpec(a.shape, lambda i, e: (0,) * a.ndim)
    wd = pltpu.VMEM((tt * W_PITCH, N_SEL), jnp.uint32)
    return pl.pallas_call(
        functools.partial(_peer_down_kernel, ib=ib, ne=ne),
        grid=(ntile + 1, ne),
        in_specs=[sel, sel, sel, pl.BlockSpec((vb, D_MODEL), lambda i, e: (e, 0)), tok, full(lg), full(lb)],
        out_specs=tok,
        out_shape=jax.ShapeDtypeStruct((t, D_MODEL), F32),
        scratch_shapes=[wd, wd, pltpu.VMEM((tt, D_MODEL), F32)],
        compiler_params=_cparams(("arbitrary", "arbitrary")),
        name="peer_down",
    )(isel, jsel, w, vperm, h2, lg, lb)


def _rope_tables(seq):
    half = QK_ROPE_DIM // 2
    inv = 1.0 / (ROPE_THETA ** (jnp.arange(0, QK_ROPE_DIM, 2, dtype=F32) / QK_ROPE_DIM))
    ang = jnp.arange(seq, dtype=F32)[:, None] * inv[None, :]
    cos, sin = jnp.cos(ang), jnp.sin(ang)
    c32 = jnp.concatenate([cos, cos], axis=1)
    s32 = jnp.concatenate([-sin, sin], axis=1)
    z = lambda w: jnp.zeros((seq, w), F32)
    scale = (QK_NOPE_DIM + QK_ROPE_DIM) ** -0.5 * math.log2(math.e)
    pad = HEAD_SLOT - QK_NOPE_DIM - QK_ROPE_DIM
    cq = jnp.concatenate([jnp.full((seq, QK_NOPE_DIM), scale, F32), scale * c32, z(pad)], axis=1)
    sq = jnp.concatenate([z(QK_NOPE_DIM), scale * s32, z(pad)], axis=1)
    ck = jnp.concatenate([z(QK_NOPE_DIM), c32, z(pad)], axis=1)
    sk = jnp.concatenate([z(QK_NOPE_DIM), s32, z(pad)], axis=1)
    return cq, sq, ck, sk


def _swap_halves(w):
    half = w.shape[-1] // 2
    return jnp.concatenate([w[..., half:], w[..., :half]], axis=-1)


def _slot(w, offset):
    return jnp.pad(w, ((0, 0), (offset, HEAD_SLOT - offset - w.shape[1])))


def _position_features(seq):
    t = jnp.linspace(0.0, 1.0, seq, dtype=F32)[:, None]
    bands = (HY_EMB_DIM - 1) // 2
    w = 2.0 * math.pi * jnp.arange(seq, dtype=F32) / seq
    f = jnp.linspace(1e-4, bands - 1, bands, dtype=F32)
    ang = w[:, None] * f[None, :]
    return jnp.concatenate([t, jnp.cos(ang), -jnp.sin(ang)], axis=-1)


def kernel(x, emb_ln_g, emb_ln_b, w_in, q_norm_g, w_uq, kv_norm_g, w_ukv, hy_short_w, hy_short_b, hy_filt_w1,
           hy_filt_b1, hy_filt_freq1, hy_filt_w2, hy_filt_b2, hy_filt_freq2, hy_filt_w3, hy_bias, attn_out_g,
           hy_out_g, w_o, ln_mix_g, ln_mix_b, peer_wq, peer_sub_keys, peer_u, peer_v, ln_ffn_g, ln_ffn_b):
    batch, seq, _ = x.shape
    assert w_in.shape[0] == DEPTH == 1 and seq % TOEP == 0
    t = batch * seq
    nblk = seq // TOEP
    r2 = lambda a: a.reshape(1, -1)
    x2 = x.reshape(t, D_MODEL)
    lg, lb = r2(emb_ln_g), r2(emb_ln_b)

    wi = w_in[0]
    w_kr = wi[:, OFF_CKV:OFF_KR]
    wall = jnp.concatenate([wi[:, :OFF_CKV], _slot(w_kr, QK_NOPE_DIM), _slot(_swap_halves(w_kr), QK_NOPE_DIM),
                            wi[:, OFF_KR:]], axis=1).astype(BF16)
    dq = QK_NOPE_DIM + QK_ROPE_DIM
    wuq = w_uq[0].reshape(Q_LORA_RANK, ATTN_HEADS, dq)
    wq = jnp.pad(wuq, ((0, 0), (0, 0), (0, HEAD_SLOT - dq))).reshape(Q_LORA_RANK, -1).astype(BF16)
    wqs = jnp.pad(_swap_halves(wuq[..., QK_NOPE_DIM:]),
                  ((0, 0), (0, 0), (QK_NOPE_DIM, HEAD_SLOT - dq))).reshape(Q_LORA_RANK, -1).astype(BF16)
    wukv = w_ukv[0].reshape(KV_LORA_RANK, ATTN_HEADS, QK_NOPE_DIM + V_HEAD_DIM)
    wk = jnp.pad(wukv[..., :QK_NOPE_DIM],
                 ((0, 0), (0, 0), (0, HEAD_SLOT - QK_NOPE_DIM))).reshape(KV_LORA_RANK, -1).astype(BF16)
    wv = wukv[..., QK_NOPE_DIM:].reshape(KV_LORA_RANK, -1).T.astype(BF16)
    cq, sq, ck, sk = _rope_tables(seq)

    tt_proj = min(512, seq)
    q, k, vt, hy = _ln_proj(x2, lg, lb, wall, r2(q_norm_g[0]), wq, wqs, r2(kv_norm_g[0]), wk, wv,
                           cq, sq, ck, sk, seq, tt_proj)

    a = _attention(q, k, vt, attn_out_g[0].reshape(-1, 1), batch, seq, min(256, seq))

    ut = _short_conv(hy.reshape(batch, seq, -1), hy_short_w[0], r2(hy_short_b[0]))
    ut = ut.reshape((HY_ORDER + 1) * HY_WIDTH, nblk * batch, TOEP)

    lag = jnp.abs(jnp.arange(2 * seq) - seq)
    feats = _position_features(seq)
    zt = jnp.pad(feats[jnp.minimum(lag, seq - 1)].T, ((0, HY_FILTER_HIDDEN - HY_EMB_DIM), (0, 0)))
    col = lambda a_: a_.reshape(-1, 1)
    w1t = jnp.pad(hy_filt_w1[0].T, ((0, 0), (0, HY_FILTER_HIDDEN - HY_EMB_DIM)))
    w3t = hy_filt_w3[0].T.reshape(HY_ORDER, 2, HY_WIDTH, HY_FILTER_HIDDEN)
    deltas = jnp.abs(jnp.linspace(math.log(HY_TARGET) / HY_SLOW_DECAY, math.log(HY_TARGET) / HY_FAST_DECAY,
                                  HY_WIDTH, dtype=F32))
    g = _filters(zt, w1t, col(hy_filt_b1[0]), col(hy_filt_freq1[0]), hy_filt_w2[0].T, col(hy_filt_b2[0]),
                 col(hy_filt_freq2[0]), w3t, col(deltas))
    yt = _hyena(hy_bias[0], ut, g.reshape(HY_ORDER, HY_WIDTH, 1, 2 * seq), batch, nblk, 8)

    h2, h2b, qt = _mix(x2, lg, lb, a, yt.reshape(HY_WIDTH, nblk * batch * TOEP), r2(hy_out_g[0]),
                       w_o[0].astype(BF16), r2(ln_mix_g[0]), r2(ln_mix_b[0]), peer_wq[0].T.astype(BF16), batch, nblk)

    keys = peer_sub_keys[0].astype(BF16).reshape(N_KEYSETS, PEER_NKEYS, PEER_DK // 2)
    w, isel, jsel = _peer_up(h2b, peer_u[0].astype(BF16).T, qt, keys, min(1024, t), 1024)
    ib = 8
    half = PEER_NKEYS // 2
    vperm = peer_v[0].astype(BF16).reshape(2, half // ib, ib, PEER_NKEYS, D_MODEL)
    vperm = vperm.transpose(1, 2, 0, 3, 4).reshape(PEER_NKEYS * PEER_NKEYS, D_MODEL)
    out = _peer_down(isel, jsel, w, vperm, h2, r2(ln_ffn_g[0]), r2(ln_ffn_b[0]), min(256, t), ib)
    return out.reshape(batch, seq, D_MODEL)
```

```python
import functools
import math

import jax
import jax.numpy as jnp
from jax import lax
from jax.experimental import pallas as pl
from jax.experimental.pallas import tpu as pltpu

F32 = jnp.float32
BF16 = jnp.bfloat16
I32 = jnp.int32

D_MODEL = 1024
ATTN_HEADS = 8
QK_NOPE_DIM = 64
QK_ROPE_DIM = 32
V_HEAD_DIM = 64
Q_LORA_RANK = 256
KV_LORA_RANK = 128
ATTN_WIDTH = ATTN_HEADS * V_HEAD_DIM
ROPE_THETA = 10000.0
HY_WIDTH = D_MODEL - ATTN_WIDTH
HY_ORDER = 2
HY_GROUPS = 8
HY_SHORT = 3
HY_EMB_DIM = 33
HY_FILTER_HIDDEN = 64
HY_FAST_DECAY = 0.3
HY_SLOW_DECAY = 1.5
HY_TARGET = 1e-2
OFF_CQ = Q_LORA_RANK
OFF_CKV = OFF_CQ + KV_LORA_RANK
OFF_KR = OFF_CKV + QK_ROPE_DIM
PEER_HEADS = 8
PEER_NKEYS = 128
PEER_DK = 128
PEER_TOPK = 16
DEPTH = 1
ALPHA = (2 * DEPTH) ** 0.25
LN_EPS = 1e-5
RMS_EPS = 1e-6

LANES = 128
HEAD_SLOT = 128
TOEP = 256
N_SEL = PEER_HEADS * PEER_TOPK
W_PITCH = 72
GATHER_ROWS = 8
STEP_SLICES = 4
VMEM_LIMIT = 60 * 1024 * 1024


def _cparams(sem):
    return pltpu.CompilerParams(dimension_semantics=sem, vmem_limit_bytes=VMEM_LIMIT)


def _layer_norm(x, g, b):
    mu = jnp.mean(x, axis=-1, keepdims=True)
    xc = x - mu
    var = jnp.mean(xc * xc, axis=-1, keepdims=True)
    return xc * lax.rsqrt(var + LN_EPS) * g + b


def _rms(x, g):
    return x * lax.rsqrt(jnp.mean(x * x, axis=-1, keepdims=True) + RMS_EPS) * g


def _dot(a, b):
    return jnp.dot(a, b, preferred_element_type=F32)


def _dot_nt(a, b):
    return lax.dot_general(a, b, (((1,), (1,)), ((), ())), preferred_element_type=F32)


N_MLA_COLS = Q_LORA_RANK + KV_LORA_RANK + 2 * HEAD_SLOT


def _ln_proj_kernel(x_ref, lg_ref, lb_ref, wall_ref, gq_ref, wq_ref, wqs_ref, gkv_ref, wk_ref, wv_ref,
                    cq_ref, sq_ref, ck_ref, sk_ref, q_ref, k_ref, vt_ref, hy_ref):
    h = _layer_norm(x_ref[...], lg_ref[...], lb_ref[...])
    proj = _dot(h.astype(BF16), wall_ref[...])
    c_q = proj[:, :OFF_CQ]
    c_kv = proj[:, OFF_CQ:OFF_CKV]
    kr = proj[:, OFF_CKV:OFF_CKV + HEAD_SLOT]
    kr_sw = proj[:, OFF_CKV + HEAD_SLOT:N_MLA_COLS]
    hy_ref[...] = proj[:, N_MLA_COLS:].astype(BF16)

    nq = _rms(c_q, gq_ref[...]).astype(BF16)
    qa = _dot(nq, wq_ref[...])
    qb = _dot(nq, wqs_ref[...])
    nkv = _rms(c_kv, gkv_ref[...]).astype(BF16)
    kn = _dot(nkv, wk_ref[...])
    vt_ref[...] = _dot_nt(wv_ref[...], nkv).astype(BF16)
    k_pe = kr * ck_ref[...] + kr_sw * sk_ref[...]
    cq, sq = cq_ref[...], sq_ref[...]
    for hd in range(ATTN_HEADS):
        sl = slice(hd * HEAD_SLOT, (hd + 1) * HEAD_SLOT)
        q_ref[:, sl] = (qa[:, sl] * cq + qb[:, sl] * sq).astype(BF16)
        k_ref[:, sl] = (kn[:, sl] + k_pe).astype(BF16)


def _ln_proj(x2, lg, lb, wall, gq, wq, wqs, gkv, wk, wv, cq, sq, ck, sk, seq, tt):
    t = x2.shape[0]
    npos = seq // tt
    full = lambda a: pl.BlockSpec(a.shape, lambda i: (0,) * a.ndim)
    tab = pl.BlockSpec((tt, HEAD_SLOT), lambda i: (i % npos, 0))
    row = lambda w: pl.BlockSpec((tt, w), lambda i: (i, 0))
    hw = wall.shape[1] - N_MLA_COLS
    return pl.pallas_call(
        _ln_proj_kernel,
        grid=(t // tt,),
        in_specs=[row(D_MODEL), full(lg), full(lb), full(wall), full(gq), full(wq), full(wqs), full(gkv),
                  full(wk), full(wv), tab, tab, tab, tab],
        out_specs=[row(ATTN_HEADS * HEAD_SLOT), row(ATTN_HEADS * HEAD_SLOT),
                   pl.BlockSpec((ATTN_WIDTH, tt), lambda i: (0, i)), row(hw)],
        out_shape=[jax.ShapeDtypeStruct((t, ATTN_HEADS * HEAD_SLOT), BF16),
                   jax.ShapeDtypeStruct((t, ATTN_HEADS * HEAD_SLOT), BF16),
                   jax.ShapeDtypeStruct((ATTN_WIDTH, t), BF16),
                   jax.ShapeDtypeStruct((t, hw), BF16)],
        compiler_params=_cparams(("parallel",)),
        name="ln_proj",
    )(x2, lg, lb, wall, gq, wq, wqs, gkv, wk, wv, cq, sq, ck, sk)


def _attn_kernel(q_ref, k_ref, vt_ref, g_ref, o_ref, s0_ref, m0_ref, s1_ref, m1_ref, *, nsteps):
    n = pl.program_id(0)

    def scores(s_ref, m_ref):
        for hh in range(2):
            sl = slice(hh * HEAD_SLOT, (hh + 1) * HEAD_SLOT)
            s = _dot_nt(k_ref[:, sl], q_ref[:, sl])
            s_ref[hh] = s
            m_ref[hh] = jnp.max(s, axis=0, keepdims=True)

    def finish(s_ref, m_ref):
        outs = []
        ones = jnp.ones((16, vt_ref.shape[1]), BF16)
        for hh in range(2):
            vs = slice(hh * V_HEAD_DIM, (hh + 1) * V_HEAD_DIM)
            p = jnp.exp2(s_ref[hh] - m_ref[hh]).astype(BF16)
            ov = _dot(jnp.concatenate([vt_ref[vs, :], ones], axis=0), p)
            o = ov[:V_HEAD_DIM] / ov[V_HEAD_DIM:V_HEAD_DIM + 1]
            ms = jnp.mean(o * o, axis=0, keepdims=True)
            outs.append(o * lax.rsqrt(ms + RMS_EPS) * g_ref[vs, :])
        o_ref[...] = jnp.concatenate(outs, axis=0).T.astype(BF16)

    @pl.when(n == 0)
    def _():
        s1_ref[...] = jnp.zeros_like(s1_ref)
        m1_ref[...] = jnp.zeros_like(m1_ref)

    @pl.when((n < nsteps) & (n % 2 == 0))
    def _():
        finish(s1_ref, m1_ref)
        scores(s0_ref, m0_ref)

    @pl.when((n < nsteps) & (n % 2 == 1))
    def _():
        finish(s0_ref, m0_ref)
        scores(s1_ref, m1_ref)

    @pl.when(n == nsteps)
    def _():
        if nsteps % 2 == 0:
            finish(s1_ref, m1_ref)
        else:
            finish(s0_ref, m0_ref)


def _attention(q, k, vt, g, batch, seq, tq):
    t = q.shape[0]
    nq = seq // tq
    npair = ATTN_HEADS // 2
    nsteps = batch * npair * nq

    def tile(n):
        n = jnp.clip(n, 0, nsteps - 1)
        return n // (npair * nq), (n // nq) % npair, n % nq

    def q_map(n):
        b, p, i = tile(n)
        return b * nq + i, p

    def k_map(n):
        b, p, _ = tile(n)
        return b, p

    def vt_map(n):
        b, p, _ = tile(n - 1)
        return p, b

    def g_map(n):
        return tile(n - 1)[1], 0

    def o_map(n):
        b, p, i = tile(n - 1)
        return b * nq + i, p

    return pl.pallas_call(
        functools.partial(_attn_kernel, nsteps=nsteps),
        grid=(nsteps + 1,),
        in_specs=[pl.BlockSpec((tq, 2 * HEAD_SLOT), q_map),
                  pl.BlockSpec((seq, 2 * HEAD_SLOT), k_map),
                  pl.BlockSpec((2 * V_HEAD_DIM, seq), vt_map),
                  pl.BlockSpec((2 * V_HEAD_DIM, 1), g_map)],
        out_specs=pl.BlockSpec((tq, 2 * V_HEAD_DIM), o_map),
        out_shape=jax.ShapeDtypeStruct((t, ATTN_WIDTH), BF16),
        scratch_shapes=[pltpu.VMEM((2, seq, tq), F32), pltpu.VMEM((2, 1, tq), F32),
                        pltpu.VMEM((2, seq, tq), F32), pltpu.VMEM((2, 1, tq), F32)],
        compiler_params=_cparams(("arbitrary",)),
        name="attention",
    )(q, k, vt, g)


def _short_conv_kernel(prev_ref, x_ref, next_ref, w_ref, b_ref, o_ref):
    j = pl.program_id(1)
    nj = pl.num_programs(1)
    x = x_ref[0].astype(F32)
    rows = x.shape[0]
    before = jnp.where(j > 0, prev_ref[0, 7:8, :].astype(F32), 0.0)
    after = jnp.where(j < nj - 1, next_ref[0, 0:1, :].astype(F32), 0.0)
    rid = lax.broadcasted_iota(I32, x.shape, 0)
    xm = jnp.where(rid == 0, before, pltpu.roll(x, 1, 0))
    xp = jnp.where(rid == rows - 1, after, pltpu.roll(x, rows - 1, 0))
    u = b_ref[...] + xm * w_ref[0:1, :] + x * w_ref[1:2, :] + xp * w_ref[2:3, :]
    o_ref[...] = u.T.astype(BF16)


def _short_conv(hy3, w, b):
    batch, seq, width = hy3.shape
    nj = seq // TOEP
    sub = TOEP // 8
    last8 = seq // 8 - 1
    return pl.pallas_call(
        _short_conv_kernel,
        grid=(batch, nj),
        in_specs=[pl.BlockSpec((1, 8, width), lambda bb, j: (bb, jnp.maximum(j * sub - 1, 0), 0)),
                  pl.BlockSpec((1, TOEP, width), lambda bb, j: (bb, j, 0)),
                  pl.BlockSpec((1, 8, width), lambda bb, j: (bb, jnp.minimum((j + 1) * sub, last8), 0)),
                  pl.BlockSpec(w.shape, lambda bb, j: (0, 0)),
                  pl.BlockSpec(b.shape, lambda bb, j: (0, 0))],
        out_specs=pl.BlockSpec((width, TOEP), lambda bb, j: (0, j * batch + bb)),
        out_shape=jax.ShapeDtypeStruct((width, nj * batch * TOEP), BF16),
        compiler_params=_cparams(("parallel", "parallel")),
        name="short_conv",
    )(hy3, hy3, hy3, w, b)


def _filter_kernel(z_ref, w1_ref, b1_ref, f1_ref, w2_ref, b2_ref, f2_ref, w3_ref, dl_ref, g_ref):
    hp = lax.Precision.HIGHEST
    z = z_ref[...]
    n = z.shape[1]
    half = n // 2
    h1 = jnp.sin(f1_ref[...] * (jnp.dot(w1_ref[...], z, precision=hp, preferred_element_type=F32) + b1_ref[...]))
    h2 = jnp.sin(f2_ref[...] * (jnp.dot(w2_ref[...], h1, precision=hp, preferred_element_type=F32) + b2_ref[...]))
    decay = jnp.exp(-z[0:1, :] * dl_ref[...])
    lane = lax.broadcasted_iota(I32, (1, n), 1)
    masks = (lane >= half, (lane >= 1) & (lane <= half))
    out = None
    for d in range(2):
        hd = jnp.dot(w3_ref[0, d], h2, precision=hp, preferred_element_type=F32) * decay
        hd = jnp.where(masks[d], hd, 0.0)
        hd = hd / (jnp.sum(jnp.abs(hd), axis=1, keepdims=True) + 1e-6)
        out = hd if out is None else out + hd
    g_ref[0] = out


def _filters(zt, w1t, b1, f1, w2t, b2, f2, w3t, dl):
    n = zt.shape[1]
    cb = 128
    full = lambda a: pl.BlockSpec(a.shape, lambda o, c: (0,) * a.ndim)
    return pl.pallas_call(
        _filter_kernel,
        grid=(HY_ORDER, HY_WIDTH // cb),
        in_specs=[full(zt), full(w1t), full(b1), full(f1), full(w2t), full(b2), full(f2),
                  pl.BlockSpec((1, 2, cb, HY_FILTER_HIDDEN), lambda o, c: (o, 0, c, 0)),
                  pl.BlockSpec((cb, 1), lambda o, c: (c, 0))],
        out_specs=pl.BlockSpec((1, cb, n), lambda o, c: (o, c, 0)),
        out_shape=jax.ShapeDtypeStruct((HY_ORDER, HY_WIDTH, n), F32),
        compiler_params=_cparams(("parallel", "parallel")),
        name="filters",
    )(zt, w1t, b1, f1, w2t, b2, f2, w3t, dl)


def _hyena_kernel(bias_ref, v_ref, x1_ref, x2_ref, g_ref, o_ref, acc_ref, *, batch, nblk, cb):
    rows = batch * nblk
    seq = nblk * TOEP
    c0 = pl.program_id(0) * cb
    gates = (x1_ref, x2_ref)

    def per_channel(c, carry):
        zf = v_ref[c].astype(F32)
        for o in range(HY_ORDER):
            zb = zf.astype(BF16)
            acc_ref[...] = jnp.zeros_like(acc_ref)
            for d in range(-(nblk - 1), nblk):
                base = seq + TOEP * d - TOEP
                win = g_ref[o, c, :, pl.ds(base, 2 * TOEP)]
                w = pltpu.roll(jnp.broadcast_to(win, (TOEP, 2 * TOEP)), TOEP, 1, stride=1, stride_axis=0)
                w = w[:, :TOEP].astype(BF16)
                n = rows - batch * abs(d)
                if d >= 0:
                    acc_ref[batch * d:, :] += _dot(zb[:n], w)
                else:
                    acc_ref[:n, :] += _dot(zb[batch * (-d):], w)
            zf = gates[o][c].astype(F32) * (acc_ref[...] + bias_ref[o, c0 + c] * zf)
        o_ref[c] = zf.astype(BF16)
        return carry

    lax.fori_loop(0, cb, per_channel, 0)


def _hyena(bias, ut, g4, batch, nblk, cb):
    rows = batch * nblk
    nc = HY_WIDTH // cb
    kern = functools.partial(_hyena_kernel, batch=batch, nblk=nblk, cb=cb)
    blk = lambda off: pl.BlockSpec((cb, rows, TOEP), lambda i: (off * nc + i, 0, 0))
    return pl.pallas_call(
        kern,
        grid=(nc,),
        in_specs=[pl.BlockSpec(memory_space=pltpu.SMEM), blk(0), blk(1), blk(2),
                  pl.BlockSpec((HY_ORDER, cb, 1, g4.shape[3]), lambda i: (0, i, 0, 0))],
        out_specs=pl.BlockSpec((cb, rows, TOEP), lambda i: (i, 0, 0)),
        out_shape=jax.ShapeDtypeStruct((HY_WIDTH, rows, TOEP), BF16),
        scratch_shapes=[pltpu.VMEM((rows, TOEP), F32)],
        compiler_params=_cparams(("parallel",)),
        name="hyena",
    )(bias, ut, ut, ut, g4)


def _mix_kernel(x_ref, lg_ref, lb_ref, a_ref, yt_ref, hg_ref, wo_ref, mg_ref, mb_ref, wq_ref,
                h2_ref, h2b_ref, qt_ref, cat_ref):
    y = yt_ref[...].astype(F32).T
    cat_ref[:, :ATTN_WIDTH] = a_ref[...]
    gw = HY_WIDTH // HY_GROUPS
    for g in range(HY_GROUPS):
        sl = slice(g * gw, (g + 1) * gw)
        cat_ref[:, ATTN_WIDTH + g * gw:ATTN_WIDTH + (g + 1) * gw] = _rms(y[:, sl], hg_ref[:, sl]).astype(BF16)
    mix = _dot(cat_ref[...], wo_ref[...])
    h = _layer_norm(x_ref[...], lg_ref[...], lb_ref[...])
    h2 = _layer_norm(ALPHA * h + mix, mg_ref[...], mb_ref[...])
    h2_ref[...] = h2
    h2b = h2.astype(BF16)
    h2b_ref[...] = h2b
    qt_ref[...] = _dot_nt(wq_ref[...], h2b).astype(BF16)


def _mix(x2, lg, lb, a, yt2, hg, wo, mg, mb, wqt, batch, nblk):
    t = x2.shape[0]
    full = lambda w: pl.BlockSpec(w.shape, lambda i: (0,) * w.ndim)
    row = lambda w: pl.BlockSpec((TOEP, w), lambda i: (i, 0))
    return pl.pallas_call(
        _mix_kernel,
        grid=(t // TOEP,),
        in_specs=[row(D_MODEL), full(lg), full(lb), row(ATTN_WIDTH),
                  pl.BlockSpec((HY_WIDTH, TOEP), lambda i: (0, (i % nblk) * batch + i // nblk)),
                  full(hg), full(wo), full(mg), full(mb), full(wqt)],
        out_specs=[row(D_MODEL), row(D_MODEL), pl.BlockSpec((wqt.shape[0], TOEP), lambda i: (0, i))],
        out_shape=[jax.ShapeDtypeStruct((t, D_MODEL), F32), jax.ShapeDtypeStruct((t, D_MODEL), BF16),
                   jax.ShapeDtypeStruct((wqt.shape[0], t), BF16)],
        scratch_shapes=[pltpu.VMEM((TOEP, D_MODEL), BF16)],
        compiler_params=_cparams(("parallel",)),
        name="mix",
    )(x2, lg, lb, a, yt2, hg, wo, mg, mb, wqt)


N_KEYSETS = 2 * PEER_HEADS


SORT_GROUP = 4


def _top16(s, val_ref, idx_ref, lanes):
    neg = -jnp.inf
    sub = lax.broadcasted_iota(I32, (8, s.shape[1]), 0)
    ntile = s.shape[0] // 8
    groups = []
    for g0 in range(0, ntile, SORT_GROUP):
        t = [s[8 * v:8 * v + 8, :] for v in range(g0, g0 + SORT_GROUP)]
        ix = [sub + 8 * v for v in range(g0, g0 + SORT_GROUP)]
        for end in range(SORT_GROUP - 1, 0, -1):
            for p in range(end):
                swap = t[p + 1] > t[p]
                t[p], t[p + 1] = jnp.where(swap, t[p + 1], t[p]), jnp.where(swap, t[p], t[p + 1])
                ix[p], ix[p + 1] = jnp.where(swap, ix[p + 1], ix[p]), jnp.where(swap, ix[p], ix[p + 1])
        groups.append((t, ix))
    for r in range(PEER_TOPK):
        heads = [t[0] for t, _ in groups]
        while len(heads) > 1:
            heads = [jnp.maximum(heads[k], heads[k + 1]) for k in range(0, len(heads), 2)]
        m = jnp.max(heads[0], axis=0, keepdims=True)
        cand = [jnp.where(t[0] == m, ix[0], PEER_NKEYS) for t, ix in groups]
        while len(cand) > 1:
            cand = [jnp.minimum(cand[k], cand[k + 1]) for k in range(0, len(cand), 2)]
        idx = jnp.min(cand[0], axis=0, keepdims=True)
        val_ref[r:r + 1, lanes] = m
        idx_ref[r:r + 1, lanes] = idx
        for t, ix in groups:
            hit = ix[0] == idx
            for p in range(SORT_GROUP - 1):
                t[p] = jnp.where(hit, t[p + 1], t[p])
                ix[p] = jnp.where(hit, ix[p + 1], ix[p])
            t[-1] = jnp.where(hit, neg, t[-1])


_CODE_BITS = 14


def _pair_top16(v1_ref, i1_ref, v2_ref, i2_ref, best_ref, code_ref, lanes):
    neg = -jnp.inf
    width = v1_ref[0:8, lanes].shape[1]
    sub = lax.broadcasted_iota(I32, (8, width), 0)
    bc = lambda ref, r: jnp.broadcast_to(ref[r:r + 1, lanes], (8, width))

    def tiles(a_ref, b_ref, combine):
        b_lo = b_ref[0:8, lanes]
        a_hi = pltpu.roll(a_ref[8:16, lanes], 2, 0)
        out = [combine(bc(a_ref, 0), b_lo), combine(bc(a_ref, 0), b_ref[8:16, lanes]), combine(bc(a_ref, 1), b_lo)]
        out.append(jnp.where(sub < 5, combine(bc(a_ref, 2), b_lo), combine(bc(a_ref, 4), pltpu.roll(b_lo, 5, 0))))
        out.append(jnp.where(sub < 4, combine(bc(a_ref, 3), b_lo),
                             jnp.where(sub < 6, combine(bc(a_ref, 5), pltpu.roll(b_lo, 4, 0)),
                                       combine(bc(a_ref, 6), pltpu.roll(b_lo, 6, 0)))))
        out.append(jnp.where(sub < 2, combine(bc(a_ref, 7), b_lo), combine(a_hi, bc(b_ref, 0))))
        out.append(combine(a_hi, bc(b_ref, 0)))
        return out

    pos = [sub, sub + 8, sub + 16,
           jnp.where(sub < 5, sub + 32, sub + (64 - 5)),
           jnp.where(sub < 4, sub + 48, jnp.where(sub < 6, sub + (80 - 4), sub + (96 - 6))),
           jnp.where(sub < 2, sub + 112, (sub + 6) * 16),
           (sub + 14) * 16]
    cand = tiles(v1_ref, v2_ref, lambda a, b: a + b)
    cand[-1] = jnp.where(sub < 2, cand[-1], neg)
    code = tiles(i1_ref, i2_ref, lambda a, b: a * PEER_NKEYS + b)
    key = [p * (1 << _CODE_BITS) + c for p, c in zip(pos, code)]
    big = jnp.int32(1 << 30)

    def tree(xs, op):
        xs = list(xs)
        while len(xs) > 1:
            xs = [op(xs[k], xs[k + 1]) for k in range(0, len(xs) - 1, 2)] + ([xs[-1]] if len(xs) % 2 else [])
        return xs[0]

    for kk in range(PEER_TOPK):
        m = jnp.max(tree(cand, jnp.maximum), axis=0, keepdims=True)
        kmin = jnp.min(tree([jnp.where(c == m, k, big) for c, k in zip(cand, key)], jnp.minimum), axis=0, keepdims=True)
        best_ref[kk:kk + 1, lanes] = m
        code_ref[kk:kk + 1, lanes] = kmin & ((1 << _CODE_BITS) - 1)
        cand = [jnp.where(k == kmin, neg, c) for c, k in zip(cand, key)]


def _peer_up_kernel(x_ref, u_ref, q_ref, keys_ref, w_ref, isel_ref, jsel_ref,
                    a0_ref, a1_ref, v1_ref, i1_ref, v2_ref, i2_ref, best_ref, codeh_ref, codet_ref, gatet_ref,
                    iseln_ref, jseln_ref, gaten_ref, *, ne):
    i = pl.program_id(0)
    e = pl.program_id(1)
    tt = x_ref.shape[0]
    nsub = u_ref.shape[1] // PEER_NKEYS
    prev = (i + 1) % 2
    cur = i % 2
    chunks = [slice(c, c + LANES) for c in range(0, tt, LANES)]

    def pick(e_src, src, r_lo, r_hi):
        isel_p, jsel_p = iseln_ref.at[prev], jseln_ref.at[prev]
        for r0 in range(r_lo, r_hi, GATHER_ROWS):
            rows = slice(r0, r0 + GATHER_ROWS)
            isel = isel_p[rows, :]
            jsel = jsel_p[rows, :]
            acc = w_ref[rows, :]
            for ii in range(nsub):
                got = jnp.take_along_axis(src[rows, ii * PEER_NKEYS:(ii + 1) * PEER_NKEYS], jsel, axis=1)
                acc = jnp.where(isel == e_src * nsub + ii, got, acc)
            w_ref[rows, :] = acc
        return acc

    def step(dst, src, val_ref, idx_ref, pair):
        assert len(chunks) == 2 * STEP_SLICES
        s = _dot(keys_ref[0], q_ref[...])
        eb = u_ref.shape[1]
        never = e < 0
        bounds = [0] + [2 * c + 1 for c in range(STEP_SLICES)] + [len(chunks)]
        deps = ()
        for c in range(STEP_SLICES + 1):
            nxt = deps
            if c < STEP_SLICES:
                cols = slice(c * eb // STEP_SLICES, (c + 1) * eb // STEP_SLICES)
                res = _dot(x_ref[...], u_ref[:, cols])
                dst[:, cols] = res
                acc = pick(e - 1, src, c * tt // STEP_SLICES, (c + 1) * tt // STEP_SLICES)
                nxt = (res[:PEER_NKEYS, :LANES], jnp.broadcast_to(acc[0:1, :], (PEER_NKEYS, LANES)))
            for lanes in chunks[bounds[c]:bounds[c + 1]]:
                sc = s[:, lanes]
                for dep in deps:
                    sc = jnp.where(never, dep, sc)
                _top16(sc, val_ref, idx_ref, lanes)
                if pair:
                    _pair_top16(v1_ref, i1_ref, v2_ref, i2_ref, best_ref, codeh_ref, lanes)
            deps = nxt

    @pl.when((i == 0) & (e == 0))
    def _():
        a1_ref[...] = jnp.zeros_like(a1_ref)
        iseln_ref[...] = jnp.zeros_like(iseln_ref)
        jseln_ref[...] = jnp.zeros_like(jseln_ref)
        gaten_ref[...] = jnp.zeros_like(gaten_ref)

    @pl.when(e == 0)
    def _():
        w_ref[...] = jnp.zeros_like(w_ref)

    @pl.when((e < ne) & (e % 2 == 0))
    def _():
        step(a0_ref, a1_ref, v1_ref, i1_ref, False)

    @pl.when((e < ne) & (e % 2 == 1))
    def _():
        step(a1_ref, a0_ref, v2_ref, i2_ref, True)
        best = best_ref[...]
        ex = jnp.exp(best - best[0:1, :])
        row0 = pl.multiple_of((e // 2) * PEER_TOPK, PEER_TOPK)
        gatet_ref[pl.ds(row0, PEER_TOPK), :] = ex / jnp.sum(ex, axis=0, keepdims=True)
        codet_ref[pl.ds(row0, PEER_TOPK), :] = codeh_ref[...]

    @pl.when(e == ne)
    def _():
        pick(e - 1, a1_ref if ne % 2 == 0 else a0_ref, 0, tt)
        s = w_ref[...]
        gelu = 0.5 * s * (1.0 + lax.erf(s * (2.0 ** -0.5)))
        w_ref[...] = gaten_ref[prev] * gelu
        code_n = codet_ref[...].T
        isel = code_n >> 7
        jsel = code_n & (PEER_NKEYS - 1)
        isel_ref[...] = isel
        jsel_ref[...] = jsel
        iseln_ref[cur] = isel
        jseln_ref[cur] = jsel
        gaten_ref[cur] = gatet_ref[...].T


def _peer_up(h2b, ut, qt, keys, tt, eb):
    t = h2b.shape[0]
    ne = ut.shape[1] // eb
    assert ne == N_KEYSETS
    ntile = t // tt
    half = PEER_DK // 2
    tok = lambda i, e: (jnp.maximum(i - 1, 0), 0)
    rt = lambda i, e: (jnp.minimum(i, ntile - 1), 0)
    vm = lambda shape, dt: pltpu.VMEM(shape, dt)
    return pl.pallas_call(
        functools.partial(_peer_up_kernel, ne=ne),
        grid=(ntile + 1, ne + 1),
        in_specs=[pl.BlockSpec((tt, D_MODEL), tok),
                  pl.BlockSpec((D_MODEL, eb), lambda i, e: (0, jnp.minimum(e, ne - 1))),
                  pl.BlockSpec((half, tt), lambda i, e: (jnp.minimum(e, ne - 1), jnp.minimum(i, ntile - 1))),
                  pl.BlockSpec((1, PEER_NKEYS, half), lambda i, e: (jnp.minimum(e, ne - 1), 0, 0))],
        out_specs=[pl.BlockSpec((tt, N_SEL), tok), pl.BlockSpec((tt, N_SEL), rt), pl.BlockSpec((tt, N_SEL), rt)],
        out_shape=[jax.ShapeDtypeStruct((t, N_SEL), F32), jax.ShapeDtypeStruct((t, N_SEL), I32),
                   jax.ShapeDtypeStruct((t, N_SEL), I32)],
        scratch_shapes=[vm((tt, eb), F32), vm((tt, eb), F32),
                        vm((PEER_TOPK, tt), F32), vm((PEER_TOPK, tt), I32),
                        vm((PEER_TOPK, tt), F32), vm((PEER_TOPK, tt), I32),
                        vm((PEER_TOPK, tt), F32), vm((PEER_TOPK, tt), I32),
                        vm((N_SEL, tt), I32), vm((N_SEL, tt), F32),
                        vm((2, tt, N_SEL), I32), vm((2, tt, N_SEL), I32), vm((2, tt, N_SEL), F32)],
        compiler_params=_cparams(("arbitrary", "arbitrary")),
        name="peer_up",
    )(h2b, ut, qt, keys)


def _peer_down_kernel(isel_ref, jsel_ref, w_ref, v_ref, h2_ref, lg_ref, lb_ref, o_ref, wd0_ref, wd1_ref, acc_ref,
                      *, ib, ne):
    i = pl.program_id(0)
    e = pl.program_id(1)
    tt = isel_ref.shape[0]
    per_step = tt // ne
    half = PEER_NKEYS // 2
    hi_mask = jnp.uint32(0xFFFF0000)

    def scatter(dst):
        sub = lax.broadcasted_iota(I32, (PEER_NKEYS, N_SEL), 0)
        for g0 in range(0, per_step, 8):
            base = pl.multiple_of(e * per_step + g0, 8)
            isb = isel_ref[pl.ds(base, 8), :]
            jsb = jsel_ref[pl.ds(base, 8), :]
            wb = w_ref[pl.ds(base, 8), :]
            for r in range(8):
                pt = jnp.where(sub == isb[r:r + 1, :], wb[r:r + 1, :], 0.0).astype(BF16)
                qt = jnp.where(sub == jsb[r:r + 1, :], 1.0, 0.0).astype(BF16)
                gb = pltpu.bitcast(_dot_nt(pt, qt), jnp.uint32)
                packed = (gb[half:] & hi_mask) | (gb[:half] >> 16)
                dst[pl.ds(pl.multiple_of((base + r) * W_PITCH, 8), half), :] = packed

    def down(src):
        parts = []
        for ii in range(ib):
            pk = src[pl.ds(e * ib + ii, tt, stride=W_PITCH), :]
            parts.append(pltpu.bitcast(pk << 16, F32).astype(BF16))
            parts.append(pltpu.bitcast(pk & hi_mask, F32).astype(BF16))
        acc_ref[...] += _dot(jnp.concatenate(parts, axis=1), v_ref[...])

    @pl.when((i == 0) & (e == 0))
    def _():
        wd1_ref[...] = jnp.zeros_like(wd1_ref)

    @pl.when(e == 0)
    def _():
        acc_ref[...] = jnp.zeros_like(acc_ref)

    @pl.when(i % 2 == 0)
    def _():
        scatter(wd0_ref)
        down(wd1_ref)

    @pl.when(i % 2 == 1)
    def _():
        scatter(wd1_ref)
        down(wd0_ref)

    @pl.when(e == ne - 1)
    def _():
        o_ref[...] = _layer_norm(ALPHA * h2_ref[...] + acc_ref[...], lg_ref[...], lb_ref[...])


def _peer_down(isel, jsel, w, vperm, h2, lg, lb, tt, ib):
    t = h2.shape[0]
    ne = (PEER_NKEYS // 2) // ib
    ntile = t // tt
    vb = ib * 2 * PEER_NKEYS
    assert tt % (8 * ne) == 0
    sel = pl.BlockSpec((tt, N_SEL), lambda i, e: (jnp.minimum(i, ntile - 1), 0))
    tok = pl.BlockSpec((tt, D_MODEL), lambda i, e: (jnp.maximum(i - 1, 0), 0))
    full = lambda a: pl.BlockSpec(a.shape, lambda i, e: (0,) * a.ndim)
    wd = pltpu.VMEM((tt * W_PITCH, N_SEL), jnp.uint32)
    return pl.pallas_call(
        functools.partial(_peer_down_kernel, ib=ib, ne=ne),
        grid=(ntile + 1, ne),
        in_specs=[sel, sel, sel, pl.BlockSpec((vb, D_MODEL), lambda i, e: (e, 0)), tok, full(lg), full(lb)],
        out_specs=tok,
        out_shape=jax.ShapeDtypeStruct((t, D_MODEL), F32),
        scratch_shapes=[wd, wd, pltpu.VMEM((tt, D_MODEL), F32)],
        compiler_params=_cparams(("arbitrary", "arbitrary")),
        name="peer_down",
    )(isel, jsel, w, vperm, h2, lg, lb)


def _rope_tables(seq):
    half = QK_ROPE_DIM // 2
    inv = 1.0 / (ROPE_THETA ** (jnp.arange(0, QK_ROPE_DIM, 2, dtype=F32) / QK_ROPE_DIM))
    ang = jnp.arange(seq, dtype=F32)[:, None] * inv[None, :]
    cos, sin = jnp.cos(ang), jnp.sin(ang)
    c32 = jnp.concatenate([cos, cos], axis=1)
    s32 = jnp.concatenate([-sin, sin], axis=1)
    z = lambda w: jnp.zeros((seq, w), F32)
    scale = (QK_NOPE_DIM + QK_ROPE_DIM) ** -0.5 * math.log2(math.e)
    pad = HEAD_SLOT - QK_NOPE_DIM - QK_ROPE_DIM
    cq = jnp.concatenate([jnp.full((seq, QK_NOPE_DIM), scale, F32), scale * c32, z(pad)], axis=1)
    sq = jnp.concatenate([z(QK_NOPE_DIM), scale * s32, z(pad)], axis=1)
    ck = jnp.concatenate([z(QK_NOPE_DIM), c32, z(pad)], axis=1)
    sk = jnp.concatenate([z(QK_NOPE_DIM), s32, z(pad)], axis=1)
    return cq, sq, ck, sk


def _swap_halves(w):
    half = w.shape[-1] // 2
    return jnp.concatenate([w[..., half:], w[..., :half]], axis=-1)


def _slot(w, offset):
    return jnp.pad(w, ((0, 0), (offset, HEAD_SLOT - offset - w.shape[1])))


def _position_features(seq):
    t = jnp.linspace(0.0, 1.0, seq, dtype=F32)[:, None]
    bands = (HY_EMB_DIM - 1) // 2
    w = 2.0 * math.pi * jnp.arange(seq, dtype=F32) / seq
    f = jnp.linspace(1e-4, bands - 1, bands, dtype=F32)
    ang = w[:, None] * f[None, :]
    return jnp.concatenate([t, jnp.cos(ang), -jnp.sin(ang)], axis=-1)


def kernel(x, emb_ln_g, emb_ln_b, w_in, q_norm_g, w_uq, kv_norm_g, w_ukv, hy_short_w, hy_short_b, hy_filt_w1,
           hy_filt_b1, hy_filt_freq1, hy_filt_w2, hy_filt_b2, hy_filt_freq2, hy_filt_w3, hy_bias, attn_out_g,
           hy_out_g, w_o, ln_mix_g, ln_mix_b, peer_wq, peer_sub_keys, peer_u, peer_v, ln_ffn_g, ln_ffn_b):
    batch, seq, _ = x.shape
    assert w_in.shape[0] == DEPTH == 1 and seq % TOEP == 0
    t = batch * seq
    nblk = seq // TOEP
    r2 = lambda a: a.reshape(1, -1)
    x2 = x.reshape(t, D_MODEL)
    lg, lb = r2(emb_ln_g), r2(emb_ln_b)

    wi = w_in[0]
    w_kr = wi[:, OFF_CKV:OFF_KR]
    wall = jnp.concatenate([wi[:, :OFF_CKV], _slot(w_kr, QK_NOPE_DIM), _slot(_swap_halves(w_kr), QK_NOPE_DIM),
                            wi[:, OFF_KR:]], axis=1).astype(BF16)
    dq = QK_NOPE_DIM + QK_ROPE_DIM
    wuq = w_uq[0].reshape(Q_LORA_RANK, ATTN_HEADS, dq)
    wq = jnp.pad(wuq, ((0, 0), (0, 0), (0, HEAD_SLOT - dq))).reshape(Q_LORA_RANK, -1).astype(BF16)
    wqs = jnp.pad(_swap_halves(wuq[..., QK_NOPE_DIM:]),
                  ((0, 0), (0, 0), (QK_NOPE_DIM, HEAD_SLOT - dq))).reshape(Q_LORA_RANK, -1).astype(BF16)
    wukv = w_ukv[0].reshape(KV_LORA_RANK, ATTN_HEADS, QK_NOPE_DIM + V_HEAD_DIM)
    wk = jnp.pad(wukv[..., :QK_NOPE_DIM],
                 ((0, 0), (0, 0), (0, HEAD_SLOT - QK_NOPE_DIM))).reshape(KV_LORA_RANK, -1).astype(BF16)
    wv = wukv[..., QK_NOPE_DIM:].reshape(KV_LORA_RANK, -1).T.astype(BF16)
    cq, sq, ck, sk = _rope_tables(seq)

    tt_proj = min(512, seq)
    q, k, vt, hy = _ln_proj(x2, lg, lb, wall, r2(q_norm_g[0]), wq, wqs, r2(kv_norm_g[0]), wk, wv,
                           cq, sq, ck, sk, seq, tt_proj)

    a = _attention(q, k, vt, attn_out_g[0].reshape(-1, 1), batch, seq, min(256, seq))

    ut = _short_conv(hy.reshape(batch, seq, -1), hy_short_w[0], r2(hy_short_b[0]))
    ut = ut.reshape((HY_ORDER + 1) * HY_WIDTH, nblk * batch, TOEP)

    lag = jnp.abs(jnp.arange(2 * seq) - seq)
    feats = _position_features(seq)
    zt = jnp.pad(feats[jnp.minimum(lag, seq - 1)].T, ((0, HY_FILTER_HIDDEN - HY_EMB_DIM), (0, 0)))
    col = lambda a_: a_.reshape(-1, 1)
    w1t = jnp.pad(hy_filt_w1[0].T, ((0, 0), (0, HY_FILTER_HIDDEN - HY_EMB_DIM)))
    w3t = hy_filt_w3[0].T.reshape(HY_ORDER, 2, HY_WIDTH, HY_FILTER_HIDDEN)
    deltas = jnp.abs(jnp.linspace(math.log(HY_TARGET) / HY_SLOW_DECAY, math.log(HY_TARGET) / HY_FAST_DECAY,
                                  HY_WIDTH, dtype=F32))
    g = _filters(zt, w1t, col(hy_filt_b1[0]), col(hy_filt_freq1[0]), hy_filt_w2[0].T, col(hy_filt_b2[0]),
                 col(hy_filt_freq2[0]), w3t, col(deltas))
    yt = _hyena(hy_bias[0], ut, g.reshape(HY_ORDER, HY_WIDTH, 1, 2 * seq), batch, nblk, 8)

    h2, h2b, qt = _mix(x2, lg, lb, a, yt.reshape(HY_WIDTH, nblk * batch * TOEP), r2(hy_out_g[0]),
                       w_o[0].astype(BF16), r2(ln_mix_g[0]), r2(ln_mix_b[0]), peer_wq[0].T.astype(BF16), batch, nblk)

    keys = peer_sub_keys[0].astype(BF16).reshape(N_KEYSETS, PEER_NKEYS, PEER_DK // 2)
    w, isel, jsel = _peer_up(h2b, peer_u[0].astype(BF16).T, qt, keys, min(1024, t), 1024)
    ib = 4
    half = PEER_NKEYS // 2
    vperm = peer_v[0].astype(BF16).reshape(2, half // ib, ib, PEER_NKEYS, D_MODEL)
    vperm = vperm.transpose(1, 2, 0, 3, 4).reshape(PEER_NKEYS * PEER_NKEYS, D_MODEL)
    out = _peer_down(isel, jsel, w, vperm, h2, r2(ln_ffn_g[0]), r2(ln_ffn_b[0]), min(512, t), ib)
    return out.reshape(batch, seq, D_MODEL)
```

```python
import functools
import math

import jax
import jax.numpy as jnp
from jax import lax
from jax.experimental import pallas as pl
from jax.experimental.pallas import tpu as pltpu

F32 = jnp.float32
BF16 = jnp.bfloat16
I32 = jnp.int32

D_MODEL = 1024
ATTN_HEADS = 8
QK_NOPE_DIM = 64
QK_ROPE_DIM = 32
V_HEAD_DIM = 64
Q_LORA_RANK = 256
KV_LORA_RANK = 128
ATTN_WIDTH = ATTN_HEADS * V_HEAD_DIM
ROPE_THETA = 10000.0
HY_WIDTH = D_MODEL - ATTN_WIDTH
HY_ORDER = 2
HY_GROUPS = 8
HY_SHORT = 3
HY_EMB_DIM = 33
HY_FILTER_HIDDEN = 64
HY_FAST_DECAY = 0.3
HY_SLOW_DECAY = 1.5
HY_TARGET = 1e-2
OFF_CQ = Q_LORA_RANK
OFF_CKV = OFF_CQ + KV_LORA_RANK
OFF_KR = OFF_CKV + QK_ROPE_DIM
PEER_HEADS = 8
PEER_NKEYS = 128
PEER_DK = 128
PEER_TOPK = 16
DEPTH = 1
ALPHA = (2 * DEPTH) ** 0.25
LN_EPS = 1e-5
RMS_EPS = 1e-6

LANES = 128
HEAD_SLOT = 128
TOEP = 256
N_SEL = PEER_HEADS * PEER_TOPK
W_PITCH = 72
GATHER_ROWS = 8
STEP_SLICES = 4
VMEM_LIMIT = 60 * 1024 * 1024


def _cparams(sem):
    return pltpu.CompilerParams(dimension_semantics=sem, vmem_limit_bytes=VMEM_LIMIT)


def _layer_norm(x, g, b):
    mu = jnp.mean(x, axis=-1, keepdims=True)
    xc = x - mu
    var = jnp.mean(xc * xc, axis=-1, keepdims=True)
    return xc * lax.rsqrt(var + LN_EPS) * g + b


def _rms(x, g):
    return x * lax.rsqrt(jnp.mean(x * x, axis=-1, keepdims=True) + RMS_EPS) * g


def _dot(a, b):
    return jnp.dot(a, b, preferred_element_type=F32)


def _dot_nt(a, b):
    return lax.dot_general(a, b, (((1,), (1,)), ((), ())), preferred_element_type=F32)


N_MLA_COLS = Q_LORA_RANK + KV_LORA_RANK + 2 * HEAD_SLOT


def _ln_proj_kernel(x_ref, lg_ref, lb_ref, wall_ref, gq_ref, wq_ref, wqs_ref, gkv_ref, wk_ref, wv_ref,
                    cq_ref, sq_ref, ck_ref, sk_ref, q_ref, k_ref, vt_ref, hy_ref):
    h = _layer_norm(x_ref[...], lg_ref[...], lb_ref[...])
    proj = _dot(h.astype(BF16), wall_ref[...])
    c_q = proj[:, :OFF_CQ]
    c_kv = proj[:, OFF_CQ:OFF_CKV]
    kr = proj[:, OFF_CKV:OFF_CKV + HEAD_SLOT]
    kr_sw = proj[:, OFF_CKV + HEAD_SLOT:N_MLA_COLS]
    hy_ref[...] = proj[:, N_MLA_COLS:].astype(BF16)

    nq = _rms(c_q, gq_ref[...]).astype(BF16)
    qa = _dot(nq, wq_ref[...])
    qb = _dot(nq, wqs_ref[...])
    nkv = _rms(c_kv, gkv_ref[...]).astype(BF16)
    kn = _dot(nkv, wk_ref[...])
    vt_ref[...] = _dot_nt(wv_ref[...], nkv).astype(BF16)
    k_pe = kr * ck_ref[...] + kr_sw * sk_ref[...]
    cq, sq = cq_ref[...], sq_ref[...]
    for hd in range(ATTN_HEADS):
        sl = slice(hd * HEAD_SLOT, (hd + 1) * HEAD_SLOT)
        q_ref[:, sl] = (qa[:, sl] * cq + qb[:, sl] * sq).astype(BF16)
        k_ref[:, sl] = (kn[:, sl] + k_pe).astype(BF16)


def _ln_proj(x2, lg, lb, wall, gq, wq, wqs, gkv, wk, wv, cq, sq, ck, sk, seq, tt):
    t = x2.shape[0]
    npos = seq // tt
    full = lambda a: pl.BlockSpec(a.shape, lambda i: (0,) * a.ndim)
    tab = pl.BlockSpec((tt, HEAD_SLOT), lambda i: (i % npos, 0))
    row = lambda w: pl.BlockSpec((tt, w), lambda i: (i, 0))
    hw = wall.shape[1] - N_MLA_COLS
    return pl.pallas_call(
        _ln_proj_kernel,
        grid=(t // tt,),
        in_specs=[row(D_MODEL), full(lg), full(lb), full(wall), full(gq), full(wq), full(wqs), full(gkv),
                  full(wk), full(wv), tab, tab, tab, tab],
        out_specs=[row(ATTN_HEADS * HEAD_SLOT), row(ATTN_HEADS * HEAD_SLOT),
                   pl.BlockSpec((ATTN_WIDTH, tt), lambda i: (0, i)), row(hw)],
        out_shape=[jax.ShapeDtypeStruct((t, ATTN_HEADS * HEAD_SLOT), BF16),
                   jax.ShapeDtypeStruct((t, ATTN_HEADS * HEAD_SLOT), BF16),
                   jax.ShapeDtypeStruct((ATTN_WIDTH, t), BF16),
                   jax.ShapeDtypeStruct((t, hw), BF16)],
        compiler_params=_cparams(("parallel",)),
        name="ln_proj",
    )(x2, lg, lb, wall, gq, wq, wqs, gkv, wk, wv, cq, sq, ck, sk)


def _attn_kernel(q_ref, k_ref, vt_ref, g_ref, o_ref, s0_ref, m0_ref, s1_ref, m1_ref, *, nsteps):
    n = pl.program_id(0)

    def scores(s_ref, m_ref):
        for hh in range(2):
            sl = slice(hh * HEAD_SLOT, (hh + 1) * HEAD_SLOT)
            s = _dot_nt(k_ref[:, sl], q_ref[:, sl])
            s_ref[hh] = s
            m_ref[hh] = jnp.max(s, axis=0, keepdims=True)

    def finish(s_ref, m_ref):
        outs = []
        ones = jnp.ones((16, vt_ref.shape[1]), BF16)
        for hh in range(2):
            vs = slice(hh * V_HEAD_DIM, (hh + 1) * V_HEAD_DIM)
            p = jnp.exp2(s_ref[hh] - m_ref[hh]).astype(BF16)
            ov = _dot(jnp.concatenate([vt_ref[vs, :], ones], axis=0), p)
            o = ov[:V_HEAD_DIM] / ov[V_HEAD_DIM:V_HEAD_DIM + 1]
            ms = jnp.mean(o * o, axis=0, keepdims=True)
            outs.append(o * lax.rsqrt(ms + RMS_EPS) * g_ref[vs, :])
        o_ref[...] = jnp.concatenate(outs, axis=0).T.astype(BF16)

    @pl.when(n == 0)
    def _():
        s1_ref[...] = jnp.zeros_like(s1_ref)
        m1_ref[...] = jnp.zeros_like(m1_ref)

    @pl.when((n < nsteps) & (n % 2 == 0))
    def _():
        finish(s1_ref, m1_ref)
        scores(s0_ref, m0_ref)

    @pl.when((n < nsteps) & (n % 2 == 1))
    def _():
        finish(s0_ref, m0_ref)
        scores(s1_ref, m1_ref)

    @pl.when(n == nsteps)
    def _():
        if nsteps % 2 == 0:
            finish(s1_ref, m1_ref)
        else:
            finish(s0_ref, m0_ref)


def _attention(q, k, vt, g, batch, seq, tq):
    t = q.shape[0]
    nq = seq // tq
    npair = ATTN_HEADS // 2
    nsteps = batch * npair * nq

    def tile(n):
        n = jnp.clip(n, 0, nsteps - 1)
        return n // (npair * nq), (n // nq) % npair, n % nq

    def q_map(n):
        b, p, i = tile(n)
        return b * nq + i, p

    def k_map(n):
        b, p, _ = tile(n)
        return b, p

    def vt_map(n):
        b, p, _ = tile(n - 1)
        return p, b

    def g_map(n):
        return tile(n - 1)[1], 0

    def o_map(n):
        b, p, i = tile(n - 1)
        return b * nq + i, p

    return pl.pallas_call(
        functools.partial(_attn_kernel, nsteps=nsteps),
        grid=(nsteps + 1,),
        in_specs=[pl.BlockSpec((tq, 2 * HEAD_SLOT), q_map),
                  pl.BlockSpec((seq, 2 * HEAD_SLOT), k_map),
                  pl.BlockSpec((2 * V_HEAD_DIM, seq), vt_map),
                  pl.BlockSpec((2 * V_HEAD_DIM, 1), g_map)],
        out_specs=pl.BlockSpec((tq, 2 * V_HEAD_DIM), o_map),
        out_shape=jax.ShapeDtypeStruct((t, ATTN_WIDTH), BF16),
        scratch_shapes=[pltpu.VMEM((2, seq, tq), F32), pltpu.VMEM((2, 1, tq), F32),
                        pltpu.VMEM((2, seq, tq), F32), pltpu.VMEM((2, 1, tq), F32)],
        compiler_params=_cparams(("arbitrary",)),
        name="attention",
    )(q, k, vt, g)


def _short_conv_kernel(prev_ref, x_ref, next_ref, w_ref, b_ref, o_ref):
    j = pl.program_id(1)
    nj = pl.num_programs(1)
    x = x_ref[0].astype(F32)
    rows = x.shape[0]
    before = jnp.where(j > 0, prev_ref[0, 7:8, :].astype(F32), 0.0)
    after = jnp.where(j < nj - 1, next_ref[0, 0:1, :].astype(F32), 0.0)
    rid = lax.broadcasted_iota(I32, x.shape, 0)
    xm = jnp.where(rid == 0, before, pltpu.roll(x, 1, 0))
    xp = jnp.where(rid == rows - 1, after, pltpu.roll(x, rows - 1, 0))
    u = b_ref[...] + xm * w_ref[0:1, :] + x * w_ref[1:2, :] + xp * w_ref[2:3, :]
    o_ref[...] = u.T.astype(BF16)


def _short_conv(hy3, w, b):
    batch, seq, width = hy3.shape
    nj = seq // TOEP
    sub = TOEP // 8
    last8 = seq // 8 - 1
    return pl.pallas_call(
        _short_conv_kernel,
        grid=(batch, nj),
        in_specs=[pl.BlockSpec((1, 8, width), lambda bb, j: (bb, jnp.maximum(j * sub - 1, 0), 0)),
                  pl.BlockSpec((1, TOEP, width), lambda bb, j: (bb, j, 0)),
                  pl.BlockSpec((1, 8, width), lambda bb, j: (bb, jnp.minimum((j + 1) * sub, last8), 0)),
                  pl.BlockSpec(w.shape, lambda bb, j: (0, 0)),
                  pl.BlockSpec(b.shape, lambda bb, j: (0, 0))],
        out_specs=pl.BlockSpec((width, TOEP), lambda bb, j: (0, j * batch + bb)),
        out_shape=jax.ShapeDtypeStruct((width, nj * batch * TOEP), BF16),
        compiler_params=_cparams(("parallel", "parallel")),
        name="short_conv",
    )(hy3, hy3, hy3, w, b)


def _filter_kernel(z_ref, w1_ref, b1_ref, f1_ref, w2_ref, b2_ref, f2_ref, w3_ref, dl_ref, g_ref):
    hp = lax.Precision.HIGHEST
    z = z_ref[...]
    n = z.shape[1]
    half = n // 2
    h1 = jnp.sin(f1_ref[...] * (jnp.dot(w1_ref[...], z, precision=hp, preferred_element_type=F32) + b1_ref[...]))
    h2 = jnp.sin(f2_ref[...] * (jnp.dot(w2_ref[...], h1, precision=hp, preferred_element_type=F32) + b2_ref[...]))
    decay = jnp.exp(-z[0:1, :] * dl_ref[...])
    lane = lax.broadcasted_iota(I32, (1, n), 1)
    masks = (lane >= half, (lane >= 1) & (lane <= half))
    out = None
    for d in range(2):
        hd = jnp.dot(w3_ref[0, d], h2, precision=hp, preferred_element_type=F32) * decay
        hd = jnp.where(masks[d], hd, 0.0)
        hd = hd / (jnp.sum(jnp.abs(hd), axis=1, keepdims=True) + 1e-6)
        out = hd if out is None else out + hd
    g_ref[0] = out


def _filters(zt, w1t, b1, f1, w2t, b2, f2, w3t, dl):
    n = zt.shape[1]
    cb = 128
    full = lambda a: pl.BlockSpec(a.shape, lambda o, c: (0,) * a.ndim)
    return pl.pallas_call(
        _filter_kernel,
        grid=(HY_ORDER, HY_WIDTH // cb),
        in_specs=[full(zt), full(w1t), full(b1), full(f1), full(w2t), full(b2), full(f2),
                  pl.BlockSpec((1, 2, cb, HY_FILTER_HIDDEN), lambda o, c: (o, 0, c, 0)),
                  pl.BlockSpec((cb, 1), lambda o, c: (c, 0))],
        out_specs=pl.BlockSpec((1, cb, n), lambda o, c: (o, c, 0)),
        out_shape=jax.ShapeDtypeStruct((HY_ORDER, HY_WIDTH, n), F32),
        compiler_params=_cparams(("parallel", "parallel")),
        name="filters",
    )(zt, w1t, b1, f1, w2t, b2, f2, w3t, dl)


def _hyena_kernel(bias_ref, v_ref, x1_ref, x2_ref, g_ref, o_ref, acc_ref, *, batch, nblk, cb):
    rows = batch * nblk
    seq = nblk * TOEP
    c0 = pl.program_id(0) * cb
    gates = (x1_ref, x2_ref)

    def per_channel(c, carry):
        zf = v_ref[c].astype(F32)
        for o in range(HY_ORDER):
            zb = zf.astype(BF16)
            acc_ref[...] = jnp.zeros_like(acc_ref)
            for d in range(-(nblk - 1), nblk):
                base = seq + TOEP * d - TOEP
                win = g_ref[o, c, :, pl.ds(base, 2 * TOEP)]
                w = pltpu.roll(jnp.broadcast_to(win, (TOEP, 2 * TOEP)), TOEP, 1, stride=1, stride_axis=0)
                w = w[:, :TOEP].astype(BF16)
                n = rows - batch * abs(d)
                if d >= 0:
                    acc_ref[batch * d:, :] += _dot(zb[:n], w)
                else:
                    acc_ref[:n, :] += _dot(zb[batch * (-d):], w)
            zf = gates[o][c].astype(F32) * (acc_ref[...] + bias_ref[o, c0 + c] * zf)
        o_ref[c] = zf.astype(BF16)
        return carry

    lax.fori_loop(0, cb, per_channel, 0)


def _hyena(bias, ut, g4, batch, nblk, cb):
    rows = batch * nblk
    nc = HY_WIDTH // cb
    kern = functools.partial(_hyena_kernel, batch=batch, nblk=nblk, cb=cb)
    blk = lambda off: pl.BlockSpec((cb, rows, TOEP), lambda i: (off * nc + i, 0, 0))
    return pl.pallas_call(
        kern,
        grid=(nc,),
        in_specs=[pl.BlockSpec(memory_space=pltpu.SMEM), blk(0), blk(1), blk(2),
                  pl.BlockSpec((HY_ORDER, cb, 1, g4.shape[3]), lambda i: (0, i, 0, 0))],
        out_specs=pl.BlockSpec((cb, rows, TOEP), lambda i: (i, 0, 0)),
        out_shape=jax.ShapeDtypeStruct((HY_WIDTH, rows, TOEP), BF16),
        scratch_shapes=[pltpu.VMEM((rows, TOEP), F32)],
        compiler_params=_cparams(("parallel",)),
        name="hyena",
    )(bias, ut, ut, ut, g4)


def _mix_kernel(x_ref, lg_ref, lb_ref, a_ref, yt_ref, hg_ref, wo_ref, mg_ref, mb_ref, wq_ref,
                h2_ref, h2b_ref, qt_ref, cat_ref):
    y = yt_ref[...].astype(F32).T
    cat_ref[:, :ATTN_WIDTH] = a_ref[...]
    gw = HY_WIDTH // HY_GROUPS
    for g in range(HY_GROUPS):
        sl = slice(g * gw, (g + 1) * gw)
        cat_ref[:, ATTN_WIDTH + g * gw:ATTN_WIDTH + (g + 1) * gw] = _rms(y[:, sl], hg_ref[:, sl]).astype(BF16)
    mix = _dot(cat_ref[...], wo_ref[...])
    h = _layer_norm(x_ref[...], lg_ref[...], lb_ref[...])
    h2 = _layer_norm(ALPHA * h + mix, mg_ref[...], mb_ref[...])
    h2_ref[...] = h2
    h2b = h2.astype(BF16)
    h2b_ref[...] = h2b
    qt_ref[...] = _dot_nt(wq_ref[...], h2b).astype(BF16)


def _mix(x2, lg, lb, a, yt2, hg, wo, mg, mb, wqt, batch, nblk):
    t = x2.shape[0]
    full = lambda w: pl.BlockSpec(w.shape, lambda i: (0,) * w.ndim)
    row = lambda w: pl.BlockSpec((TOEP, w), lambda i: (i, 0))
    return pl.pallas_call(
        _mix_kernel,
        grid=(t // TOEP,),
        in_specs=[row(D_MODEL), full(lg), full(lb), row(ATTN_WIDTH),
                  pl.BlockSpec((HY_WIDTH, TOEP), lambda i: (0, (i % nblk) * batch + i // nblk)),
                  full(hg), full(wo), full(mg), full(mb), full(wqt)],
        out_specs=[row(D_MODEL), row(D_MODEL), pl.BlockSpec((wqt.shape[0], TOEP), lambda i: (0, i))],
        out_shape=[jax.ShapeDtypeStruct((t, D_MODEL), F32), jax.ShapeDtypeStruct((t, D_MODEL), BF16),
                   jax.ShapeDtypeStruct((wqt.shape[0], t), BF16)],
        scratch_shapes=[pltpu.VMEM((TOEP, D_MODEL), BF16)],
        compiler_params=_cparams(("parallel",)),
        name="mix",
    )(x2, lg, lb, a, yt2, hg, wo, mg, mb, wqt)


N_KEYSETS = 2 * PEER_HEADS


SORT_GROUP = 4


def _top16(s, val_ref, idx_ref, lanes):
    neg = -jnp.inf
    sub = lax.broadcasted_iota(I32, (8, s.shape[1]), 0)
    ntile = s.shape[0] // 8
    groups = []
    for g0 in range(0, ntile, SORT_GROUP):
        t = [s[8 * v:8 * v + 8, :] for v in range(g0, g0 + SORT_GROUP)]
        ix = [sub + 8 * v for v in range(g0, g0 + SORT_GROUP)]
        for end in range(SORT_GROUP - 1, 0, -1):
            for p in range(end):
                swap = t[p + 1] > t[p]
                t[p], t[p + 1] = jnp.where(swap, t[p + 1], t[p]), jnp.where(swap, t[p], t[p + 1])
                ix[p], ix[p + 1] = jnp.where(swap, ix[p + 1], ix[p]), jnp.where(swap, ix[p], ix[p + 1])
        groups.append((t, ix))
    for r in range(PEER_TOPK):
        heads = [t[0] for t, _ in groups]
        while len(heads) > 1:
            heads = [jnp.maximum(heads[k], heads[k + 1]) for k in range(0, len(heads), 2)]
        m = jnp.max(heads[0], axis=0, keepdims=True)
        cand = [jnp.where(t[0] == m, ix[0], PEER_NKEYS) for t, ix in groups]
        while len(cand) > 1:
            cand = [jnp.minimum(cand[k], cand[k + 1]) for k in range(0, len(cand), 2)]
        idx = jnp.min(cand[0], axis=0, keepdims=True)
        val_ref[r:r + 1, lanes] = m
        idx_ref[r:r + 1, lanes] = idx
        for t, ix in groups:
            hit = ix[0] == idx
            for p in range(SORT_GROUP - 1):
                t[p] = jnp.where(hit, t[p + 1], t[p])
                ix[p] = jnp.where(hit, ix[p + 1], ix[p])
            t[-1] = jnp.where(hit, neg, t[-1])


_CODE_BITS = 14


def _pair_top16(v1_ref, i1_ref, v2_ref, i2_ref, best_ref, code_ref, lanes):
    neg = -jnp.inf
    width = v1_ref[0:8, lanes].shape[1]
    sub = lax.broadcasted_iota(I32, (8, width), 0)
    bc = lambda ref, r: jnp.broadcast_to(ref[r:r + 1, lanes], (8, width))

    def tiles(a_ref, b_ref, combine):
        b_lo = b_ref[0:8, lanes]
        a_hi = pltpu.roll(a_ref[8:16, lanes], 2, 0)
        out = [combine(bc(a_ref, 0), b_lo), combine(bc(a_ref, 0), b_ref[8:16, lanes]), combine(bc(a_ref, 1), b_lo)]
        out.append(jnp.where(sub < 5, combine(bc(a_ref, 2), b_lo), combine(bc(a_ref, 4), pltpu.roll(b_lo, 5, 0))))
        out.append(jnp.where(sub < 4, combine(bc(a_ref, 3), b_lo),
                             jnp.where(sub < 6, combine(bc(a_ref, 5), pltpu.roll(b_lo, 4, 0)),
                                       combine(bc(a_ref, 6), pltpu.roll(b_lo, 6, 0)))))
        out.append(jnp.where(sub < 2, combine(bc(a_ref, 7), b_lo), combine(a_hi, bc(b_ref, 0))))
        out.append(combine(a_hi, bc(b_ref, 0)))
        return out

    pos = [sub, sub + 8, sub + 16,
           jnp.where(sub < 5, sub + 32, sub + (64 - 5)),
           jnp.where(sub < 4, sub + 48, jnp.where(sub < 6, sub + (80 - 4), sub + (96 - 6))),
           jnp.where(sub < 2, sub + 112, (sub + 6) * 16),
           (sub + 14) * 16]
    cand = tiles(v1_ref, v2_ref, lambda a, b: a + b)
    cand[-1] = jnp.where(sub < 2, cand[-1], neg)
    code = tiles(i1_ref, i2_ref, lambda a, b: a * PEER_NKEYS + b)
    key = [p * (1 << _CODE_BITS) + c for p, c in zip(pos, code)]
    big = jnp.int32(1 << 30)

    def tree(xs, op):
        xs = list(xs)
        while len(xs) > 1:
            xs = [op(xs[k], xs[k + 1]) for k in range(0, len(xs) - 1, 2)] + ([xs[-1]] if len(xs) % 2 else [])
        return xs[0]

    for kk in range(PEER_TOPK):
        m = jnp.max(tree(cand, jnp.maximum), axis=0, keepdims=True)
        kmin = jnp.min(tree([jnp.where(c == m, k, big) for c, k in zip(cand, key)], jnp.minimum), axis=0, keepdims=True)
        best_ref[kk:kk + 1, lanes] = m
        code_ref[kk:kk + 1, lanes] = kmin & ((1 << _CODE_BITS) - 1)
        cand = [jnp.where(k == kmin, neg, c) for c, k in zip(cand, key)]


def _peer_up_kernel(x_ref, u_ref, q_ref, keys_ref, w_ref, isel_ref, jsel_ref,
                    a0_ref, a1_ref, v1_ref, i1_ref, v2_ref, i2_ref, best_ref, codeh_ref, codet_ref, gatet_ref,
                    iseln_ref, jseln_ref, gaten_ref, *, ne):
    i = pl.program_id(0)
    e = pl.program_id(1)
    tt = x_ref.shape[0]
    nsub = u_ref.shape[1] // PEER_NKEYS
    prev = (i + 1) % 2
    cur = i % 2
    chunks = [slice(c, c + LANES) for c in range(0, tt, LANES)]

    def pick(e_src, src, r_lo, r_hi):
        isel_p, jsel_p = iseln_ref.at[prev], jseln_ref.at[prev]
        for r0 in range(r_lo, r_hi, GATHER_ROWS):
            rows = slice(r0, r0 + GATHER_ROWS)
            isel = isel_p[rows, :]
            jsel = jsel_p[rows, :]
            acc = w_ref[rows, :]
            for ii in range(nsub):
                got = jnp.take_along_axis(src[rows, ii * PEER_NKEYS:(ii + 1) * PEER_NKEYS], jsel, axis=1)
                acc = jnp.where(isel == e_src * nsub + ii, got, acc)
            w_ref[rows, :] = acc
        return acc

    def step(dst, src, val_ref, idx_ref, pair):
        assert len(chunks) == 2 * STEP_SLICES
        s = _dot(keys_ref[0], q_ref[...])
        eb = u_ref.shape[1]
        never = e < 0
        bounds = [0] + [2 * c + 1 for c in range(STEP_SLICES)] + [len(chunks)]
        deps = ()
        for c in range(STEP_SLICES + 1):
            nxt = deps
            if c < STEP_SLICES:
                cols = slice(c * eb // STEP_SLICES, (c + 1) * eb // STEP_SLICES)
                res = _dot(x_ref[...], u_ref[:, cols])
                dst[:, cols] = res
                acc = pick(e - 1, src, c * tt // STEP_SLICES, (c + 1) * tt // STEP_SLICES)
                nxt = (res[:PEER_NKEYS, :LANES], jnp.broadcast_to(acc[0:1, :], (PEER_NKEYS, LANES)))
            for lanes in chunks[bounds[c]:bounds[c + 1]]:
                sc = s[:, lanes]
                for dep in deps:
                    sc = jnp.where(never, dep, sc)
                _top16(sc, val_ref, idx_ref, lanes)
                if pair:
                    _pair_top16(v1_ref, i1_ref, v2_ref, i2_ref, best_ref, codeh_ref, lanes)
            deps = nxt

    @pl.when((i == 0) & (e == 0))
    def _():
        a1_ref[...] = jnp.zeros_like(a1_ref)
        iseln_ref[...] = jnp.zeros_like(iseln_ref)
        jseln_ref[...] = jnp.zeros_like(jseln_ref)
        gaten_ref[...] = jnp.zeros_like(gaten_ref)

    @pl.when(e == 0)
    def _():
        w_ref[...] = jnp.zeros_like(w_ref)

    @pl.when((e < ne) & (e % 2 == 0))
    def _():
        step(a0_ref, a1_ref, v1_ref, i1_ref, False)

    @pl.when((e < ne) & (e % 2 == 1))
    def _():
        step(a1_ref, a0_ref, v2_ref, i2_ref, True)
        best = best_ref[...]
        ex = jnp.exp(best - best[0:1, :])
        row0 = pl.multiple_of((e // 2) * PEER_TOPK, PEER_TOPK)
        gatet_ref[pl.ds(row0, PEER_TOPK), :] = ex / jnp.sum(ex, axis=0, keepdims=True)
        codet_ref[pl.ds(row0, PEER_TOPK), :] = codeh_ref[...]

    @pl.when(e == ne)
    def _():
        pick(e - 1, a1_ref if ne % 2 == 0 else a0_ref, 0, tt)
        s = w_ref[...]
        gelu = 0.5 * s * (1.0 + lax.erf(s * (2.0 ** -0.5)))
        w_ref[...] = gaten_ref[prev] * gelu
        code_n = codet_ref[...].T
        isel = code_n >> 7
        jsel = code_n & (PEER_NKEYS - 1)
        isel_ref[...] = isel
        jsel_ref[...] = jsel
        iseln_ref[cur] = isel
        jseln_ref[cur] = jsel
        gaten_ref[cur] = gatet_ref[...].T


def _peer_up(h2b, ut, qt, keys, tt, eb):
    t = h2b.shape[0]
    ne = ut.shape[1] // eb
    assert ne == N_KEYSETS
    ntile = t // tt
    half = PEER_DK // 2
    tok = lambda i, e: (jnp.maximum(i - 1, 0), 0)
    rt = lambda i, e: (jnp.minimum(i, ntile - 1), 0)
    vm = lambda shape, dt: pltpu.VMEM(shape, dt)
    return pl.pallas_call(
        functools.partial(_peer_up_kernel, ne=ne),
        grid=(ntile + 1, ne + 1),
        in_specs=[pl.BlockSpec((tt, D_MODEL), tok),
                  pl.BlockSpec((D_MODEL, eb), lambda i, e: (0, jnp.minimum(e, ne - 1))),
                  pl.BlockSpec((half, tt), lambda i, e: (jnp.minimum(e, ne - 1), jnp.minimum(i, ntile - 1))),
                  pl.BlockSpec((1, PEER_NKEYS, half), lambda i, e: (jnp.minimum(e, ne - 1), 0, 0))],
        out_specs=[pl.BlockSpec((tt, N_SEL), tok), pl.BlockSpec((tt, N_SEL), rt), pl.BlockSpec((tt, N_SEL), rt)],
        out_shape=[jax.ShapeDtypeStruct((t, N_SEL), F32), jax.ShapeDtypeStruct((t, N_SEL), I32),
                   jax.ShapeDtypeStruct((t, N_SEL), I32)],
        scratch_shapes=[vm((tt, eb), F32), vm((tt, eb), F32),
                        vm((PEER_TOPK, tt), F32), vm((PEER_TOPK, tt), I32),
                        vm((PEER_TOPK, tt), F32), vm((PEER_TOPK, tt), I32),
                        vm((PEER_TOPK, tt), F32), vm((PEER_TOPK, tt), I32),
                        vm((N_SEL, tt), I32), vm((N_SEL, tt), F32),
                        vm((2, tt, N_SEL), I32), vm((2, tt, N_SEL), I32), vm((2, tt, N_SEL), F32)],
        compiler_params=_cparams(("arbitrary", "arbitrary")),
        name="peer_up",
    )(h2b, ut, qt, keys)


def _peer_down_kernel(isel_ref, jsel_ref, w_ref, v_ref, h2_ref, lg_ref, lb_ref, o_ref, wd0_ref, wd1_ref, acc_ref,
                      *, ib, ne):
    i = pl.program_id(0)
    e = pl.program_id(1)
    tt = isel_ref.shape[0]
    per_step = tt // ne
    half = PEER_NKEYS // 2
    hi_mask = jnp.uint32(0xFFFF0000)

    def scatter(dst, g0, dep):
        sub = lax.broadcasted_iota(I32, (PEER_NKEYS, N_SEL), 0)
        base = pl.multiple_of(e * per_step + g0, 8)
        isb = isel_ref[pl.ds(base, 8), :]
        if dep is not None:
            isb = jnp.where(i < 0, dep, isb)
        jsb = jsel_ref[pl.ds(base, 8), :]
        wb = w_ref[pl.ds(base, 8), :]
        zero = jnp.zeros((N_SEL, PEER_NKEYS), BF16)
        for r in range(0, 8, 2):
            pts, qs = [], []
            for rr in (r, r + 1):
                pts.append(jnp.where(sub == isb[rr:rr + 1, :], wb[rr:rr + 1, :], 0.0).astype(BF16))
                qs.append(jnp.where(sub == jsb[rr:rr + 1, :], 1.0, 0.0).T.astype(BF16))
            rhs = jnp.concatenate([jnp.concatenate([qs[0], zero], axis=1),
                                   jnp.concatenate([zero, qs[1]], axis=1)], axis=0)
            g = _dot(jnp.concatenate(pts, axis=1), rhs)
            for k, rr in enumerate((r, r + 1)):
                gb = pltpu.bitcast(g[:, k * PEER_NKEYS:(k + 1) * PEER_NKEYS], jnp.uint32)
                packed = (gb[half:] & hi_mask) | (gb[:half] >> 16)
                dst[pl.ds(pl.multiple_of((base + rr) * W_PITCH, 8), half), :] = packed

    def step(dst, src):
        parts = []
        for ii in range(ib):
            pk = src[pl.ds(e * ib + ii, tt, stride=W_PITCH), :]
            parts.append(pltpu.bitcast(pk << 16, F32).astype(BF16))
            parts.append(pltpu.bitcast(pk & hi_mask, F32).astype(BF16))
        lhs = jnp.concatenate(parts, axis=1)
        ngroup = per_step // 8
        width = acc_ref.shape[1] // ngroup
        dep = None
        for c in range(ngroup):
            cols = slice(c * width, (c + 1) * width)
            res = acc_ref[:, cols] + _dot(lhs, v_ref[:, cols])
            acc_ref[:, cols] = res
            scatter(dst, 8 * c, dep)
            dep = pltpu.bitcast(res[0:8, 0:LANES], I32)

    @pl.when((i == 0) & (e == 0))
    def _():
        wd1_ref[...] = jnp.zeros_like(wd1_ref)

    @pl.when(e == 0)
    def _():
        acc_ref[...] = jnp.zeros_like(acc_ref)

    @pl.when(i % 2 == 0)
    def _():
        step(wd0_ref, wd1_ref)

    @pl.when(i % 2 == 1)
    def _():
        step(wd1_ref, wd0_ref)

    @pl.when(e == ne - 1)
    def _():
        o_ref[...] = _layer_norm(ALPHA * h2_ref[...] + acc_ref[...], lg_ref[...], lb_ref[...])


def _peer_down(isel, jsel, w, vperm, h2, lg, lb, tt, ib):
    t = h2.shape[0]
    ne = (PEER_NKEYS // 2) // ib
    ntile = t // tt
    vb = ib * 2 * PEER_NKEYS
    assert tt % (8 * ne) == 0
    sel = pl.BlockSpec((tt, N_SEL), lambda i, e: (jnp.minimum(i, ntile - 1), 0))
    tok = pl.BlockSpec((tt, D_MODEL), lambda i, e: (jnp.maximum(i - 1, 0), 0))
    full = lambda a: pl.BlockSpec(a.shape, lambda i, e: (0,) * a.ndim)
    wd = pltpu.VMEM((tt * W_PITCH, N_SEL), jnp.uint32)
    return pl.pallas_call(
        functools.partial(_peer_down_kernel, ib=ib, ne=ne),
        grid=(ntile + 1, ne),
        in_specs=[sel, sel, sel, pl.BlockSpec((vb, D_MODEL), lambda i, e: (e, 0)), tok, full(lg), full(lb)],
        out_specs=tok,
        out_shape=jax.ShapeDtypeStruct((t, D_MODEL), F32),
        scratch_shapes=[wd, wd, pltpu.VMEM((tt, D_MODEL), F32)],
        compiler_params=_cparams(("arbitrary", "arbitrary")),
        name="peer_down",
    )(isel, jsel, w, vperm, h2, lg, lb)


def _rope_tables(seq):
    half = QK_ROPE_DIM // 2
    inv = 1.0 / (ROPE_THETA ** (jnp.arange(0, QK_ROPE_DIM, 2, dtype=F32) / QK_ROPE_DIM))
    ang = jnp.arange(seq, dtype=F32)[:, None] * inv[None, :]
    cos, sin = jnp.cos(ang), jnp.sin(ang)
    c32 = jnp.concatenate([cos, cos], axis=1)
    s32 = jnp.concatenate([-sin, sin], axis=1)
    z = lambda w: jnp.zeros((seq, w), F32)
    scale = (QK_NOPE_DIM + QK_ROPE_DIM) ** -0.5 * math.log2(math.e)
    pad = HEAD_SLOT - QK_NOPE_DIM - QK_ROPE_DIM
    cq = jnp.concatenate([jnp.full((seq, QK_NOPE_DIM), scale, F32), scale * c32, z(pad)], axis=1)
    sq = jnp.concatenate([z(QK_NOPE_DIM), scale * s32, z(pad)], axis=1)
    ck = jnp.concatenate([z(QK_NOPE_DIM), c32, z(pad)], axis=1)
    sk = jnp.concatenate([z(QK_NOPE_DIM), s32, z(pad)], axis=1)
    return cq, sq, ck, sk


def _swap_halves(w):
    half = w.shape[-1] // 2
    return jnp.concatenate([w[..., half:], w[..., :half]], axis=-1)


def _slot(w, offset):
    return jnp.pad(w, ((0, 0), (offset, HEAD_SLOT - offset - w.shape[1])))


def _position_features(seq):
    t = jnp.linspace(0.0, 1.0, seq, dtype=F32)[:, None]
    bands = (HY_EMB_DIM - 1) // 2
    w = 2.0 * math.pi * jnp.arange(seq, dtype=F32) / seq
    f = jnp.linspace(1e-4, bands - 1, bands, dtype=F32)
    ang = w[:, None] * f[None, :]
    return jnp.concatenate([t, jnp.cos(ang), -jnp.sin(ang)], axis=-1)


def kernel(x, emb_ln_g, emb_ln_b, w_in, q_norm_g, w_uq, kv_norm_g, w_ukv, hy_short_w, hy_short_b, hy_filt_w1,
           hy_filt_b1, hy_filt_freq1, hy_filt_w2, hy_filt_b2, hy_filt_freq2, hy_filt_w3, hy_bias, attn_out_g,
           hy_out_g, w_o, ln_mix_g, ln_mix_b, peer_wq, peer_sub_keys, peer_u, peer_v, ln_ffn_g, ln_ffn_b):
    batch, seq, _ = x.shape
    assert w_in.shape[0] == DEPTH == 1 and seq % TOEP == 0
    t = batch * seq
    nblk = seq // TOEP
    r2 = lambda a: a.reshape(1, -1)
    x2 = x.reshape(t, D_MODEL)
    lg, lb = r2(emb_ln_g), r2(emb_ln_b)

    wi = w_in[0]
    w_kr = wi[:, OFF_CKV:OFF_KR]
    wall = jnp.concatenate([wi[:, :OFF_CKV], _slot(w_kr, QK_NOPE_DIM), _slot(_swap_halves(w_kr), QK_NOPE_DIM),
                            wi[:, OFF_KR:]], axis=1).astype(BF16)
    dq = QK_NOPE_DIM + QK_ROPE_DIM
    wuq = w_uq[0].reshape(Q_LORA_RANK, ATTN_HEADS, dq)
    wq = jnp.pad(wuq, ((0, 0), (0, 0), (0, HEAD_SLOT - dq))).reshape(Q_LORA_RANK, -1).astype(BF16)
    wqs = jnp.pad(_swap_halves(wuq[..., QK_NOPE_DIM:]),
                  ((0, 0), (0, 0), (QK_NOPE_DIM, HEAD_SLOT - dq))).reshape(Q_LORA_RANK, -1).astype(BF16)
    wukv = w_ukv[0].reshape(KV_LORA_RANK, ATTN_HEADS, QK_NOPE_DIM + V_HEAD_DIM)
    wk = jnp.pad(wukv[..., :QK_NOPE_DIM],
                 ((0, 0), (0, 0), (0, HEAD_SLOT - QK_NOPE_DIM))).reshape(KV_LORA_RANK, -1).astype(BF16)
    wv = wukv[..., QK_NOPE_DIM:].reshape(KV_LORA_RANK, -1).T.astype(BF16)
    cq, sq, ck, sk = _rope_tables(seq)

    tt_proj = min(512, seq)
    q, k, vt, hy = _ln_proj(x2, lg, lb, wall, r2(q_norm_g[0]), wq, wqs, r2(kv_norm_g[0]), wk, wv,
                           cq, sq, ck, sk, seq, tt_proj)

    a = _attention(q, k, vt, attn_out_g[0].reshape(-1, 1), batch, seq, min(256, seq))

    ut = _short_conv(hy.reshape(batch, seq, -1), hy_short_w[0], r2(hy_short_b[0]))
    ut = ut.reshape((HY_ORDER + 1) * HY_WIDTH, nblk * batch, TOEP)

    lag = jnp.abs(jnp.arange(2 * seq) - seq)
    feats = _position_features(seq)
    zt = jnp.pad(feats[jnp.minimum(lag, seq - 1)].T, ((0, HY_FILTER_HIDDEN - HY_EMB_DIM), (0, 0)))
    col = lambda a_: a_.reshape(-1, 1)
    w1t = jnp.pad(hy_filt_w1[0].T, ((0, 0), (0, HY_FILTER_HIDDEN - HY_EMB_DIM)))
    w3t = hy_filt_w3[0].T.reshape(HY_ORDER, 2, HY_WIDTH, HY_FILTER_HIDDEN)
    deltas = jnp.abs(jnp.linspace(math.log(HY_TARGET) / HY_SLOW_DECAY, math.log(HY_TARGET) / HY_FAST_DECAY,
                                  HY_WIDTH, dtype=F32))
    g = _filters(zt, w1t, col(hy_filt_b1[0]), col(hy_filt_freq1[0]), hy_filt_w2[0].T, col(hy_filt_b2[0]),
                 col(hy_filt_freq2[0]), w3t, col(deltas))
    yt = _hyena(hy_bias[0], ut, g.reshape(HY_ORDER, HY_WIDTH, 1, 2 * seq), batch, nblk, 8)

    h2, h2b, qt = _mix(x2, lg, lb, a, yt.reshape(HY_WIDTH, nblk * batch * TOEP), r2(hy_out_g[0]),
                       w_o[0].astype(BF16), r2(ln_mix_g[0]), r2(ln_mix_b[0]), peer_wq[0].T.astype(BF16), batch, nblk)

    keys = peer_sub_keys[0].astype(BF16).reshape(N_KEYSETS, PEER_NKEYS, PEER_DK // 2)
    w, isel, jsel = _peer_up(h2b, peer_u[0].astype(BF16).T, qt, keys, min(1024, t), 1024)
    ib = 4
    half = PEER_NKEYS // 2
    vperm = peer_v[0].astype(BF16).reshape(2, half // ib, ib, PEER_NKEYS, D_MODEL)
    vperm = vperm.transpose(1, 2, 0, 3, 4).reshape(PEER_NKEYS * PEER_NKEYS, D_MODEL)
    out = _peer_down(isel, jsel, w, vperm, h2, r2(ln_ffn_g[0]), r2(ln_ffn_b[0]), min(512, t), ib)
    return out.reshape(batch, seq, D_MODEL)
```

```python
import functools
import math

import jax
import jax.numpy as jnp
from jax import lax
from jax.experimental import pallas as pl
from jax.experimental.pallas import tpu as pltpu

F32 = jnp.float32
BF16 = jnp.bfloat16
I32 = jnp.int32

D_MODEL = 1024
ATTN_HEADS = 8
QK_NOPE_DIM = 64
QK_ROPE_DIM = 32
V_HEAD_DIM = 64
Q_LORA_RANK = 256
KV_LORA_RANK = 128
ATTN_WIDTH = ATTN_HEADS * V_HEAD_DIM
ROPE_THETA = 10000.0
HY_WIDTH = D_MODEL - ATTN_WIDTH
HY_ORDER = 2
HY_GROUPS = 8
HY_SHORT = 3
HY_EMB_DIM = 33
HY_FILTER_HIDDEN = 64
HY_FAST_DECAY = 0.3
HY_SLOW_DECAY = 1.5
HY_TARGET = 1e-2
OFF_CQ = Q_LORA_RANK
OFF_CKV = OFF_CQ + KV_LORA_RANK
OFF_KR = OFF_CKV + QK_ROPE_DIM
PEER_HEADS = 8
PEER_NKEYS = 128
PEER_DK = 128
PEER_TOPK = 16
DEPTH = 1
ALPHA = (2 * DEPTH) ** 0.25
LN_EPS = 1e-5
RMS_EPS = 1e-6

LANES = 128
HEAD_SLOT = 128
TOEP = 256
N_SEL = PEER_HEADS * PEER_TOPK
W_PITCH = 72
GATHER_ROWS = 8
STEP_SLICES = 4
VMEM_LIMIT = 60 * 1024 * 1024


def _cparams(sem):
    return pltpu.CompilerParams(dimension_semantics=sem, vmem_limit_bytes=VMEM_LIMIT)


def _layer_norm(x, g, b):
    mu = jnp.mean(x, axis=-1, keepdims=True)
    xc = x - mu
    var = jnp.mean(xc * xc, axis=-1, keepdims=True)
    return xc * lax.rsqrt(var + LN_EPS) * g + b


def _rms(x, g):
    return x * lax.rsqrt(jnp.mean(x * x, axis=-1, keepdims=True) + RMS_EPS) * g


def _dot(a, b):
    return jnp.dot(a, b, preferred_element_type=F32)


def _dot_nt(a, b):
    return lax.dot_general(a, b, (((1,), (1,)), ((), ())), preferred_element_type=F32)


N_MLA_COLS = Q_LORA_RANK + KV_LORA_RANK + 2 * HEAD_SLOT


def _ln_proj_kernel(x_ref, lg_ref, lb_ref, wall_ref, gq_ref, wq_ref, wqs_ref, gkv_ref, wk_ref, wv_ref,
                    cq_ref, sq_ref, ck_ref, sk_ref, q_ref, k_ref, vt_ref, hy_ref):
    h = _layer_norm(x_ref[...], lg_ref[...], lb_ref[...])
    proj = _dot(h.astype(BF16), wall_ref[...])
    c_q = proj[:, :OFF_CQ]
    c_kv = proj[:, OFF_CQ:OFF_CKV]
    kr = proj[:, OFF_CKV:OFF_CKV + HEAD_SLOT]
    kr_sw = proj[:, OFF_CKV + HEAD_SLOT:N_MLA_COLS]
    hy_ref[...] = proj[:, N_MLA_COLS:].astype(BF16)

    nq = _rms(c_q, gq_ref[...]).astype(BF16)
    qa = _dot(nq, wq_ref[...])
    qb = _dot(nq, wqs_ref[...])
    nkv = _rms(c_kv, gkv_ref[...]).astype(BF16)
    kn = _dot(nkv, wk_ref[...])
    vt_ref[...] = _dot_nt(wv_ref[...], nkv).astype(BF16)
    k_pe = kr * ck_ref[...] + kr_sw * sk_ref[...]
    cq, sq = cq_ref[...], sq_ref[...]
    for hd in range(ATTN_HEADS):
        sl = slice(hd * HEAD_SLOT, (hd + 1) * HEAD_SLOT)
        q_ref[:, sl] = (qa[:, sl] * cq + qb[:, sl] * sq).astype(BF16)
        k_ref[:, sl] = (kn[:, sl] + k_pe).astype(BF16)


def _ln_proj(x2, lg, lb, wall, gq, wq, wqs, gkv, wk, wv, cq, sq, ck, sk, seq, tt):
    t = x2.shape[0]
    npos = seq // tt
    full = lambda a: pl.BlockSpec(a.shape, lambda i: (0,) * a.ndim)
    tab = pl.BlockSpec((tt, HEAD_SLOT), lambda i: (i % npos, 0))
    row = lambda w: pl.BlockSpec((tt, w), lambda i: (i, 0))
    hw = wall.shape[1] - N_MLA_COLS
    return pl.pallas_call(
        _ln_proj_kernel,
        grid=(t // tt,),
        in_specs=[row(D_MODEL), full(lg), full(lb), full(wall), full(gq), full(wq), full(wqs), full(gkv),
                  full(wk), full(wv), tab, tab, tab, tab],
        out_specs=[row(ATTN_HEADS * HEAD_SLOT), row(ATTN_HEADS * HEAD_SLOT),
                   pl.BlockSpec((ATTN_WIDTH, tt), lambda i: (0, i)), row(hw)],
        out_shape=[jax.ShapeDtypeStruct((t, ATTN_HEADS * HEAD_SLOT), BF16),
                   jax.ShapeDtypeStruct((t, ATTN_HEADS * HEAD_SLOT), BF16),
                   jax.ShapeDtypeStruct((ATTN_WIDTH, t), BF16),
                   jax.ShapeDtypeStruct((t, hw), BF16)],
        compiler_params=_cparams(("parallel",)),
        name="ln_proj",
    )(x2, lg, lb, wall, gq, wq, wqs, gkv, wk, wv, cq, sq, ck, sk)


def _attn_kernel(q_ref, k_ref, vt_ref, g_ref, o_ref, s0_ref, m0_ref, s1_ref, m1_ref, *, nsteps):
    n = pl.program_id(0)

    def scores(s_ref, m_ref):
        for hh in range(2):
            sl = slice(hh * HEAD_SLOT, (hh + 1) * HEAD_SLOT)
            s = _dot_nt(k_ref[:, sl], q_ref[:, sl])
            s_ref[hh] = s
            m_ref[hh] = jnp.max(s, axis=0, keepdims=True)

    def finish(s_ref, m_ref):
        outs = []
        ones = jnp.ones((16, vt_ref.shape[1]), BF16)
        for hh in range(2):
            vs = slice(hh * V_HEAD_DIM, (hh + 1) * V_HEAD_DIM)
            p = jnp.exp2(s_ref[hh] - m_ref[hh]).astype(BF16)
            ov = _dot(jnp.concatenate([vt_ref[vs, :], ones], axis=0), p)
            o = ov[:V_HEAD_DIM] / ov[V_HEAD_DIM:V_HEAD_DIM + 1]
            ms = jnp.mean(o * o, axis=0, keepdims=True)
            outs.append(o * lax.rsqrt(ms + RMS_EPS) * g_ref[vs, :])
        o_ref[...] = jnp.concatenate(outs, axis=0).T.astype(BF16)

    @pl.when(n == 0)
    def _():
        s1_ref[...] = jnp.zeros_like(s1_ref)
        m1_ref[...] = jnp.zeros_like(m1_ref)

    @pl.when((n < nsteps) & (n % 2 == 0))
    def _():
        finish(s1_ref, m1_ref)
        scores(s0_ref, m0_ref)

    @pl.when((n < nsteps) & (n % 2 == 1))
    def _():
        finish(s0_ref, m0_ref)
        scores(s1_ref, m1_ref)

    @pl.when(n == nsteps)
    def _():
        if nsteps % 2 == 0:
            finish(s1_ref, m1_ref)
        else:
            finish(s0_ref, m0_ref)


def _attention(q, k, vt, g, batch, seq, tq):
    t = q.shape[0]
    nq = seq // tq
    npair = ATTN_HEADS // 2
    nsteps = batch * npair * nq

    def tile(n):
        n = jnp.clip(n, 0, nsteps - 1)
        return n // (npair * nq), (n // nq) % npair, n % nq

    def q_map(n):
        b, p, i = tile(n)
        return b * nq + i, p

    def k_map(n):
        b, p, _ = tile(n)
        return b, p

    def vt_map(n):
        b, p, _ = tile(n - 1)
        return p, b

    def g_map(n):
        return tile(n - 1)[1], 0

    def o_map(n):
        b, p, i = tile(n - 1)
        return b * nq + i, p

    return pl.pallas_call(
        functools.partial(_attn_kernel, nsteps=nsteps),
        grid=(nsteps + 1,),
        in_specs=[pl.BlockSpec((tq, 2 * HEAD_SLOT), q_map),
                  pl.BlockSpec((seq, 2 * HEAD_SLOT), k_map),
                  pl.BlockSpec((2 * V_HEAD_DIM, seq), vt_map),
                  pl.BlockSpec((2 * V_HEAD_DIM, 1), g_map)],
        out_specs=pl.BlockSpec((tq, 2 * V_HEAD_DIM), o_map),
        out_shape=jax.ShapeDtypeStruct((t, ATTN_WIDTH), BF16),
        scratch_shapes=[pltpu.VMEM((2, seq, tq), F32), pltpu.VMEM((2, 1, tq), F32),
                        pltpu.VMEM((2, seq, tq), F32), pltpu.VMEM((2, 1, tq), F32)],
        compiler_params=_cparams(("arbitrary",)),
        name="attention",
    )(q, k, vt, g)


def _short_conv_kernel(prev_ref, x_ref, next_ref, w_ref, b_ref, o_ref):
    j = pl.program_id(1)
    nj = pl.num_programs(1)
    x = x_ref[0].astype(F32)
    rows = x.shape[0]
    before = jnp.where(j > 0, prev_ref[0, 7:8, :].astype(F32), 0.0)
    after = jnp.where(j < nj - 1, next_ref[0, 0:1, :].astype(F32), 0.0)
    rid = lax.broadcasted_iota(I32, x.shape, 0)
    xm = jnp.where(rid == 0, before, pltpu.roll(x, 1, 0))
    xp = jnp.where(rid == rows - 1, after, pltpu.roll(x, rows - 1, 0))
    u = b_ref[...] + xm * w_ref[0:1, :] + x * w_ref[1:2, :] + xp * w_ref[2:3, :]
    o_ref[...] = u.T.astype(BF16)


def _short_conv(hy3, w, b):
    batch, seq, width = hy3.shape
    nj = seq // TOEP
    sub = TOEP // 8
    last8 = seq // 8 - 1
    return pl.pallas_call(
        _short_conv_kernel,
        grid=(batch, nj),
        in_specs=[pl.BlockSpec((1, 8, width), lambda bb, j: (bb, jnp.maximum(j * sub - 1, 0), 0)),
                  pl.BlockSpec((1, TOEP, width), lambda bb, j: (bb, j, 0)),
                  pl.BlockSpec((1, 8, width), lambda bb, j: (bb, jnp.minimum((j + 1) * sub, last8), 0)),
                  pl.BlockSpec(w.shape, lambda bb, j: (0, 0)),
                  pl.BlockSpec(b.shape, lambda bb, j: (0, 0))],
        out_specs=pl.BlockSpec((width, TOEP), lambda bb, j: (0, j * batch + bb)),
        out_shape=jax.ShapeDtypeStruct((width, nj * batch * TOEP), BF16),
        compiler_params=_cparams(("parallel", "parallel")),
        name="short_conv",
    )(hy3, hy3, hy3, w, b)


def _filter_kernel(z_ref, w1_ref, b1_ref, f1_ref, w2_ref, b2_ref, f2_ref, w3_ref, dl_ref, g_ref):
    hp = lax.Precision.HIGHEST
    z = z_ref[...]
    n = z.shape[1]
    half = n // 2
    h1 = jnp.sin(f1_ref[...] * (jnp.dot(w1_ref[...], z, precision=hp, preferred_element_type=F32) + b1_ref[...]))
    h2 = jnp.sin(f2_ref[...] * (jnp.dot(w2_ref[...], h1, precision=hp, preferred_element_type=F32) + b2_ref[...]))
    decay = jnp.exp(-z[0:1, :] * dl_ref[...])
    lane = lax.broadcasted_iota(I32, (1, n), 1)
    masks = (lane >= half, (lane >= 1) & (lane <= half))
    out = None
    for d in range(2):
        hd = jnp.dot(w3_ref[0, d], h2, precision=hp, preferred_element_type=F32) * decay
        hd = jnp.where(masks[d], hd, 0.0)
        hd = hd / (jnp.sum(jnp.abs(hd), axis=1, keepdims=True) + 1e-6)
        out = hd if out is None else out + hd
    g_ref[0] = out


def _filters(zt, w1t, b1, f1, w2t, b2, f2, w3t, dl):
    n = zt.shape[1]
    cb = 128
    full = lambda a: pl.BlockSpec(a.shape, lambda o, c: (0,) * a.ndim)
    return pl.pallas_call(
        _filter_kernel,
        grid=(HY_ORDER, HY_WIDTH // cb),
        in_specs=[full(zt), full(w1t), full(b1), full(f1), full(w2t), full(b2), full(f2),
                  pl.BlockSpec((1, 2, cb, HY_FILTER_HIDDEN), lambda o, c: (o, 0, c, 0)),
                  pl.BlockSpec((cb, 1), lambda o, c: (c, 0))],
        out_specs=pl.BlockSpec((1, cb, n), lambda o, c: (o, c, 0)),
        out_shape=jax.ShapeDtypeStruct((HY_ORDER, HY_WIDTH, n), F32),
        compiler_params=_cparams(("parallel", "parallel")),
        name="filters",
    )(zt, w1t, b1, f1, w2t, b2, f2, w3t, dl)


def _hyena_kernel(bias_ref, v_ref, x1_ref, x2_ref, g_ref, o_ref, acc_ref, *, batch, nblk, cb):
    rows = batch * nblk
    seq = nblk * TOEP
    c0 = pl.program_id(0) * cb
    gates = (x1_ref, x2_ref)

    def per_channel(c, carry):
        zf = v_ref[c].astype(F32)
        for o in range(HY_ORDER):
            zb = zf.astype(BF16)
            acc_ref[...] = jnp.zeros_like(acc_ref)
            for d in range(-(nblk - 1), nblk):
                base = seq + TOEP * d - TOEP
                win = g_ref[o, c, :, pl.ds(base, 2 * TOEP)]
                w = pltpu.roll(jnp.broadcast_to(win, (TOEP, 2 * TOEP)), TOEP, 1, stride=1, stride_axis=0)
                w = w[:, :TOEP].astype(BF16)
                n = rows - batch * abs(d)
                if d >= 0:
                    acc_ref[batch * d:, :] += _dot(zb[:n], w)
                else:
                    acc_ref[:n, :] += _dot(zb[batch * (-d):], w)
            zf = gates[o][c].astype(F32) * (acc_ref[...] + bias_ref[o, c0 + c] * zf)
        o_ref[c] = zf.astype(BF16)
        return carry

    lax.fori_loop(0, cb, per_channel, 0)


def _hyena(bias, ut, g4, batch, nblk, cb):
    rows = batch * nblk
    nc = HY_WIDTH // cb
    kern = functools.partial(_hyena_kernel, batch=batch, nblk=nblk, cb=cb)
    blk = lambda off: pl.BlockSpec((cb, rows, TOEP), lambda i: (off * nc + i, 0, 0))
    return pl.pallas_call(
        kern,
        grid=(nc,),
        in_specs=[pl.BlockSpec(memory_space=pltpu.SMEM), blk(0), blk(1), blk(2),
                  pl.BlockSpec((HY_ORDER, cb, 1, g4.shape[3]), lambda i: (0, i, 0, 0))],
        out_specs=pl.BlockSpec((cb, rows, TOEP), lambda i: (i, 0, 0)),
        out_shape=jax.ShapeDtypeStruct((HY_WIDTH, rows, TOEP), BF16),
        scratch_shapes=[pltpu.VMEM((rows, TOEP), F32)],
        compiler_params=_cparams(("parallel",)),
        name="hyena",
    )(bias, ut, ut, ut, g4)


def _mix_kernel(x_ref, lg_ref, lb_ref, a_ref, yt_ref, hg_ref, wo_ref, mg_ref, mb_ref, wq_ref,
                h2_ref, h2b_ref, qt_ref, cat0_ref, cat1_ref):
    n = pl.program_id(0)

    def gather_heads(cat_ref):
        y = yt_ref[...].astype(F32).T
        cat_ref[:, :ATTN_WIDTH] = a_ref[...]
        gw = HY_WIDTH // HY_GROUPS
        for g in range(HY_GROUPS):
            sl = slice(g * gw, (g + 1) * gw)
            cat_ref[:, ATTN_WIDTH + g * gw:ATTN_WIDTH + (g + 1) * gw] = _rms(y[:, sl], hg_ref[:, sl]).astype(BF16)

    def project(cat_ref):
        mix = _dot(cat_ref[...], wo_ref[...])
        h = _layer_norm(x_ref[...], lg_ref[...], lb_ref[...])
        h2 = _layer_norm(ALPHA * h + mix, mg_ref[...], mb_ref[...])
        h2_ref[...] = h2
        h2b = h2.astype(BF16)
        h2b_ref[...] = h2b
        qt_ref[...] = _dot_nt(wq_ref[...], h2b).astype(BF16)

    @pl.when(n == 0)
    def _():
        cat1_ref[...] = jnp.zeros_like(cat1_ref)

    @pl.when(n % 2 == 0)
    def _():
        gather_heads(cat0_ref)
        project(cat1_ref)

    @pl.when(n % 2 == 1)
    def _():
        gather_heads(cat1_ref)
        project(cat0_ref)


def _mix(x2, lg, lb, a, yt2, hg, wo, mg, mb, wqt, batch, nblk):
    t = x2.shape[0]
    nt = t // TOEP
    full = lambda w: pl.BlockSpec(w.shape, lambda i: (0,) * w.ndim)
    cur = lambda i: jnp.minimum(i, nt - 1)
    prev = lambda i: jnp.maximum(i - 1, 0)
    done = lambda w: pl.BlockSpec((TOEP, w), lambda i: (prev(i), 0))
    cat = pltpu.VMEM((TOEP, D_MODEL), BF16)
    return pl.pallas_call(
        _mix_kernel,
        grid=(nt + 1,),
        in_specs=[done(D_MODEL), full(lg), full(lb), pl.BlockSpec((TOEP, ATTN_WIDTH), lambda i: (cur(i), 0)),
                  pl.BlockSpec((HY_WIDTH, TOEP), lambda i: (0, (cur(i) % nblk) * batch + cur(i) // nblk)),
                  full(hg), full(wo), full(mg), full(mb), full(wqt)],
        out_specs=[done(D_MODEL), done(D_MODEL), pl.BlockSpec((wqt.shape[0], TOEP), lambda i: (0, prev(i)))],
        out_shape=[jax.ShapeDtypeStruct((t, D_MODEL), F32), jax.ShapeDtypeStruct((t, D_MODEL), BF16),
                   jax.ShapeDtypeStruct((wqt.shape[0], t), BF16)],
        scratch_shapes=[cat, cat],
        compiler_params=_cparams(("arbitrary",)),
        name="mix",
    )(x2, lg, lb, a, yt2, hg, wo, mg, mb, wqt)


N_KEYSETS = 2 * PEER_HEADS


SORT_GROUP = 4


def _top16(s, val_ref, idx_ref, lanes):
    neg = -jnp.inf
    sub = lax.broadcasted_iota(I32, (8, s.shape[1]), 0)
    ntile = s.shape[0] // 8
    groups = []
    for g0 in range(0, ntile, SORT_GROUP):
        t = [s[8 * v:8 * v + 8, :] for v in range(g0, g0 + SORT_GROUP)]
        ix = [sub + 8 * v for v in range(g0, g0 + SORT_GROUP)]
        for end in range(SORT_GROUP - 1, 0, -1):
            for p in range(end):
                swap = t[p + 1] > t[p]
                t[p], t[p + 1] = jnp.where(swap, t[p + 1], t[p]), jnp.where(swap, t[p], t[p + 1])
                ix[p], ix[p + 1] = jnp.where(swap, ix[p + 1], ix[p]), jnp.where(swap, ix[p], ix[p + 1])
        groups.append((t, ix))
    for r in range(PEER_TOPK):
        heads = [t[0] for t, _ in groups]
        while len(heads) > 1:
            heads = [jnp.maximum(heads[k], heads[k + 1]) for k in range(0, len(heads), 2)]
        m = jnp.max(heads[0], axis=0, keepdims=True)
        cand = [jnp.where(t[0] == m, ix[0], PEER_NKEYS) for t, ix in groups]
        while len(cand) > 1:
            cand = [jnp.minimum(cand[k], cand[k + 1]) for k in range(0, len(cand), 2)]
        idx = jnp.min(cand[0], axis=0, keepdims=True)
        val_ref[r:r + 1, lanes] = m
        idx_ref[r:r + 1, lanes] = idx
        for t, ix in groups:
            hit = ix[0] == idx
            for p in range(SORT_GROUP - 1):
                t[p] = jnp.where(hit, t[p + 1], t[p])
                ix[p] = jnp.where(hit, ix[p + 1], ix[p])
            t[-1] = jnp.where(hit, neg, t[-1])


_CODE_BITS = 14


def _pair_top16(v1_ref, i1_ref, v2_ref, i2_ref, best_ref, code_ref, lanes):
    neg = -jnp.inf
    width = v1_ref[0:8, lanes].shape[1]
    sub = lax.broadcasted_iota(I32, (8, width), 0)
    bc = lambda ref, r: jnp.broadcast_to(ref[r:r + 1, lanes], (8, width))

    def tiles(a_ref, b_ref, combine):
        b_lo = b_ref[0:8, lanes]
        a_hi = pltpu.roll(a_ref[8:16, lanes], 2, 0)
        out = [combine(bc(a_ref, 0), b_lo), combine(bc(a_ref, 0), b_ref[8:16, lanes]), combine(bc(a_ref, 1), b_lo)]
        out.append(jnp.where(sub < 5, combine(bc(a_ref, 2), b_lo), combine(bc(a_ref, 4), pltpu.roll(b_lo, 5, 0))))
        out.append(jnp.where(sub < 4, combine(bc(a_ref, 3), b_lo),
                             jnp.where(sub < 6, combine(bc(a_ref, 5), pltpu.roll(b_lo, 4, 0)),
                                       combine(bc(a_ref, 6), pltpu.roll(b_lo, 6, 0)))))
        out.append(jnp.where(sub < 2, combine(bc(a_ref, 7), b_lo), combine(a_hi, bc(b_ref, 0))))
        out.append(combine(a_hi, bc(b_ref, 0)))
        return out

    pos = [sub, sub + 8, sub + 16,
           jnp.where(sub < 5, sub + 32, sub + (64 - 5)),
           jnp.where(sub < 4, sub + 48, jnp.where(sub < 6, sub + (80 - 4), sub + (96 - 6))),
           jnp.where(sub < 2, sub + 112, (sub + 6) * 16),
           (sub + 14) * 16]
    cand = tiles(v1_ref, v2_ref, lambda a, b: a + b)
    cand[-1] = jnp.where(sub < 2, cand[-1], neg)
    code = tiles(i1_ref, i2_ref, lambda a, b: a * PEER_NKEYS + b)
    key = [p * (1 << _CODE_BITS) + c for p, c in zip(pos, code)]
    big = jnp.int32(1 << 30)

    def tree(xs, op):
        xs = list(xs)
        while len(xs) > 1:
            xs = [op(xs[k], xs[k + 1]) for k in range(0, len(xs) - 1, 2)] + ([xs[-1]] if len(xs) % 2 else [])
        return xs[0]

    for kk in range(PEER_TOPK):
        m = jnp.max(tree(cand, jnp.maximum), axis=0, keepdims=True)
        kmin = jnp.min(tree([jnp.where(c == m, k, big) for c, k in zip(cand, key)], jnp.minimum), axis=0, keepdims=True)
        best_ref[kk:kk + 1, lanes] = m
        code_ref[kk:kk + 1, lanes] = kmin & ((1 << _CODE_BITS) - 1)
        cand = [jnp.where(k == kmin, neg, c) for c, k in zip(cand, key)]


def _peer_up_kernel(x_ref, u_ref, q_ref, keys_ref, w_ref, isel_ref, jsel_ref,
                    a0_ref, a1_ref, v1_ref, i1_ref, v2_ref, i2_ref, best_ref, codeh_ref, codet_ref, gatet_ref,
                    iseln_ref, jseln_ref, gaten_ref, *, ne):
    i = pl.program_id(0)
    e = pl.program_id(1)
    tt = x_ref.shape[0]
    nsub = u_ref.shape[1] // PEER_NKEYS
    prev = (i + 1) % 2
    cur = i % 2
    chunks = [slice(c, c + LANES) for c in range(0, tt, LANES)]

    def pick(e_src, src, r_lo, r_hi):
        isel_p, jsel_p = iseln_ref.at[prev], jseln_ref.at[prev]
        for r0 in range(r_lo, r_hi, GATHER_ROWS):
            rows = slice(r0, r0 + GATHER_ROWS)
            isel = isel_p[rows, :]
            jsel = jsel_p[rows, :]
            acc = w_ref[rows, :]
            for ii in range(nsub):
                got = jnp.take_along_axis(src[rows, ii * PEER_NKEYS:(ii + 1) * PEER_NKEYS], jsel, axis=1)
                acc = jnp.where(isel == e_src * nsub + ii, got, acc)
            w_ref[rows, :] = acc
        return acc

    def step(dst, src, val_ref, idx_ref, pair):
        assert len(chunks) == 2 * STEP_SLICES
        s = _dot(keys_ref[0], q_ref[...])
        eb = u_ref.shape[1]
        never = e < 0
        bounds = [0] + [2 * c + 1 for c in range(STEP_SLICES)] + [len(chunks)]
        deps = ()
        for c in range(STEP_SLICES + 1):
            nxt = deps
            if c < STEP_SLICES:
                cols = slice(c * eb // STEP_SLICES, (c + 1) * eb // STEP_SLICES)
                res = _dot(x_ref[...], u_ref[:, cols])
                dst[:, cols] = res
                acc = pick(e - 1, src, c * tt // STEP_SLICES, (c + 1) * tt // STEP_SLICES)
                nxt = (res[:PEER_NKEYS, :LANES], jnp.broadcast_to(acc[0:1, :], (PEER_NKEYS, LANES)))
            for lanes in chunks[bounds[c]:bounds[c + 1]]:
                sc = s[:, lanes]
                for dep in deps:
                    sc = jnp.where(never, dep, sc)
                _top16(sc, val_ref, idx_ref, lanes)
                if pair:
                    _pair_top16(v1_ref, i1_ref, v2_ref, i2_ref, best_ref, codeh_ref, lanes)
            deps = nxt

    @pl.when((i == 0) & (e == 0))
    def _():
        a1_ref[...] = jnp.zeros_like(a1_ref)
        iseln_ref[...] = jnp.zeros_like(iseln_ref)
        jseln_ref[...] = jnp.zeros_like(jseln_ref)
        gaten_ref[...] = jnp.zeros_like(gaten_ref)

    @pl.when(e == 0)
    def _():
        w_ref[...] = jnp.zeros_like(w_ref)

    @pl.when((e < ne) & (e % 2 == 0))
    def _():
        step(a0_ref, a1_ref, v1_ref, i1_ref, False)

    @pl.when((e < ne) & (e % 2 == 1))
    def _():
        step(a1_ref, a0_ref, v2_ref, i2_ref, True)
        best = best_ref[...]
        ex = jnp.exp(best - best[0:1, :])
        row0 = pl.multiple_of((e // 2) * PEER_TOPK, PEER_TOPK)
        gatet_ref[pl.ds(row0, PEER_TOPK), :] = ex / jnp.sum(ex, axis=0, keepdims=True)
        codet_ref[pl.ds(row0, PEER_TOPK), :] = codeh_ref[...]

    @pl.when(e == ne)
    def _():
        pick(e - 1, a1_ref if ne % 2 == 0 else a0_ref, 0, tt)
        s = w_ref[...]
        gelu = 0.5 * s * (1.0 + lax.erf(s * (2.0 ** -0.5)))
        w_ref[...] = gaten_ref[prev] * gelu
        code_n = codet_ref[...].T
        isel = code_n >> 7
        jsel = code_n & (PEER_NKEYS - 1)
        isel_ref[...] = isel
        jsel_ref[...] = jsel
        iseln_ref[cur] = isel
        jseln_ref[cur] = jsel
        gaten_ref[cur] = gatet_ref[...].T


def _peer_up(h2b, ut, qt, keys, tt, eb):
    t = h2b.shape[0]
    ne = ut.shape[1] // eb
    assert ne == N_KEYSETS
    ntile = t // tt
    half = PEER_DK // 2
    tok = lambda i, e: (jnp.maximum(i - 1, 0), 0)
    rt = lambda i, e: (jnp.minimum(i, ntile - 1), 0)
    vm = lambda shape, dt: pltpu.VMEM(shape, dt)
    return pl.pallas_call(
        functools.partial(_peer_up_kernel, ne=ne),
        grid=(ntile + 1, ne + 1),
        in_specs=[pl.BlockSpec((tt, D_MODEL), tok),
                  pl.BlockSpec((D_MODEL, eb), lambda i, e: (0, jnp.minimum(e, ne - 1))),
                  pl.BlockSpec((half, tt), lambda i, e: (jnp.minimum(e, ne - 1), jnp.minimum(i, ntile - 1))),
                  pl.BlockSpec((1, PEER_NKEYS, half), lambda i, e: (jnp.minimum(e, ne - 1), 0, 0))],
        out_specs=[pl.BlockSpec((tt, N_SEL), tok), pl.BlockSpec((tt, N_SEL), rt), pl.BlockSpec((tt, N_SEL), rt)],
        out_shape=[jax.ShapeDtypeStruct((t, N_SEL), F32), jax.ShapeDtypeStruct((t, N_SEL), I32),
                   jax.ShapeDtypeStruct((t, N_SEL), I32)],
        scratch_shapes=[vm((tt, eb), F32), vm((tt, eb), F32),
                        vm((PEER_TOPK, tt), F32), vm((PEER_TOPK, tt), I32),
                        vm((PEER_TOPK, tt), F32), vm((PEER_TOPK, tt), I32),
                        vm((PEER_TOPK, tt), F32), vm((PEER_TOPK, tt), I32),
                        vm((N_SEL, tt), I32), vm((N_SEL, tt), F32),
                        vm((2, tt, N_SEL), I32), vm((2, tt, N_SEL), I32), vm((2, tt, N_SEL), F32)],
        compiler_params=_cparams(("arbitrary", "arbitrary")),
        name="peer_up",
    )(h2b, ut, qt, keys)


def _peer_down_kernel(isel_ref, jsel_ref, w_ref, v_ref, h2_ref, lg_ref, lb_ref, o_ref, wd0_ref, wd1_ref, acc_ref,
                      *, ib, ne):
    i = pl.program_id(0)
    e = pl.program_id(1)
    tt = isel_ref.shape[0]
    per_step = tt // ne
    half = PEER_NKEYS // 2
    hi_mask = jnp.uint32(0xFFFF0000)

    def scatter(dst, g0, dep):
        sub = lax.broadcasted_iota(I32, (PEER_NKEYS, N_SEL), 0)
        base = pl.multiple_of(e * per_step + g0, 8)
        isb = isel_ref[pl.ds(base, 8), :]
        if dep is not None:
            isb = jnp.where(i < 0, dep, isb)
        jsb = jsel_ref[pl.ds(base, 8), :]
        wb = w_ref[pl.ds(base, 8), :]
        zero = jnp.zeros((N_SEL, PEER_NKEYS), BF16)
        for r in range(0, 8, 2):
            pts, qs = [], []
            for rr in (r, r + 1):
                pts.append(jnp.where(sub == isb[rr:rr + 1, :], wb[rr:rr + 1, :], 0.0).astype(BF16))
                qs.append(jnp.where(sub == jsb[rr:rr + 1, :], 1.0, 0.0).T.astype(BF16))
            rhs = jnp.concatenate([jnp.concatenate([qs[0], zero], axis=1),
                                   jnp.concatenate([zero, qs[1]], axis=1)], axis=0)
            g = _dot(jnp.concatenate(pts, axis=1), rhs)
            for k, rr in enumerate((r, r + 1)):
                gb = pltpu.bitcast(g[:, k * PEER_NKEYS:(k + 1) * PEER_NKEYS], jnp.uint32)
                packed = (gb[half:] & hi_mask) | (gb[:half] >> 16)
                dst[pl.ds(pl.multiple_of((base + rr) * W_PITCH, 8), half), :] = packed

    def step(dst, src):
        parts = []
        for ii in range(ib):
            pk = src[pl.ds(e * ib + ii, tt, stride=W_PITCH), :]
            parts.append(pltpu.bitcast(pk << 16, F32).astype(BF16))
            parts.append(pltpu.bitcast(pk & hi_mask, F32).astype(BF16))
        lhs = jnp.concatenate(parts, axis=1)
        ngroup = per_step // 8
        width = acc_ref.shape[1] // ngroup
        dep = None
        for c in range(ngroup):
            cols = slice(c * width, (c + 1) * width)
            res = acc_ref[:, cols] + _dot(lhs, v_ref[:, cols])
            acc_ref[:, cols] = res
            scatter(dst, 8 * c, dep)
            dep = pltpu.bitcast(res[0:8, 0:LANES], I32)

    @pl.when((i == 0) & (e == 0))
    def _():
        wd1_ref[...] = jnp.zeros_like(wd1_ref)

    @pl.when(e == 0)
    def _():
        acc_ref[...] = jnp.zeros_like(acc_ref)

    @pl.when(i % 2 == 0)
    def _():
        step(wd0_ref, wd1_ref)

    @pl.when(i % 2 == 1)
    def _():
        step(wd1_ref, wd0_ref)

    @pl.when(e == ne - 1)
    def _():
        o_ref[...] = _layer_norm(ALPHA * h2_ref[...] + acc_ref[...], lg_ref[...], lb_ref[...])


def _peer_down(isel, jsel, w, vperm, h2, lg, lb, tt, ib):
    t = h2.shape[0]
    ne = (PEER_NKEYS // 2) // ib
    ntile = t // tt
    vb = ib * 2 * PEER_NKEYS
    assert tt % (8 * ne) == 0
    sel = pl.BlockSpec((tt, N_SEL), lambda i, e: (jnp.minimum(i, ntile - 1), 0))
    tok = pl.BlockSpec((tt, D_MODEL), lambda i, e: (jnp.maximum(i - 1, 0), 0))
    full = lambda a: pl.BlockSpec(a.shape, lambda i, e: (0,) * a.ndim)
    wd = pltpu.VMEM((tt * W_PITCH, N_SEL), jnp.uint32)
    return pl.pallas_call(
        functools.partial(_peer_down_kernel, ib=ib, ne=ne),
        grid=(ntile + 1, ne),
        in_specs=[sel, sel, sel, pl.BlockSpec((vb, D_MODEL), lambda i, e: (e, 0)), tok, full(lg), full(lb)],
        out_specs=tok,
        out_shape=jax.ShapeDtypeStruct((t, D_MODEL), F32),
        scratch_shapes=[wd, wd, pltpu.VMEM((tt, D_MODEL), F32)],
        compiler_params=_cparams(("arbitrary", "arbitrary")),
        name="peer_down",
    )(isel, jsel, w, vperm, h2, lg, lb)


def _rope_tables(seq):
    half = QK_ROPE_DIM // 2
    inv = 1.0 / (ROPE_THETA ** (jnp.arange(0, QK_ROPE_DIM, 2, dtype=F32) / QK_ROPE_DIM))
    ang = jnp.arange(seq, dtype=F32)[:, None] * inv[None, :]
    cos, sin = jnp.cos(ang), jnp.sin(ang)
    c32 = jnp.concatenate([cos, cos], axis=1)
    s32 = jnp.concatenate([-sin, sin], axis=1)
    z = lambda w: jnp.zeros((seq, w), F32)
    scale = (QK_NOPE_DIM + QK_ROPE_DIM) ** -0.5 * math.log2(math.e)
    pad = HEAD_SLOT - QK_NOPE_DIM - QK_ROPE_DIM
    cq = jnp.concatenate([jnp.full((seq, QK_NOPE_DIM), scale, F32), scale * c32, z(pad)], axis=1)
    sq = jnp.concatenate([z(QK_NOPE_DIM), scale * s32, z(pad)], axis=1)
    ck = jnp.concatenate([z(QK_NOPE_DIM), c32, z(pad)], axis=1)
    sk = jnp.concatenate([z(QK_NOPE_DIM), s32, z(pad)], axis=1)
    return cq, sq, ck, sk


def _swap_halves(w):
    half = w.shape[-1] // 2
    return jnp.concatenate([w[..., half:], w[..., :half]], axis=-1)


def _slot(w, offset):
    return jnp.pad(w, ((0, 0), (offset, HEAD_SLOT - offset - w.shape[1])))


def _position_features(seq):
    t = jnp.linspace(0.0, 1.0, seq, dtype=F32)[:, None]
    bands = (HY_EMB_DIM - 1) // 2
    w = 2.0 * math.pi * jnp.arange(seq, dtype=F32) / seq
    f = jnp.linspace(1e-4, bands - 1, bands, dtype=F32)
    ang = w[:, None] * f[None, :]
    return jnp.concatenate([t, jnp.cos(ang), -jnp.sin(ang)], axis=-1)


def kernel(x, emb_ln_g, emb_ln_b, w_in, q_norm_g, w_uq, kv_norm_g, w_ukv, hy_short_w, hy_short_b, hy_filt_w1,
           hy_filt_b1, hy_filt_freq1, hy_filt_w2, hy_filt_b2, hy_filt_freq2, hy_filt_w3, hy_bias, attn_out_g,
           hy_out_g, w_o, ln_mix_g, ln_mix_b, peer_wq, peer_sub_keys, peer_u, peer_v, ln_ffn_g, ln_ffn_b):
    batch, seq, _ = x.shape
    assert w_in.shape[0] == DEPTH == 1 and seq % TOEP == 0
    t = batch * seq
    nblk = seq // TOEP
    r2 = lambda a: a.reshape(1, -1)
    x2 = x.reshape(t, D_MODEL)
    lg, lb = r2(emb_ln_g), r2(emb_ln_b)

    wi = w_in[0]
    w_kr = wi[:, OFF_CKV:OFF_KR]
    wall = jnp.concatenate([wi[:, :OFF_CKV], _slot(w_kr, QK_NOPE_DIM), _slot(_swap_halves(w_kr), QK_NOPE_DIM),
                            wi[:, OFF_KR:]], axis=1).astype(BF16)
    dq = QK_NOPE_DIM + QK_ROPE_DIM
    wuq = w_uq[0].reshape(Q_LORA_RANK, ATTN_HEADS, dq)
    wq = jnp.pad(wuq, ((0, 0), (0, 0), (0, HEAD_SLOT - dq))).reshape(Q_LORA_RANK, -1).astype(BF16)
    wqs = jnp.pad(_swap_halves(wuq[..., QK_NOPE_DIM:]),
                  ((0, 0), (0, 0), (QK_NOPE_DIM, HEAD_SLOT - dq))).reshape(Q_LORA_RANK, -1).astype(BF16)
    wukv = w_ukv[0].reshape(KV_LORA_RANK, ATTN_HEADS, QK_NOPE_DIM + V_HEAD_DIM)
    wk = jnp.pad(wukv[..., :QK_NOPE_DIM],
                 ((0, 0), (0, 0), (0, HEAD_SLOT - QK_NOPE_DIM))).reshape(KV_LORA_RANK, -1).astype(BF16)
    wv = wukv[..., QK_NOPE_DIM:].reshape(KV_LORA_RANK, -1).T.astype(BF16)
    cq, sq, ck, sk = _rope_tables(seq)

    tt_proj = min(512, seq)
    q, k, vt, hy = _ln_proj(x2, lg, lb, wall, r2(q_norm_g[0]), wq, wqs, r2(kv_norm_g[0]), wk, wv,
                           cq, sq, ck, sk, seq, tt_proj)

    a = _attention(q, k, vt, attn_out_g[0].reshape(-1, 1), batch, seq, min(512, seq))

    ut = _short_conv(hy.reshape(batch, seq, -1), hy_short_w[0], r2(hy_short_b[0]))
    ut = ut.reshape((HY_ORDER + 1) * HY_WIDTH, nblk * batch, TOEP)

    lag = jnp.abs(jnp.arange(2 * seq) - seq)
    feats = _position_features(seq)
    zt = jnp.pad(feats[jnp.minimum(lag, seq - 1)].T, ((0, HY_FILTER_HIDDEN - HY_EMB_DIM), (0, 0)))
    col = lambda a_: a_.reshape(-1, 1)
    w1t = jnp.pad(hy_filt_w1[0].T, ((0, 0), (0, HY_FILTER_HIDDEN - HY_EMB_DIM)))
    w3t = hy_filt_w3[0].T.reshape(HY_ORDER, 2, HY_WIDTH, HY_FILTER_HIDDEN)
    deltas = jnp.abs(jnp.linspace(math.log(HY_TARGET) / HY_SLOW_DECAY, math.log(HY_TARGET) / HY_FAST_DECAY,
                                  HY_WIDTH, dtype=F32))
    g = _filters(zt, w1t, col(hy_filt_b1[0]), col(hy_filt_freq1[0]), hy_filt_w2[0].T, col(hy_filt_b2[0]),
                 col(hy_filt_freq2[0]), w3t, col(deltas))
    yt = _hyena(hy_bias[0], ut, g.reshape(HY_ORDER, HY_WIDTH, 1, 2 * seq), batch, nblk, 8)

    h2, h2b, qt = _mix(x2, lg, lb, a, yt.reshape(HY_WIDTH, nblk * batch * TOEP), r2(hy_out_g[0]),
                       w_o[0].astype(BF16), r2(ln_mix_g[0]), r2(ln_mix_b[0]), peer_wq[0].T.astype(BF16), batch, nblk)

    keys = peer_sub_keys[0].astype(BF16).reshape(N_KEYSETS, PEER_NKEYS, PEER_DK // 2)
    w, isel, jsel = _peer_up(h2b, peer_u[0].astype(BF16).T, qt, keys, min(1024, t), 1024)
    ib = 4
    half = PEER_NKEYS // 2
    vperm = peer_v[0].astype(BF16).reshape(2, half // ib, ib, PEER_NKEYS, D_MODEL)
    vperm = vperm.transpose(1, 2, 0, 3, 4).reshape(PEER_NKEYS * PEER_NKEYS, D_MODEL)
    out = _peer_down(isel, jsel, w, vperm, h2, r2(ln_ffn_g[0]), r2(ln_ffn_b[0]), min(512, t), ib)
    return out.reshape(batch, seq, D_MODEL)
```

```python
import functools
import math

import jax
import jax.numpy as jnp
from jax import lax
from jax.experimental import pallas as pl
from jax.experimental.pallas import tpu as pltpu

F32 = jnp.float32
BF16 = jnp.bfloat16
I32 = jnp.int32

D_MODEL = 1024
ATTN_HEADS = 8
QK_NOPE_DIM = 64
QK_ROPE_DIM = 32
V_HEAD_DIM = 64
Q_LORA_RANK = 256
KV_LORA_RANK = 128
ATTN_WIDTH = ATTN_HEADS * V_HEAD_DIM
ROPE_THETA = 10000.0
HY_WIDTH = D_MODEL - ATTN_WIDTH
HY_ORDER = 2
HY_GROUPS = 8
HY_SHORT = 3
HY_EMB_DIM = 33
HY_FILTER_HIDDEN = 64
HY_FAST_DECAY = 0.3
HY_SLOW_DECAY = 1.5
HY_TARGET = 1e-2
OFF_CQ = Q_LORA_RANK
OFF_CKV = OFF_CQ + KV_LORA_RANK
OFF_KR = OFF_CKV + QK_ROPE_DIM
PEER_HEADS = 8
PEER_NKEYS = 128
PEER_DK = 128
PEER_TOPK = 16
DEPTH = 1
ALPHA = (2 * DEPTH) ** 0.25
LN_EPS = 1e-5
RMS_EPS = 1e-6

LANES = 128
HEAD_SLOT = 128
TOEP = 256
N_SEL = PEER_HEADS * PEER_TOPK
W_PITCH = 72
GATHER_ROWS = 8
STEP_SLICES = 4
VMEM_LIMIT = 60 * 1024 * 1024


def _cparams(sem):
    return pltpu.CompilerParams(dimension_semantics=sem, vmem_limit_bytes=VMEM_LIMIT)


def _layer_norm(x, g, b):
    mu = jnp.mean(x, axis=-1, keepdims=True)
    xc = x - mu
    var = jnp.mean(xc * xc, axis=-1, keepdims=True)
    return xc * lax.rsqrt(var + LN_EPS) * g + b


def _rms(x, g):
    return x * lax.rsqrt(jnp.mean(x * x, axis=-1, keepdims=True) + RMS_EPS) * g


def _dot(a, b):
    return jnp.dot(a, b, preferred_element_type=F32)


def _dot_nt(a, b):
    return lax.dot_general(a, b, (((1,), (1,)), ((), ())), preferred_element_type=F32)


N_MLA_COLS = Q_LORA_RANK + KV_LORA_RANK + 2 * HEAD_SLOT


def _ln_proj_kernel(x_ref, xprev_ref, xnext_ref, lg_ref, lb_ref, wall_ref, gq_ref, wq_ref, wqs_ref, gkv_ref, wk_ref,
                    wv_ref, cw_ref, cb_ref, cq_ref, sq_ref, ck_ref, sk_ref, q_ref, k_ref, vt_ref, ut_ref, *, npos):
    j = pl.program_id(0) % npos
    h = _layer_norm(x_ref[...], lg_ref[...], lb_ref[...])
    proj = _dot(h.astype(BF16), wall_ref[...])
    c_q = proj[:, :OFF_CQ]
    c_kv = proj[:, OFF_CQ:OFF_CKV]
    kr = proj[:, OFF_CKV:OFF_CKV + HEAD_SLOT]
    kr_sw = proj[:, OFF_CKV + HEAD_SLOT:N_MLA_COLS]
    hy = proj[:, N_MLA_COLS:]

    nq = _rms(c_q, gq_ref[...]).astype(BF16)
    qa = _dot(nq, wq_ref[...])
    qb = _dot(nq, wqs_ref[...])
    nkv = _rms(c_kv, gkv_ref[...]).astype(BF16)
    kn = _dot(nkv, wk_ref[...])
    vt_ref[...] = _dot_nt(wv_ref[...], nkv).astype(BF16)
    k_pe = kr * ck_ref[...] + kr_sw * sk_ref[...]
    cq, sq = cq_ref[...], sq_ref[...]
    for hd in range(ATTN_HEADS):
        sl = slice(hd * HEAD_SLOT, (hd + 1) * HEAD_SLOT)
        q_ref[:, sl] = (qa[:, sl] * cq + qb[:, sl] * sq).astype(BF16)
        k_ref[:, sl] = (kn[:, sl] + k_pe).astype(BF16)

    halo = jnp.concatenate([xprev_ref[...], xnext_ref[...]], axis=0)
    hh = _dot(_layer_norm(halo, lg_ref[...], lb_ref[...]).astype(BF16), wall_ref[:, N_MLA_COLS:])
    rows = hy.shape[0]
    before = jnp.where(j > 0, hh[7:8, :], 0.0)
    after = jnp.where(j < npos - 1, hh[8:9, :], 0.0)
    rid = lax.broadcasted_iota(I32, hy.shape, 0)
    xm = jnp.where(rid == 0, before, pltpu.roll(hy, 1, 0))
    xp = jnp.where(rid == rows - 1, after, pltpu.roll(hy, rows - 1, 0))
    u = cb_ref[...] + xm * cw_ref[0:1, :] + hy * cw_ref[1:2, :] + xp * cw_ref[2:3, :]
    ut_ref[...] = u.T.astype(BF16)


def _ln_proj(x2, lg, lb, wall, gq, wq, wqs, gkv, wk, wv, cw, cb, cq, sq, ck, sk, batch, seq):
    t = x2.shape[0]
    tt = TOEP
    npos = seq // tt
    sub = tt // 8
    last8 = t // 8 - 1
    full = lambda a: pl.BlockSpec(a.shape, lambda i: (0,) * a.ndim)
    tab = pl.BlockSpec((tt, HEAD_SLOT), lambda i: (i % npos, 0))
    row = lambda w: pl.BlockSpec((tt, w), lambda i: (i, 0))
    hw = wall.shape[1] - N_MLA_COLS
    return pl.pallas_call(
        functools.partial(_ln_proj_kernel, npos=npos),
        grid=(t // tt,),
        in_specs=[row(D_MODEL),
                  pl.BlockSpec((8, D_MODEL), lambda i: (jnp.maximum(i * sub - 1, 0), 0)),
                  pl.BlockSpec((8, D_MODEL), lambda i: (jnp.minimum((i + 1) * sub, last8), 0)),
                  full(lg), full(lb), full(wall), full(gq), full(wq), full(wqs), full(gkv),
                  full(wk), full(wv), full(cw), full(cb), tab, tab, tab, tab],
        out_specs=[row(ATTN_HEADS * HEAD_SLOT), row(ATTN_HEADS * HEAD_SLOT),
                   pl.BlockSpec((ATTN_WIDTH, tt), lambda i: (0, i)),
                   pl.BlockSpec((hw, tt), lambda i: (0, (i % npos) * batch + i // npos))],
        out_shape=[jax.ShapeDtypeStruct((t, ATTN_HEADS * HEAD_SLOT), BF16),
                   jax.ShapeDtypeStruct((t, ATTN_HEADS * HEAD_SLOT), BF16),
                   jax.ShapeDtypeStruct((ATTN_WIDTH, t), BF16),
                   jax.ShapeDtypeStruct((hw, npos * batch * tt), BF16)],
        compiler_params=_cparams(("parallel",)),
        name="ln_proj",
    )(x2, x2, x2, lg, lb, wall, gq, wq, wqs, gkv, wk, wv, cw, cb, cq, sq, ck, sk)


def _attn_kernel(q_ref, k_ref, vt_ref, g_ref, o_ref, s0_ref, m0_ref, s1_ref, m1_ref, *, nsteps):
    n = pl.program_id(0)

    def scores(s_ref, m_ref):
        for hh in range(2):
            sl = slice(hh * HEAD_SLOT, (hh + 1) * HEAD_SLOT)
            s = _dot_nt(k_ref[:, sl], q_ref[:, sl])
            s_ref[hh] = s
            m_ref[hh] = jnp.max(s, axis=0, keepdims=True)

    def finish(s_ref, m_ref):
        outs = []
        ones = jnp.ones((16, vt_ref.shape[1]), BF16)
        for hh in range(2):
            vs = slice(hh * V_HEAD_DIM, (hh + 1) * V_HEAD_DIM)
            p = jnp.exp2(s_ref[hh] - m_ref[hh]).astype(BF16)
            ov = _dot(jnp.concatenate([vt_ref[vs, :], ones], axis=0), p)
            o = ov[:V_HEAD_DIM] / ov[V_HEAD_DIM:V_HEAD_DIM + 1]
            ms = jnp.mean(o * o, axis=0, keepdims=True)
            outs.append(o * lax.rsqrt(ms + RMS_EPS) * g_ref[vs, :])
        o_ref[...] = jnp.concatenate(outs, axis=0).T.astype(BF16)

    @pl.when(n == 0)
    def _():
        s1_ref[...] = jnp.zeros_like(s1_ref)
        m1_ref[...] = jnp.zeros_like(m1_ref)

    @pl.when((n < nsteps) & (n % 2 == 0))
    def _():
        finish(s1_ref, m1_ref)
        scores(s0_ref, m0_ref)

    @pl.when((n < nsteps) & (n % 2 == 1))
    def _():
        finish(s0_ref, m0_ref)
        scores(s1_ref, m1_ref)

    @pl.when(n == nsteps)
    def _():
        if nsteps % 2 == 0:
            finish(s1_ref, m1_ref)
        else:
            finish(s0_ref, m0_ref)


def _attention(q, k, vt, g, batch, seq, tq):
    t = q.shape[0]
    nq = seq // tq
    npair = ATTN_HEADS // 2
    nsteps = batch * npair * nq

    def tile(n):
        n = jnp.clip(n, 0, nsteps - 1)
        return n // (npair * nq), (n // nq) % npair, n % nq

    def q_map(n):
        b, p, i = tile(n)
        return b * nq + i, p

    def k_map(n):
        b, p, _ = tile(n)
        return b, p

    def vt_map(n):
        b, p, _ = tile(n - 1)
        return p, b

    def g_map(n):
        return tile(n - 1)[1], 0

    def o_map(n):
        b, p, i = tile(n - 1)
        return b * nq + i, p

    return pl.pallas_call(
        functools.partial(_attn_kernel, nsteps=nsteps),
        grid=(nsteps + 1,),
        in_specs=[pl.BlockSpec((tq, 2 * HEAD_SLOT), q_map),
                  pl.BlockSpec((seq, 2 * HEAD_SLOT), k_map),
                  pl.BlockSpec((2 * V_HEAD_DIM, seq), vt_map),
                  pl.BlockSpec((2 * V_HEAD_DIM, 1), g_map)],
        out_specs=pl.BlockSpec((tq, 2 * V_HEAD_DIM), o_map),
        out_shape=jax.ShapeDtypeStruct((t, ATTN_WIDTH), BF16),
        scratch_shapes=[pltpu.VMEM((2, seq, tq), F32), pltpu.VMEM((2, 1, tq), F32),
                        pltpu.VMEM((2, seq, tq), F32), pltpu.VMEM((2, 1, tq), F32)],
        compiler_params=_cparams(("arbitrary",)),
        name="attention",
    )(q, k, vt, g)


def _filter_kernel(z_ref, w1_ref, b1_ref, f1_ref, w2_ref, b2_ref, f2_ref, w3_ref, dl_ref, g_ref):
    hp = lax.Precision.HIGHEST
    z = z_ref[...]
    n = z.shape[1]
    half = n // 2
    h1 = jnp.sin(f1_ref[...] * (jnp.dot(w1_ref[...], z, precision=hp, preferred_element_type=F32) + b1_ref[...]))
    h2 = jnp.sin(f2_ref[...] * (jnp.dot(w2_ref[...], h1, precision=hp, preferred_element_type=F32) + b2_ref[...]))
    decay = jnp.exp(-z[0:1, :] * dl_ref[...])
    lane = lax.broadcasted_iota(I32, (1, n), 1)
    masks = (lane >= half, (lane >= 1) & (lane <= half))
    out = None
    for d in range(2):
        hd = jnp.dot(w3_ref[0, d], h2, precision=hp, preferred_element_type=F32) * decay
        hd = jnp.where(masks[d], hd, 0.0)
        hd = hd / (jnp.sum(jnp.abs(hd), axis=1, keepdims=True) + 1e-6)
        out = hd if out is None else out + hd
    g_ref[0] = out


def _filters(zt, w1t, b1, f1, w2t, b2, f2, w3t, dl):
    n = zt.shape[1]
    cb = 128
    full = lambda a: pl.BlockSpec(a.shape, lambda o, c: (0,) * a.ndim)
    return pl.pallas_call(
        _filter_kernel,
        grid=(HY_ORDER, HY_WIDTH // cb),
        in_specs=[full(zt), full(w1t), full(b1), full(f1), full(w2t), full(b2), full(f2),
                  pl.BlockSpec((1, 2, cb, HY_FILTER_HIDDEN), lambda o, c: (o, 0, c, 0)),
                  pl.BlockSpec((cb, 1), lambda o, c: (c, 0))],
        out_specs=pl.BlockSpec((1, cb, n), lambda o, c: (o, c, 0)),
        out_shape=jax.ShapeDtypeStruct((HY_ORDER, HY_WIDTH, n), F32),
        compiler_params=_cparams(("parallel", "parallel")),
        name="filters",
    )(zt, w1t, b1, f1, w2t, b2, f2, w3t, dl)


def _hyena_kernel(bias_ref, v_ref, x1_ref, x2_ref, g_ref, o_ref, acc_ref, *, batch, nblk, cb):
    rows = batch * nblk
    seq = nblk * TOEP
    c0 = pl.program_id(0) * cb
    gates = (x1_ref, x2_ref)

    def per_channel(c, carry):
        zf = v_ref[c].astype(F32)
        for o in range(HY_ORDER):
            zb = zf.astype(BF16)
            acc_ref[...] = jnp.zeros_like(acc_ref)
            for d in range(-(nblk - 1), nblk):
                base = seq + TOEP * d - TOEP
                win = g_ref[o, c, :, pl.ds(base, 2 * TOEP)]
                w = pltpu.roll(jnp.broadcast_to(win, (TOEP, 2 * TOEP)), TOEP, 1, stride=1, stride_axis=0)
                w = w[:, :TOEP].astype(BF16)
                n = rows - batch * abs(d)
                if d >= 0:
                    acc_ref[batch * d:, :] += _dot(zb[:n], w)
                else:
                    acc_ref[:n, :] += _dot(zb[batch * (-d):], w)
            zf = gates[o][c].astype(F32) * (acc_ref[...] + bias_ref[o, c0 + c] * zf)
        o_ref[c] = zf.astype(BF16)
        return carry

    lax.fori_loop(0, cb, per_channel, 0)


def _hyena(bias, ut, g4, batch, nblk, cb):
    rows = batch * nblk
    nc = HY_WIDTH // cb
    kern = functools.partial(_hyena_kernel, batch=batch, nblk=nblk, cb=cb)
    blk = lambda off: pl.BlockSpec((cb, rows, TOEP), lambda i: (off * nc + i, 0, 0))
    return pl.pallas_call(
        kern,
        grid=(nc,),
        in_specs=[pl.BlockSpec(memory_space=pltpu.SMEM), blk(0), blk(1), blk(2),
                  pl.BlockSpec((HY_ORDER, cb, 1, g4.shape[3]), lambda i: (0, i, 0, 0))],
        out_specs=pl.BlockSpec((cb, rows, TOEP), lambda i: (i, 0, 0)),
        out_shape=jax.ShapeDtypeStruct((HY_WIDTH, rows, TOEP), BF16),
        scratch_shapes=[pltpu.VMEM((rows, TOEP), F32)],
        compiler_params=_cparams(("parallel",)),
        name="hyena",
    )(bias, ut, ut, ut, g4)


def _mix_kernel(x_ref, lg_ref, lb_ref, a_ref, yt_ref, hg_ref, wo_ref, mg_ref, mb_ref, wq_ref,
                h2_ref, h2b_ref, qt_ref, cat0_ref, cat1_ref):
    n = pl.program_id(0)

    def gather_heads(cat_ref):
        y = yt_ref[...].astype(F32).T
        cat_ref[:, :ATTN_WIDTH] = a_ref[...]
        gw = HY_WIDTH // HY_GROUPS
        for g in range(HY_GROUPS):
            sl = slice(g * gw, (g + 1) * gw)
            cat_ref[:, ATTN_WIDTH + g * gw:ATTN_WIDTH + (g + 1) * gw] = _rms(y[:, sl], hg_ref[:, sl]).astype(BF16)

    def project(cat_ref):
        mix = _dot(cat_ref[...], wo_ref[...])
        h = _layer_norm(x_ref[...], lg_ref[...], lb_ref[...])
        h2 = _layer_norm(ALPHA * h + mix, mg_ref[...], mb_ref[...])
        h2_ref[...] = h2
        h2b = h2.astype(BF16)
        h2b_ref[...] = h2b
        qt_ref[...] = _dot_nt(wq_ref[...], h2b).astype(BF16)

    @pl.when(n == 0)
    def _():
        cat1_ref[...] = jnp.zeros_like(cat1_ref)

    @pl.when(n % 2 == 0)
    def _():
        gather_heads(cat0_ref)
        project(cat1_ref)

    @pl.when(n % 2 == 1)
    def _():
        gather_heads(cat1_ref)
        project(cat0_ref)


def _mix(x2, lg, lb, a, yt2, hg, wo, mg, mb, wqt, batch, nblk):
    t = x2.shape[0]
    nt = t // TOEP
    full = lambda w: pl.BlockSpec(w.shape, lambda i: (0,) * w.ndim)
    cur = lambda i: jnp.minimum(i, nt - 1)
    prev = lambda i: jnp.maximum(i - 1, 0)
    done = lambda w: pl.BlockSpec((TOEP, w), lambda i: (prev(i), 0))
    cat = pltpu.VMEM((TOEP, D_MODEL), BF16)
    return pl.pallas_call(
        _mix_kernel,
        grid=(nt + 1,),
        in_specs=[done(D_MODEL), full(lg), full(lb), pl.BlockSpec((TOEP, ATTN_WIDTH), lambda i: (cur(i), 0)),
                  pl.BlockSpec((HY_WIDTH, TOEP), lambda i: (0, (cur(i) % nblk) * batch + cur(i) // nblk)),
                  full(hg), full(wo), full(mg), full(mb), full(wqt)],
        out_specs=[done(D_MODEL), done(D_MODEL), pl.BlockSpec((wqt.shape[0], TOEP), lambda i: (0, prev(i)))],
        out_shape=[jax.ShapeDtypeStruct((t, D_MODEL), F32), jax.ShapeDtypeStruct((t, D_MODEL), BF16),
                   jax.ShapeDtypeStruct((wqt.shape[0], t), BF16)],
        scratch_shapes=[cat, cat],
        compiler_params=_cparams(("arbitrary",)),
        name="mix",
    )(x2, lg, lb, a, yt2, hg, wo, mg, mb, wqt)


N_KEYSETS = 2 * PEER_HEADS


SORT_GROUP = 4


def _top16(s, val_ref, idx_ref, lanes):
    neg = -jnp.inf
    sub = lax.broadcasted_iota(I32, (8, s.shape[1]), 0)
    ntile = s.shape[0] // 8
    groups = []
    for g0 in range(0, ntile, SORT_GROUP):
        t = [s[8 * v:8 * v + 8, :] for v in range(g0, g0 + SORT_GROUP)]
        ix = [sub + 8 * v for v in range(g0, g0 + SORT_GROUP)]
        for end in range(SORT_GROUP - 1, 0, -1):
            for p in range(end):
                swap = t[p + 1] > t[p]
                t[p], t[p + 1] = jnp.where(swap, t[p + 1], t[p]), jnp.where(swap, t[p], t[p + 1])
                ix[p], ix[p + 1] = jnp.where(swap, ix[p + 1], ix[p]), jnp.where(swap, ix[p], ix[p + 1])
        groups.append((t, ix))
    for r in range(PEER_TOPK):
        heads = [t[0] for t, _ in groups]
        while len(heads) > 1:
            heads = [jnp.maximum(heads[k], heads[k + 1]) for k in range(0, len(heads), 2)]
        m = jnp.max(heads[0], axis=0, keepdims=True)
        cand = [jnp.where(t[0] == m, ix[0], PEER_NKEYS) for t, ix in groups]
        while len(cand) > 1:
            cand = [jnp.minimum(cand[k], cand[k + 1]) for k in range(0, len(cand), 2)]
        idx = jnp.min(cand[0], axis=0, keepdims=True)
        val_ref[r:r + 1, lanes] = m
        idx_ref[r:r + 1, lanes] = idx
        for t, ix in groups:
            hit = ix[0] == idx
            for p in range(SORT_GROUP - 1):
                t[p] = jnp.where(hit, t[p + 1], t[p])
                ix[p] = jnp.where(hit, ix[p + 1], ix[p])
            t[-1] = jnp.where(hit, neg, t[-1])


_CODE_BITS = 14


def _pair_top16(v1_ref, i1_ref, v2_ref, i2_ref, best_ref, code_ref, lanes):
    neg = -jnp.inf
    width = v1_ref[0:8, lanes].shape[1]
    sub = lax.broadcasted_iota(I32, (8, width), 0)
    bc = lambda ref, r: jnp.broadcast_to(ref[r:r + 1, lanes], (8, width))

    def tiles(a_ref, b_ref, combine):
        b_lo = b_ref[0:8, lanes]
        a_hi = pltpu.roll(a_ref[8:16, lanes], 2, 0)
        out = [combine(bc(a_ref, 0), b_lo), combine(bc(a_ref, 0), b_ref[8:16, lanes]), combine(bc(a_ref, 1), b_lo)]
        out.append(jnp.where(sub < 5, combine(bc(a_ref, 2), b_lo), combine(bc(a_ref, 4), pltpu.roll(b_lo, 5, 0))))
        out.append(jnp.where(sub < 4, combine(bc(a_ref, 3), b_lo),
                             jnp.where(sub < 6, combine(bc(a_ref, 5), pltpu.roll(b_lo, 4, 0)),
                                       combine(bc(a_ref, 6), pltpu.roll(b_lo, 6, 0)))))
        out.append(jnp.where(sub < 2, combine(bc(a_ref, 7), b_lo), combine(a_hi, bc(b_ref, 0))))
        out.append(combine(a_hi, bc(b_ref, 0)))
        return out

    pos = [sub, sub + 8, sub + 16,
           jnp.where(sub < 5, sub + 32, sub + (64 - 5)),
           jnp.where(sub < 4, sub + 48, jnp.where(sub < 6, sub + (80 - 4), sub + (96 - 6))),
           jnp.where(sub < 2, sub + 112, (sub + 6) * 16),
           (sub + 14) * 16]
    cand = tiles(v1_ref, v2_ref, lambda a, b: a + b)
    cand[-1] = jnp.where(sub < 2, cand[-1], neg)
    code = tiles(i1_ref, i2_ref, lambda a, b: a * PEER_NKEYS + b)
    key = [p * (1 << _CODE_BITS) + c for p, c in zip(pos, code)]
    big = jnp.int32(1 << 30)

    def tree(xs, op):
        xs = list(xs)
        while len(xs) > 1:
            xs = [op(xs[k], xs[k + 1]) for k in range(0, len(xs) - 1, 2)] + ([xs[-1]] if len(xs) % 2 else [])
        return xs[0]

    for kk in range(PEER_TOPK):
        m = jnp.max(tree(cand, jnp.maximum), axis=0, keepdims=True)
        kmin = jnp.min(tree([jnp.where(c == m, k, big) for c, k in zip(cand, key)], jnp.minimum), axis=0, keepdims=True)
        best_ref[kk:kk + 1, lanes] = m
        code_ref[kk:kk + 1, lanes] = kmin & ((1 << _CODE_BITS) - 1)
        cand = [jnp.where(k == kmin, neg, c) for c, k in zip(cand, key)]


def _peer_up_kernel(x_ref, u_ref, q_ref, keys_ref, w_ref, isel_ref, jsel_ref,
                    a0_ref, a1_ref, v1_ref, i1_ref, v2_ref, i2_ref, best_ref, codeh_ref, codet_ref, gatet_ref,
                    iseln_ref, jseln_ref, gaten_ref, *, ne):
    i = pl.program_id(0)
    e = pl.program_id(1)
    tt = x_ref.shape[0]
    nsub = u_ref.shape[0] // PEER_NKEYS
    prev = (i + 1) % 2
    cur = i % 2
    chunks = [slice(c, c + LANES) for c in range(0, tt, LANES)]

    def pick(e_src, src, r_lo, r_hi):
        isel_p, jsel_p = iseln_ref.at[prev], jseln_ref.at[prev]
        for r0 in range(r_lo, r_hi, GATHER_ROWS):
            rows = slice(r0, r0 + GATHER_ROWS)
            isel = isel_p[rows, :]
            jsel = jsel_p[rows, :]
            acc = w_ref[rows, :]
            for ii in range(nsub):
                got = jnp.take_along_axis(src[rows, ii * PEER_NKEYS:(ii + 1) * PEER_NKEYS], jsel, axis=1)
                acc = jnp.where(isel == e_src * nsub + ii, got, acc)
            w_ref[rows, :] = acc
        return acc

    def step(dst, src, val_ref, idx_ref, pair):
        assert len(chunks) == 2 * STEP_SLICES
        s = _dot(keys_ref[0], q_ref[...])
        eb = u_ref.shape[0]
        never = e < 0
        bounds = [0] + [2 * c + 1 for c in range(STEP_SLICES)] + [len(chunks)]
        deps = ()
        for c in range(STEP_SLICES + 1):
            nxt = deps
            if c < STEP_SLICES:
                cols = slice(c * eb // STEP_SLICES, (c + 1) * eb // STEP_SLICES)
                res = _dot_nt(x_ref[...], u_ref[cols, :])
                dst[:, cols] = res
                acc = pick(e - 1, src, c * tt // STEP_SLICES, (c + 1) * tt // STEP_SLICES)
                nxt = (res[:PEER_NKEYS, :LANES], jnp.broadcast_to(acc[0:1, :], (PEER_NKEYS, LANES)))
            for lanes in chunks[bounds[c]:bounds[c + 1]]:
                sc = s[:, lanes]
                for dep in deps:
                    sc = jnp.where(never, dep, sc)
                _top16(sc, val_ref, idx_ref, lanes)
                if pair:
                    _pair_top16(v1_ref, i1_ref, v2_ref, i2_ref, best_ref, codeh_ref, lanes)
            deps = nxt

    @pl.when((i == 0) & (e == 0))
    def _():
        a1_ref[...] = jnp.zeros_like(a1_ref)
        iseln_ref[...] = jnp.zeros_like(iseln_ref)
        jseln_ref[...] = jnp.zeros_like(jseln_ref)
        gaten_ref[...] = jnp.zeros_like(gaten_ref)

    @pl.when(e == 0)
    def _():
        w_ref[...] = jnp.zeros_like(w_ref)

    @pl.when((e < ne) & (e % 2 == 0))
    def _():
        step(a0_ref, a1_ref, v1_ref, i1_ref, False)

    @pl.when((e < ne) & (e % 2 == 1))
    def _():
        step(a1_ref, a0_ref, v2_ref, i2_ref, True)
        best = best_ref[...]
        ex = jnp.exp(best - best[0:1, :])
        row0 = pl.multiple_of((e // 2) * PEER_TOPK, PEER_TOPK)
        gatet_ref[pl.ds(row0, PEER_TOPK), :] = ex / jnp.sum(ex, axis=0, keepdims=True)
        codet_ref[pl.ds(row0, PEER_TOPK), :] = codeh_ref[...]

    @pl.when(e == ne)
    def _():
        pick(e - 1, a1_ref if ne % 2 == 0 else a0_ref, 0, tt)
        s = w_ref[...]
        gelu = 0.5 * s * (1.0 + lax.erf(s * (2.0 ** -0.5)))
        w_ref[...] = gaten_ref[prev] * gelu
        code_n = codet_ref[...].T
        isel = code_n >> 7
        jsel = code_n & (PEER_NKEYS - 1)
        isel_ref[...] = isel
        jsel_ref[...] = jsel
        iseln_ref[cur] = isel
        jseln_ref[cur] = jsel
        gaten_ref[cur] = gatet_ref[...].T


def _peer_up(h2b, ut, qt, keys, tt, eb):
    t = h2b.shape[0]
    ne = ut.shape[0] // eb
    assert ne == N_KEYSETS
    ntile = t // tt
    half = PEER_DK // 2
    tok = lambda i, e: (jnp.maximum(i - 1, 0), 0)
    rt = lambda i, e: (jnp.minimum(i, ntile - 1), 0)
    vm = lambda shape, dt: pltpu.VMEM(shape, dt)
    return pl.pallas_call(
        functools.partial(_peer_up_kernel, ne=ne),
        grid=(ntile + 1, ne + 1),
        in_specs=[pl.BlockSpec((tt, D_MODEL), tok),
                  pl.BlockSpec((eb, D_MODEL), lambda i, e: (jnp.minimum(e, ne - 1), 0)),
                  pl.BlockSpec((half, tt), lambda i, e: (jnp.minimum(e, ne - 1), jnp.minimum(i, ntile - 1))),
                  pl.BlockSpec((1, PEER_NKEYS, half), lambda i, e: (jnp.minimum(e, ne - 1), 0, 0))],
        out_specs=[pl.BlockSpec((tt, N_SEL), tok), pl.BlockSpec((tt, N_SEL), rt), pl.BlockSpec((tt, N_SEL), rt)],
        out_shape=[jax.ShapeDtypeStruct((t, N_SEL), F32), jax.ShapeDtypeStruct((t, N_SEL), I32),
                   jax.ShapeDtypeStruct((t, N_SEL), I32)],
        scratch_shapes=[vm((tt, eb), F32), vm((tt, eb), F32),
                        vm((PEER_TOPK, tt), F32), vm((PEER_TOPK, tt), I32),
                        vm((PEER_TOPK, tt), F32), vm((PEER_TOPK, tt), I32),
                        vm((PEER_TOPK, tt), F32), vm((PEER_TOPK, tt), I32),
                        vm((N_SEL, tt), I32), vm((N_SEL, tt), F32),
                        vm((2, tt, N_SEL), I32), vm((2, tt, N_SEL), I32), vm((2, tt, N_SEL), F32)],
        compiler_params=_cparams(("arbitrary", "arbitrary")),
        name="peer_up",
    )(h2b, ut, qt, keys)


def _peer_down_kernel(isel_ref, jsel_ref, w_ref, v_ref, h2_ref, lg_ref, lb_ref, o_ref, wd0_ref, wd1_ref, acc_ref,
                      *, ib, ne):
    i = pl.program_id(0)
    e = pl.program_id(1)
    tt = isel_ref.shape[0]
    per_step = tt // ne
    half = PEER_NKEYS // 2
    hi_mask = jnp.uint32(0xFFFF0000)

    def scatter(dst, g0, dep):
        sub = lax.broadcasted_iota(I32, (PEER_NKEYS, N_SEL), 0)
        base = pl.multiple_of(e * per_step + g0, 8)
        isb = isel_ref[pl.ds(base, 8), :]
        if dep is not None:
            isb = jnp.where(i < 0, dep, isb)
        jsb = jsel_ref[pl.ds(base, 8), :]
        wb = w_ref[pl.ds(base, 8), :]
        zero = jnp.zeros((N_SEL, PEER_NKEYS), BF16)
        for r in range(0, 8, 2):
            pts, qs = [], []
            for rr in (r, r + 1):
                pts.append(jnp.where(sub == isb[rr:rr + 1, :], wb[rr:rr + 1, :], 0.0).astype(BF16))
                qs.append(jnp.where(sub == jsb[rr:rr + 1, :], 1.0, 0.0).T.astype(BF16))
            rhs = jnp.concatenate([jnp.concatenate([qs[0], zero], axis=1),
                                   jnp.concatenate([zero, qs[1]], axis=1)], axis=0)
            g = _dot(jnp.concatenate(pts, axis=1), rhs)
            for k, rr in enumerate((r, r + 1)):
                gb = pltpu.bitcast(g[:, k * PEER_NKEYS:(k + 1) * PEER_NKEYS], jnp.uint32)
                packed = (gb[half:] & hi_mask) | (gb[:half] >> 16)
                dst[pl.ds(pl.multiple_of((base + rr) * W_PITCH, 8), half), :] = packed

    def step(dst, src):
        parts = []
        for ii in range(ib):
            pk = src[pl.ds(e * ib + ii, tt, stride=W_PITCH), :]
            parts.append(pltpu.bitcast(pk << 16, F32).astype(BF16))
            parts.append(pltpu.bitcast(pk & hi_mask, F32).astype(BF16))
        lhs = jnp.concatenate(parts, axis=1)
        nslice = acc_ref.shape[1] // TOEP
        per_slice = per_step // nslice
        dep = None
        for c in range(nslice):
            cols = slice(c * TOEP, (c + 1) * TOEP)
            res = acc_ref[:, cols] + _dot(lhs, v_ref[:, cols])
            acc_ref[:, cols] = res
            for g0 in range(c * per_slice, (c + 1) * per_slice, 8):
                scatter(dst, g0, dep)
            dep = pltpu.bitcast(res[0:8, 0:LANES], I32)

    @pl.when((i == 0) & (e == 0))
    def _():
        wd1_ref[...] = jnp.zeros_like(wd1_ref)

    @pl.when(e == 0)
    def _():
        acc_ref[...] = jnp.zeros_like(acc_ref)

    @pl.when(i % 2 == 0)
    def _():
        step(wd0_ref, wd1_ref)

    @pl.when(i % 2 == 1)
    def _():
        step(wd1_ref, wd0_ref)

    @pl.when(e == ne - 1)
    def _():
        o_ref[...] = _layer_norm(ALPHA * h2_ref[...] + acc_ref[...], lg_ref[...], lb_ref[...])


def _peer_down(isel, jsel, w, vperm, h2, lg, lb, tt, ib):
    t = h2.shape[0]
    ne = (PEER_NKEYS // 2) // ib
    ntile = t // tt
    vb = ib * 2 * PEER_NKEYS
    assert tt % (8 * ne) == 0
    sel = pl.BlockSpec((tt, N_SEL), lambda i, e: (jnp.minimum(i, ntile - 1), 0))
    tok = pl.BlockSpec((tt, D_MODEL), lambda i, e: (jnp.maximum(i - 1, 0), 0))
    full = lambda a: pl.BlockSpec(a.shape, lambda i, e: (0,) * a.ndim)
    wd = pltpu.VMEM((tt * W_PITCH, N_SEL), jnp.uint32)
    return pl.pallas_call(
        functools.partial(_peer_down_kernel, ib=ib, ne=ne),
        grid=(ntile + 1, ne),
        in_specs=[sel, sel, sel, pl.BlockSpec((vb, D_MODEL), lambda i, e: (e, 0)), tok, full(lg), full(lb)],
        out_specs=tok,
        out_shape=jax.ShapeDtypeStruct((t, D_MODEL), F32),
        scratch_shapes=[wd, wd, pltpu.VMEM((tt, D_MODEL), F32)],
        compiler_params=_cparams(("arbitrary", "arbitrary")),
        name="peer_down",
    )(isel, jsel, w, vperm, h2, lg, lb)


def _rope_tables(seq):
    half = QK_ROPE_DIM // 2
    inv = 1.0 / (ROPE_THETA ** (jnp.arange(0, QK_ROPE_DIM, 2, dtype=F32) / QK_ROPE_DIM))
    ang = jnp.arange(seq, dtype=F32)[:, None] * inv[None, :]
    cos, sin = jnp.cos(ang), jnp.sin(ang)
    c32 = jnp.concatenate([cos, cos], axis=1)
    s32 = jnp.concatenate([-sin, sin], axis=1)
    z = lambda w: jnp.zeros((seq, w), F32)
    scale = (QK_NOPE_DIM + QK_ROPE_DIM) ** -0.5 * math.log2(math.e)
    pad = HEAD_SLOT - QK_NOPE_DIM - QK_ROPE_DIM
    cq = jnp.concatenate([jnp.full((seq, QK_NOPE_DIM), scale, F32), scale * c32, z(pad)], axis=1)
    sq = jnp.concatenate([z(QK_NOPE_DIM), scale * s32, z(pad)], axis=1)
    ck = jnp.concatenate([z(QK_NOPE_DIM), c32, z(pad)], axis=1)
    sk = jnp.concatenate([z(QK_NOPE_DIM), s32, z(pad)], axis=1)
    return cq, sq, ck, sk


def _swap_halves(w):
    half = w.shape[-1] // 2
    return jnp.concatenate([w[..., half:], w[..., :half]], axis=-1)


def _slot(w, offset):
    return jnp.pad(w, ((0, 0), (offset, HEAD_SLOT - offset - w.shape[1])))


def _position_features(seq):
    t = jnp.linspace(0.0, 1.0, seq, dtype=F32)[:, None]
    bands = (HY_EMB_DIM - 1) // 2
    w = 2.0 * math.pi * jnp.arange(seq, dtype=F32) / seq
    f = jnp.linspace(1e-4, bands - 1, bands, dtype=F32)
    ang = w[:, None] * f[None, :]
    return jnp.concatenate([t, jnp.cos(ang), -jnp.sin(ang)], axis=-1)


def kernel(x, emb_ln_g, emb_ln_b, w_in, q_norm_g, w_uq, kv_norm_g, w_ukv, hy_short_w, hy_short_b, hy_filt_w1,
           hy_filt_b1, hy_filt_freq1, hy_filt_w2, hy_filt_b2, hy_filt_freq2, hy_filt_w3, hy_bias, attn_out_g,
           hy_out_g, w_o, ln_mix_g, ln_mix_b, peer_wq, peer_sub_keys, peer_u, peer_v, ln_ffn_g, ln_ffn_b):
    batch, seq, _ = x.shape
    assert w_in.shape[0] == DEPTH == 1 and seq % TOEP == 0
    t = batch * seq
    nblk = seq // TOEP
    r2 = lambda a: a.reshape(1, -1)
    x2 = x.reshape(t, D_MODEL)
    lg, lb = r2(emb_ln_g), r2(emb_ln_b)

    wi = w_in[0]
    w_kr = wi[:, OFF_CKV:OFF_KR]
    wall = jnp.concatenate([wi[:, :OFF_CKV], _slot(w_kr, QK_NOPE_DIM), _slot(_swap_halves(w_kr), QK_NOPE_DIM),
                            wi[:, OFF_KR:]], axis=1).astype(BF16)
    dq = QK_NOPE_DIM + QK_ROPE_DIM
    wuq = w_uq[0].reshape(Q_LORA_RANK, ATTN_HEADS, dq)
    wq = jnp.pad(wuq, ((0, 0), (0, 0), (0, HEAD_SLOT - dq))).reshape(Q_LORA_RANK, -1).astype(BF16)
    wqs = jnp.pad(_swap_halves(wuq[..., QK_NOPE_DIM:]),
                  ((0, 0), (0, 0), (QK_NOPE_DIM, HEAD_SLOT - dq))).reshape(Q_LORA_RANK, -1).astype(BF16)
    wukv = w_ukv[0].reshape(KV_LORA_RANK, ATTN_HEADS, QK_NOPE_DIM + V_HEAD_DIM)
    wk = jnp.pad(wukv[..., :QK_NOPE_DIM],
                 ((0, 0), (0, 0), (0, HEAD_SLOT - QK_NOPE_DIM))).reshape(KV_LORA_RANK, -1).astype(BF16)
    wv = wukv[..., QK_NOPE_DIM:].reshape(KV_LORA_RANK, -1).T.astype(BF16)
    cq, sq, ck, sk = _rope_tables(seq)

    q, k, vt, ut = _ln_proj(x2, lg, lb, wall, r2(q_norm_g[0]), wq, wqs, r2(kv_norm_g[0]), wk, wv,
                           hy_short_w[0], r2(hy_short_b[0]), cq, sq, ck, sk, batch, seq)

    a = _attention(q, k, vt, attn_out_g[0].reshape(-1, 1), batch, seq, min(512, seq))

    ut = ut.reshape((HY_ORDER + 1) * HY_WIDTH, nblk * batch, TOEP)

    lag = jnp.abs(jnp.arange(2 * seq) - seq)
    feats = _position_features(seq)
    zt = jnp.pad(feats[jnp.minimum(lag, seq - 1)].T, ((0, HY_FILTER_HIDDEN - HY_EMB_DIM), (0, 0)))
    col = lambda a_: a_.reshape(-1, 1)
    w1t = jnp.pad(hy_filt_w1[0].T, ((0, 0), (0, HY_FILTER_HIDDEN - HY_EMB_DIM)))
    w3t = hy_filt_w3[0].T.reshape(HY_ORDER, 2, HY_WIDTH, HY_FILTER_HIDDEN)
    deltas = jnp.abs(jnp.linspace(math.log(HY_TARGET) / HY_SLOW_DECAY, math.log(HY_TARGET) / HY_FAST_DECAY,
                                  HY_WIDTH, dtype=F32))
    g = _filters(zt, w1t, col(hy_filt_b1[0]), col(hy_filt_freq1[0]), hy_filt_w2[0].T, col(hy_filt_b2[0]),
                 col(hy_filt_freq2[0]), w3t, col(deltas))
    yt = _hyena(hy_bias[0], ut, g.reshape(HY_ORDER, HY_WIDTH, 1, 2 * seq), batch, nblk, 8)

    h2, h2b, qt = _mix(x2, lg, lb, a, yt.reshape(HY_WIDTH, nblk * batch * TOEP), r2(hy_out_g[0]),
                       w_o[0].astype(BF16), r2(ln_mix_g[0]), r2(ln_mix_b[0]), peer_wq[0].T.astype(BF16), batch, nblk)

    keys = peer_sub_keys[0].astype(BF16).reshape(N_KEYSETS, PEER_NKEYS, PEER_DK // 2)
    w, isel, jsel = _peer_up(h2b, peer_u[0].astype(BF16), qt, keys, min(1024, t), 1024)
    ib = 8
    half = PEER_NKEYS // 2
    vperm = peer_v[0].astype(BF16).reshape(2, half // ib, ib, PEER_NKEYS, D_MODEL)
    vperm = vperm.transpose(1, 2, 0, 3, 4).reshape(PEER_NKEYS * PEER_NKEYS, D_MODEL)
    out = _peer_down(isel, jsel, w, vperm, h2, r2(ln_ffn_g[0]), r2(ln_ffn_b[0]), min(512, t), ib)
    return out.reshape(batch, seq, D_MODEL)
```

```python
import functools
import math

import jax
import jax.numpy as jnp
from jax import lax
from jax.experimental import pallas as pl
from jax.experimental.pallas import tpu as pltpu

F32 = jnp.float32
BF16 = jnp.bfloat16
I32 = jnp.int32

D_MODEL = 1024
ATTN_HEADS = 8
QK_NOPE_DIM = 64
QK_ROPE_DIM = 32
V_HEAD_DIM = 64
Q_LORA_RANK = 256
KV_LORA_RANK = 128
ATTN_WIDTH = ATTN_HEADS * V_HEAD_DIM
ROPE_THETA = 10000.0
HY_WIDTH = D_MODEL - ATTN_WIDTH
HY_ORDER = 2
HY_GROUPS = 8
HY_SHORT = 3
HY_EMB_DIM = 33
HY_FILTER_HIDDEN = 64
HY_FAST_DECAY = 0.3
HY_SLOW_DECAY = 1.5
HY_TARGET = 1e-2
OFF_CQ = Q_LORA_RANK
OFF_CKV = OFF_CQ + KV_LORA_RANK
OFF_KR = OFF_CKV + QK_ROPE_DIM
PEER_HEADS = 8
PEER_NKEYS = 128
PEER_DK = 128
PEER_TOPK = 16
DEPTH = 1
ALPHA = (2 * DEPTH) ** 0.25
LN_EPS = 1e-5
RMS_EPS = 1e-6

LANES = 128
HEAD_SLOT = 128
TOEP = 256
N_SEL = PEER_HEADS * PEER_TOPK
W_PITCH = 72
GATHER_ROWS = 8
STEP_SLICES = 4
VMEM_LIMIT = 60 * 1024 * 1024


def _cparams(sem):
    return pltpu.CompilerParams(dimension_semantics=sem, vmem_limit_bytes=VMEM_LIMIT)


def _layer_norm(x, g, b):
    mu = jnp.mean(x, axis=-1, keepdims=True)
    xc = x - mu
    var = jnp.mean(xc * xc, axis=-1, keepdims=True)
    return xc * lax.rsqrt(var + LN_EPS) * g + b


def _rms(x, g):
    return x * lax.rsqrt(jnp.mean(x * x, axis=-1, keepdims=True) + RMS_EPS) * g


def _dot(a, b):
    return jnp.dot(a, b, preferred_element_type=F32)


def _dot_nt(a, b):
    return lax.dot_general(a, b, (((1,), (1,)), ((), ())), preferred_element_type=F32)


N_MLA_COLS = Q_LORA_RANK + KV_LORA_RANK + 2 * HEAD_SLOT


def _ln_proj_kernel(x_ref, xprev_ref, xnext_ref, lg_ref, lb_ref, wall_ref, gq_ref, wq_ref, wqs_ref, gkv_ref, wk_ref,
                    wv_ref, cw_ref, cb_ref, cq_ref, sq_ref, ck_ref, sk_ref, q_ref, k_ref, vt_ref, ut_ref, *, npos):
    j = pl.program_id(0) % npos
    h = _layer_norm(x_ref[...], lg_ref[...], lb_ref[...])
    proj = _dot(h.astype(BF16), wall_ref[...])
    c_q = proj[:, :OFF_CQ]
    c_kv = proj[:, OFF_CQ:OFF_CKV]
    kr = proj[:, OFF_CKV:OFF_CKV + HEAD_SLOT]
    kr_sw = proj[:, OFF_CKV + HEAD_SLOT:N_MLA_COLS]
    hy = proj[:, N_MLA_COLS:]

    nq = _rms(c_q, gq_ref[...]).astype(BF16)
    qa = _dot(nq, wq_ref[...])
    qb = _dot(nq, wqs_ref[...])
    nkv = _rms(c_kv, gkv_ref[...]).astype(BF16)
    kn = _dot(nkv, wk_ref[...])
    vt_ref[...] = _dot_nt(wv_ref[...], nkv).astype(BF16)
    k_pe = kr * ck_ref[...] + kr_sw * sk_ref[...]
    cq, sq = cq_ref[...], sq_ref[...]
    for hd in range(ATTN_HEADS):
        sl = slice(hd * HEAD_SLOT, (hd + 1) * HEAD_SLOT)
        q_ref[:, sl] = (qa[:, sl] * cq + qb[:, sl] * sq).astype(BF16)
        k_ref[:, sl] = (kn[:, sl] + k_pe).astype(BF16)

    halo = jnp.concatenate([xprev_ref[...], xnext_ref[...]], axis=0)
    hh = _dot(_layer_norm(halo, lg_ref[...], lb_ref[...]).astype(BF16), wall_ref[:, N_MLA_COLS:])
    rows = hy.shape[0]
    before = jnp.where(j > 0, hh[7:8, :], 0.0)
    after = jnp.where(j < npos - 1, hh[8:9, :], 0.0)
    rid = lax.broadcasted_iota(I32, hy.shape, 0)
    xm = jnp.where(rid == 0, before, pltpu.roll(hy, 1, 0))
    xp = jnp.where(rid == rows - 1, after, pltpu.roll(hy, rows - 1, 0))
    u = cb_ref[...] + xm * cw_ref[0:1, :] + hy * cw_ref[1:2, :] + xp * cw_ref[2:3, :]
    ut_ref[...] = u.T.astype(BF16)


def _ln_proj(x2, lg, lb, wall, gq, wq, wqs, gkv, wk, wv, cw, cb, cq, sq, ck, sk, batch, seq):
    t = x2.shape[0]
    tt = TOEP
    npos = seq // tt
    sub = tt // 8
    last8 = t // 8 - 1
    full = lambda a: pl.BlockSpec(a.shape, lambda i: (0,) * a.ndim)
    tab = pl.BlockSpec((tt, HEAD_SLOT), lambda i: (i % npos, 0))
    row = lambda w: pl.BlockSpec((tt, w), lambda i: (i, 0))
    hw = wall.shape[1] - N_MLA_COLS
    return pl.pallas_call(
        functools.partial(_ln_proj_kernel, npos=npos),
        grid=(t // tt,),
        in_specs=[row(D_MODEL),
                  pl.BlockSpec((8, D_MODEL), lambda i: (jnp.maximum(i * sub - 1, 0), 0)),
                  pl.BlockSpec((8, D_MODEL), lambda i: (jnp.minimum((i + 1) * sub, last8), 0)),
                  full(lg), full(lb), full(wall), full(gq), full(wq), full(wqs), full(gkv),
                  full(wk), full(wv), full(cw), full(cb), tab, tab, tab, tab],
        out_specs=[row(ATTN_HEADS * HEAD_SLOT), row(ATTN_HEADS * HEAD_SLOT),
                   pl.BlockSpec((ATTN_WIDTH, tt), lambda i: (0, i)),
                   pl.BlockSpec((hw, tt), lambda i: (0, (i % npos) * batch + i // npos))],
        out_shape=[jax.ShapeDtypeStruct((t, ATTN_HEADS * HEAD_SLOT), BF16),
                   jax.ShapeDtypeStruct((t, ATTN_HEADS * HEAD_SLOT), BF16),
                   jax.ShapeDtypeStruct((ATTN_WIDTH, t), BF16),
                   jax.ShapeDtypeStruct((hw, npos * batch * tt), BF16)],
        compiler_params=_cparams(("parallel",)),
        name="ln_proj",
    )(x2, x2, x2, lg, lb, wall, gq, wq, wqs, gkv, wk, wv, cw, cb, cq, sq, ck, sk)


def _attn_kernel(q_ref, k_ref, vt_ref, g_ref, o_ref, s0_ref, m0_ref, s1_ref, m1_ref, *, nsteps):
    n = pl.program_id(0)

    def scores(s_ref, m_ref):
        for hh in range(2):
            sl = slice(hh * HEAD_SLOT, (hh + 1) * HEAD_SLOT)
            s = _dot_nt(k_ref[:, sl], q_ref[:, sl])
            s_ref[hh] = s
            m_ref[hh] = jnp.max(s, axis=0, keepdims=True)

    def finish(s_ref, m_ref):
        outs = []
        ones = jnp.ones((16, vt_ref.shape[1]), BF16)
        for hh in range(2):
            vs = slice(hh * V_HEAD_DIM, (hh + 1) * V_HEAD_DIM)
            p = jnp.exp2(s_ref[hh] - m_ref[hh]).astype(BF16)
            ov = _dot(jnp.concatenate([vt_ref[vs, :], ones], axis=0), p)
            o = ov[:V_HEAD_DIM] / ov[V_HEAD_DIM:V_HEAD_DIM + 1]
            ms = jnp.mean(o * o, axis=0, keepdims=True)
            outs.append(o * lax.rsqrt(ms + RMS_EPS) * g_ref[vs, :])
        o_ref[...] = jnp.concatenate(outs, axis=0).T.astype(BF16)

    @pl.when(n == 0)
    def _():
        s1_ref[...] = jnp.zeros_like(s1_ref)
        m1_ref[...] = jnp.zeros_like(m1_ref)

    @pl.when((n < nsteps) & (n % 2 == 0))
    def _():
        finish(s1_ref, m1_ref)
        scores(s0_ref, m0_ref)

    @pl.when((n < nsteps) & (n % 2 == 1))
    def _():
        finish(s0_ref, m0_ref)
        scores(s1_ref, m1_ref)

    @pl.when(n == nsteps)
    def _():
        if nsteps % 2 == 0:
            finish(s1_ref, m1_ref)
        else:
            finish(s0_ref, m0_ref)


def _attention(q, k, vt, g, batch, seq, tq):
    t = q.shape[0]
    nq = seq // tq
    npair = ATTN_HEADS // 2
    nsteps = batch * npair * nq

    def tile(n):
        n = jnp.clip(n, 0, nsteps - 1)
        return n // (npair * nq), (n // nq) % npair, n % nq

    def q_map(n):
        b, p, i = tile(n)
        return b * nq + i, p

    def k_map(n):
        b, p, _ = tile(n)
        return b, p

    def vt_map(n):
        b, p, _ = tile(n - 1)
        return p, b

    def g_map(n):
        return tile(n - 1)[1], 0

    def o_map(n):
        b, p, i = tile(n - 1)
        return b * nq + i, p

    return pl.pallas_call(
        functools.partial(_attn_kernel, nsteps=nsteps),
        grid=(nsteps + 1,),
        in_specs=[pl.BlockSpec((tq, 2 * HEAD_SLOT), q_map),
                  pl.BlockSpec((seq, 2 * HEAD_SLOT), k_map),
                  pl.BlockSpec((2 * V_HEAD_DIM, seq), vt_map),
                  pl.BlockSpec((2 * V_HEAD_DIM, 1), g_map)],
        out_specs=pl.BlockSpec((tq, 2 * V_HEAD_DIM), o_map),
        out_shape=jax.ShapeDtypeStruct((t, ATTN_WIDTH), BF16),
        scratch_shapes=[pltpu.VMEM((2, seq, tq), F32), pltpu.VMEM((2, 1, tq), F32),
                        pltpu.VMEM((2, seq, tq), F32), pltpu.VMEM((2, 1, tq), F32)],
        compiler_params=_cparams(("arbitrary",)),
        name="attention",
    )(q, k, vt, g)


def _filter_kernel(z_ref, w1_ref, b1_ref, f1_ref, w2_ref, b2_ref, f2_ref, w3_ref, dl_ref, g_ref):
    hp = lax.Precision.HIGHEST
    z = z_ref[...]
    n = z.shape[1]
    half = n // 2
    h1 = jnp.sin(f1_ref[...] * (jnp.dot(w1_ref[...], z, precision=hp, preferred_element_type=F32) + b1_ref[...]))
    h2 = jnp.sin(f2_ref[...] * (jnp.dot(w2_ref[...], h1, precision=hp, preferred_element_type=F32) + b2_ref[...]))
    decay = jnp.exp(-z[0:1, :] * dl_ref[...])
    lane = lax.broadcasted_iota(I32, (1, n), 1)
    masks = (lane >= half, (lane >= 1) & (lane <= half))
    out = None
    for d in range(2):
        hd = jnp.dot(w3_ref[0, d], h2, precision=hp, preferred_element_type=F32) * decay
        hd = jnp.where(masks[d], hd, 0.0)
        hd = hd / (jnp.sum(jnp.abs(hd), axis=1, keepdims=True) + 1e-6)
        out = hd if out is None else out + hd
    g_ref[0] = out


def _filters(zt, w1t, b1, f1, w2t, b2, f2, w3t, dl):
    n = zt.shape[1]
    cb = 128
    full = lambda a: pl.BlockSpec(a.shape, lambda o, c: (0,) * a.ndim)
    return pl.pallas_call(
        _filter_kernel,
        grid=(HY_ORDER, HY_WIDTH // cb),
        in_specs=[full(zt), full(w1t), full(b1), full(f1), full(w2t), full(b2), full(f2),
                  pl.BlockSpec((1, 2, cb, HY_FILTER_HIDDEN), lambda o, c: (o, 0, c, 0)),
                  pl.BlockSpec((cb, 1), lambda o, c: (c, 0))],
        out_specs=pl.BlockSpec((1, cb, n), lambda o, c: (o, c, 0)),
        out_shape=jax.ShapeDtypeStruct((HY_ORDER, HY_WIDTH, n), F32),
        compiler_params=_cparams(("parallel", "parallel")),
        name="filters",
    )(zt, w1t, b1, f1, w2t, b2, f2, w3t, dl)


def _hyena_kernel(bias_ref, v_ref, x1_ref, x2_ref, g_ref, o_ref, acc_ref, *, batch, nblk, cb):
    rows = batch * nblk
    seq = nblk * TOEP
    c0 = pl.program_id(0) * cb
    gates = (x1_ref, x2_ref)

    def per_channel(c, carry):
        zf = v_ref[c].astype(F32)
        for o in range(HY_ORDER):
            zb = zf.astype(BF16)
            acc_ref[...] = jnp.zeros_like(acc_ref)
            for d in range(-(nblk - 1), nblk):
                base = seq + TOEP * d - TOEP
                win = g_ref[o, c, :, pl.ds(base, 2 * TOEP)]
                w = pltpu.roll(jnp.broadcast_to(win, (TOEP, 2 * TOEP)), TOEP, 1, stride=1, stride_axis=0)
                w = w[:, :TOEP].astype(BF16)
                n = rows - batch * abs(d)
                if d >= 0:
                    acc_ref[batch * d:, :] += _dot(zb[:n], w)
                else:
                    acc_ref[:n, :] += _dot(zb[batch * (-d):], w)
            zf = gates[o][c].astype(F32) * (acc_ref[...] + bias_ref[o, c0 + c] * zf)
        o_ref[c] = zf.astype(BF16)
        return carry

    lax.fori_loop(0, cb, per_channel, 0)


def _hyena(bias, ut, g4, batch, nblk, cb):
    rows = batch * nblk
    nc = HY_WIDTH // cb
    kern = functools.partial(_hyena_kernel, batch=batch, nblk=nblk, cb=cb)
    blk = lambda off: pl.BlockSpec((cb, rows, TOEP), lambda i: (off * nc + i, 0, 0))
    return pl.pallas_call(
        kern,
        grid=(nc,),
        in_specs=[pl.BlockSpec(memory_space=pltpu.SMEM), blk(0), blk(1), blk(2),
                  pl.BlockSpec((HY_ORDER, cb, 1, g4.shape[3]), lambda i: (0, i, 0, 0))],
        out_specs=pl.BlockSpec((cb, rows, TOEP), lambda i: (i, 0, 0)),
        out_shape=jax.ShapeDtypeStruct((HY_WIDTH, rows, TOEP), BF16),
        scratch_shapes=[pltpu.VMEM((rows, TOEP), F32)],
        compiler_params=_cparams(("parallel",)),
        name="hyena",
    )(bias, ut, ut, ut, g4)


def _mix_kernel(x_ref, lg_ref, lb_ref, a_ref, yt_ref, hg_ref, wo_ref, mg_ref, mb_ref, wq_ref,
                h2_ref, h2b_ref, qt_ref, cat0_ref, cat1_ref):
    n = pl.program_id(0)

    def gather_heads(cat_ref):
        y = yt_ref[...].astype(F32).T
        cat_ref[:, :ATTN_WIDTH] = a_ref[...]
        gw = HY_WIDTH // HY_GROUPS
        for g in range(HY_GROUPS):
            sl = slice(g * gw, (g + 1) * gw)
            cat_ref[:, ATTN_WIDTH + g * gw:ATTN_WIDTH + (g + 1) * gw] = _rms(y[:, sl], hg_ref[:, sl]).astype(BF16)

    def project(cat_ref):
        mix = _dot(cat_ref[...], wo_ref[...])
        h = _layer_norm(x_ref[...], lg_ref[...], lb_ref[...])
        h2 = _layer_norm(ALPHA * h + mix, mg_ref[...], mb_ref[...])
        h2_ref[...] = h2
        h2b = h2.astype(BF16)
        h2b_ref[...] = h2b
        qt_ref[...] = _dot_nt(wq_ref[...], h2b).astype(BF16)

    @pl.when(n == 0)
    def _():
        cat1_ref[...] = jnp.zeros_like(cat1_ref)

    @pl.when(n % 2 == 0)
    def _():
        gather_heads(cat0_ref)
        project(cat1_ref)

    @pl.when(n % 2 == 1)
    def _():
        gather_heads(cat1_ref)
        project(cat0_ref)


def _mix(x2, lg, lb, a, yt2, hg, wo, mg, mb, wqt, batch, nblk):
    t = x2.shape[0]
    nt = t // TOEP
    full = lambda w: pl.BlockSpec(w.shape, lambda i: (0,) * w.ndim)
    cur = lambda i: jnp.minimum(i, nt - 1)
    prev = lambda i: jnp.maximum(i - 1, 0)
    done = lambda w: pl.BlockSpec((TOEP, w), lambda i: (prev(i), 0))
    cat = pltpu.VMEM((TOEP, D_MODEL), BF16)
    return pl.pallas_call(
        _mix_kernel,
        grid=(nt + 1,),
        in_specs=[done(D_MODEL), full(lg), full(lb), pl.BlockSpec((TOEP, ATTN_WIDTH), lambda i: (cur(i), 0)),
                  pl.BlockSpec((HY_WIDTH, TOEP), lambda i: (0, (cur(i) % nblk) * batch + cur(i) // nblk)),
                  full(hg), full(wo), full(mg), full(mb), full(wqt)],
        out_specs=[done(D_MODEL), done(D_MODEL), pl.BlockSpec((wqt.shape[0], TOEP), lambda i: (0, prev(i)))],
        out_shape=[jax.ShapeDtypeStruct((t, D_MODEL), F32), jax.ShapeDtypeStruct((t, D_MODEL), BF16),
                   jax.ShapeDtypeStruct((wqt.shape[0], t), BF16)],
        scratch_shapes=[cat, cat],
        compiler_params=_cparams(("arbitrary",)),
        name="mix",
    )(x2, lg, lb, a, yt2, hg, wo, mg, mb, wqt)


N_KEYSETS = 2 * PEER_HEADS


SORT_GROUP = 4


def _top16(s, val_ref, idx_ref, lanes):
    neg = -jnp.inf
    sub = lax.broadcasted_iota(I32, (8, s.shape[1]), 0)
    ntile = s.shape[0] // 8
    groups = []
    for g0 in range(0, ntile, SORT_GROUP):
        t = [s[8 * v:8 * v + 8, :] for v in range(g0, g0 + SORT_GROUP)]
        ix = [sub + 8 * v for v in range(g0, g0 + SORT_GROUP)]
        for end in range(SORT_GROUP - 1, 0, -1):
            for p in range(end):
                swap = t[p + 1] > t[p]
                t[p], t[p + 1] = jnp.where(swap, t[p + 1], t[p]), jnp.where(swap, t[p], t[p + 1])
                ix[p], ix[p + 1] = jnp.where(swap, ix[p + 1], ix[p]), jnp.where(swap, ix[p], ix[p + 1])
        groups.append((t, ix))
    for r in range(PEER_TOPK):
        heads = [t[0] for t, _ in groups]
        while len(heads) > 1:
            heads = [jnp.maximum(heads[k], heads[k + 1]) for k in range(0, len(heads), 2)]
        m = jnp.max(heads[0], axis=0, keepdims=True)
        cand = [jnp.where(t[0] == m, ix[0], PEER_NKEYS) for t, ix in groups]
        while len(cand) > 1:
            cand = [jnp.minimum(cand[k], cand[k + 1]) for k in range(0, len(cand), 2)]
        idx = jnp.min(cand[0], axis=0, keepdims=True)
        val_ref[r:r + 1, lanes] = m
        idx_ref[r:r + 1, lanes] = idx
        for t, ix in groups:
            hit = ix[0] == idx
            for p in range(SORT_GROUP - 1):
                t[p] = jnp.where(hit, t[p + 1], t[p])
                ix[p] = jnp.where(hit, ix[p + 1], ix[p])
            t[-1] = jnp.where(hit, neg, t[-1])


_CODE_BITS = 14


def _pair_top16(v1_ref, i1_ref, v2_ref, i2_ref, best_ref, code_ref, lanes):
    neg = -jnp.inf
    width = v1_ref[0:8, lanes].shape[1]
    sub = lax.broadcasted_iota(I32, (8, width), 0)
    bc = lambda ref, r: jnp.broadcast_to(ref[r:r + 1, lanes], (8, width))

    def tiles(a_ref, b_ref, combine):
        b_lo = b_ref[0:8, lanes]
        a_hi = pltpu.roll(a_ref[8:16, lanes], 2, 0)
        out = [combine(bc(a_ref, 0), b_lo), combine(bc(a_ref, 0), b_ref[8:16, lanes]), combine(bc(a_ref, 1), b_lo)]
        out.append(jnp.where(sub < 5, combine(bc(a_ref, 2), b_lo), combine(bc(a_ref, 4), pltpu.roll(b_lo, 5, 0))))
        out.append(jnp.where(sub < 4, combine(bc(a_ref, 3), b_lo),
                             jnp.where(sub < 6, combine(bc(a_ref, 5), pltpu.roll(b_lo, 4, 0)),
                                       combine(bc(a_ref, 6), pltpu.roll(b_lo, 6, 0)))))
        out.append(jnp.where(sub < 2, combine(bc(a_ref, 7), b_lo), combine(a_hi, bc(b_ref, 0))))
        out.append(combine(a_hi, bc(b_ref, 0)))
        return out

    pos = [sub, sub + 8, sub + 16,
           jnp.where(sub < 5, sub + 32, sub + (64 - 5)),
           jnp.where(sub < 4, sub + 48, jnp.where(sub < 6, sub + (80 - 4), sub + (96 - 6))),
           jnp.where(sub < 2, sub + 112, (sub + 6) * 16),
           (sub + 14) * 16]
    cand = tiles(v1_ref, v2_ref, lambda a, b: a + b)
    cand[-1] = jnp.where(sub < 2, cand[-1], neg)
    code = tiles(i1_ref, i2_ref, lambda a, b: a * PEER_NKEYS + b)
    key = [p * (1 << _CODE_BITS) + c for p, c in zip(pos, code)]
    big = jnp.int32(1 << 30)

    def tree(xs, op):
        xs = list(xs)
        while len(xs) > 1:
            xs = [op(xs[k], xs[k + 1]) for k in range(0, len(xs) - 1, 2)] + ([xs[-1]] if len(xs) % 2 else [])
        return xs[0]

    for kk in range(PEER_TOPK):
        m = jnp.max(tree(cand, jnp.maximum), axis=0, keepdims=True)
        kmin = jnp.min(tree([jnp.where(c == m, k, big) for c, k in zip(cand, key)], jnp.minimum), axis=0, keepdims=True)
        best_ref[kk:kk + 1, lanes] = m
        code_ref[kk:kk + 1, lanes] = kmin & ((1 << _CODE_BITS) - 1)
        cand = [jnp.where(k == kmin, neg, c) for c, k in zip(cand, key)]


def _peer_up_kernel(x_ref, u_ref, q_ref, keys_ref, w_ref, isel_ref, jsel_ref,
                    a0_ref, a1_ref, v1_ref, i1_ref, v2_ref, i2_ref, best_ref, codeh_ref, codet_ref, gatet_ref,
                    iseln_ref, jseln_ref, gaten_ref, *, ne):
    i = pl.program_id(0)
    e = pl.program_id(1)
    tt = x_ref.shape[0]
    nsub = u_ref.shape[0] // PEER_NKEYS
    prev = (i + 1) % 2
    cur = i % 2
    chunks = [slice(c, c + LANES) for c in range(0, tt, LANES)]

    def pick(e_src, src, r_lo, r_hi):
        isel_p, jsel_p = iseln_ref.at[prev], jseln_ref.at[prev]
        for r0 in range(r_lo, r_hi, GATHER_ROWS):
            rows = slice(r0, r0 + GATHER_ROWS)
            isel = isel_p[rows, :]
            jsel = jsel_p[rows, :]
            acc = w_ref[rows, :]
            for ii in range(nsub):
                got = jnp.take_along_axis(src[rows, ii * PEER_NKEYS:(ii + 1) * PEER_NKEYS], jsel, axis=1)
                acc = jnp.where(isel == e_src * nsub + ii, got, acc)
            w_ref[rows, :] = acc
        return acc

    def step(dst, src, val_ref, idx_ref, pair):
        assert len(chunks) == 2 * STEP_SLICES
        s = _dot(keys_ref[0], q_ref[...])
        eb = u_ref.shape[0]
        never = e < 0
        bounds = [0] + [2 * c + 1 for c in range(STEP_SLICES)] + [len(chunks)]
        deps = ()
        for c in range(STEP_SLICES + 1):
            nxt = deps
            if c < STEP_SLICES:
                cols = slice(c * eb // STEP_SLICES, (c + 1) * eb // STEP_SLICES)
                res = _dot_nt(x_ref[...], u_ref[cols, :])
                dst[:, cols] = res
                acc = pick(e - 1, src, c * tt // STEP_SLICES, (c + 1) * tt // STEP_SLICES)
                nxt = (res[:PEER_NKEYS, :LANES], jnp.broadcast_to(acc[0:1, :], (PEER_NKEYS, LANES)))
            for lanes in chunks[bounds[c]:bounds[c + 1]]:
                sc = s[:, lanes]
                for dep in deps:
                    sc = jnp.where(never, dep, sc)
                _top16(sc, val_ref, idx_ref, lanes)
                if pair:
                    _pair_top16(v1_ref, i1_ref, v2_ref, i2_ref, best_ref, codeh_ref, lanes)
            deps = nxt

    @pl.when((i == 0) & (e == 0))
    def _():
        a1_ref[...] = jnp.zeros_like(a1_ref)
        iseln_ref[...] = jnp.zeros_like(iseln_ref)
        jseln_ref[...] = jnp.zeros_like(jseln_ref)
        gaten_ref[...] = jnp.zeros_like(gaten_ref)

    @pl.when(e == 0)
    def _():
        w_ref[...] = jnp.zeros_like(w_ref)

    @pl.when((e < ne) & (e % 2 == 0))
    def _():
        step(a0_ref, a1_ref, v1_ref, i1_ref, False)

    @pl.when((e < ne) & (e % 2 == 1))
    def _():
        step(a1_ref, a0_ref, v2_ref, i2_ref, True)
        best = best_ref[...]
        ex = jnp.exp(best - best[0:1, :])
        row0 = pl.multiple_of((e // 2) * PEER_TOPK, PEER_TOPK)
        gatet_ref[pl.ds(row0, PEER_TOPK), :] = ex / jnp.sum(ex, axis=0, keepdims=True)
        codet_ref[pl.ds(row0, PEER_TOPK), :] = codeh_ref[...]

    @pl.when(e == ne)
    def _():
        pick(e - 1, a1_ref if ne % 2 == 0 else a0_ref, 0, tt)
        s = w_ref[...]
        gelu = 0.5 * s * (1.0 + lax.erf(s * (2.0 ** -0.5)))
        w_ref[...] = gaten_ref[prev] * gelu
        code_n = codet_ref[...].T
        isel = code_n >> 7
        jsel = code_n & (PEER_NKEYS - 1)
        isel_ref[...] = isel
        jsel_ref[...] = jsel
        iseln_ref[cur] = isel
        jseln_ref[cur] = jsel
        gaten_ref[cur] = gatet_ref[...].T


def _peer_up(h2b, ut, qt, keys, tt, eb):
    t = h2b.shape[0]
    ne = ut.shape[0] // eb
    assert ne == N_KEYSETS
    ntile = t // tt
    half = PEER_DK // 2
    tok = lambda i, e: (jnp.maximum(i - 1, 0), 0)
    rt = lambda i, e: (jnp.minimum(i, ntile - 1), 0)
    vm = lambda shape, dt: pltpu.VMEM(shape, dt)
    return pl.pallas_call(
        functools.partial(_peer_up_kernel, ne=ne),
        grid=(ntile + 1, ne + 1),
        in_specs=[pl.BlockSpec((tt, D_MODEL), tok),
                  pl.BlockSpec((eb, D_MODEL), lambda i, e: (jnp.minimum(e, ne - 1), 0)),
                  pl.BlockSpec((half, tt), lambda i, e: (jnp.minimum(e, ne - 1), jnp.minimum(i, ntile - 1))),
                  pl.BlockSpec((1, PEER_NKEYS, half), lambda i, e: (jnp.minimum(e, ne - 1), 0, 0))],
        out_specs=[pl.BlockSpec((tt, N_SEL), tok), pl.BlockSpec((tt, N_SEL), rt), pl.BlockSpec((tt, N_SEL), rt)],
        out_shape=[jax.ShapeDtypeStruct((t, N_SEL), F32), jax.ShapeDtypeStruct((t, N_SEL), I32),
                   jax.ShapeDtypeStruct((t, N_SEL), I32)],
        scratch_shapes=[vm((tt, eb), F32), vm((tt, eb), F32),
                        vm((PEER_TOPK, tt), F32), vm((PEER_TOPK, tt), I32),
                        vm((PEER_TOPK, tt), F32), vm((PEER_TOPK, tt), I32),
                        vm((PEER_TOPK, tt), F32), vm((PEER_TOPK, tt), I32),
                        vm((N_SEL, tt), I32), vm((N_SEL, tt), F32),
                        vm((2, tt, N_SEL), I32), vm((2, tt, N_SEL), I32), vm((2, tt, N_SEL), F32)],
        compiler_params=_cparams(("arbitrary", "arbitrary")),
        name="peer_up",
    )(h2b, ut, qt, keys)


def _peer_down_kernel(isel_ref, jsel_ref, w_ref, v_ref, h2_ref, lg_ref, lb_ref, o_ref, wd0_ref, wd1_ref, acc_ref,
                      *, ib, ne):
    i = pl.program_id(0)
    e = pl.program_id(1)
    tt = isel_ref.shape[0]
    per_step = tt // ne
    half = PEER_NKEYS // 2
    hi_mask = jnp.uint32(0xFFFF0000)

    def scatter(dst, g0, dep):
        sub = lax.broadcasted_iota(I32, (PEER_NKEYS, N_SEL), 0)
        base = pl.multiple_of(e * per_step + g0, 8)
        isb = isel_ref[pl.ds(base, 8), :]
        if dep is not None:
            isb = jnp.where(i < 0, dep, isb)
        jsb = jsel_ref[pl.ds(base, 8), :]
        wb = w_ref[pl.ds(base, 8), :]
        zero = jnp.zeros((N_SEL, PEER_NKEYS), BF16)
        for r in range(0, 8, 2):
            pts, qs = [], []
            for rr in (r, r + 1):
                pts.append(jnp.where(sub == isb[rr:rr + 1, :], wb[rr:rr + 1, :], 0.0).astype(BF16))
                qs.append(jnp.where(sub == jsb[rr:rr + 1, :], 1.0, 0.0).T.astype(BF16))
            rhs = jnp.concatenate([jnp.concatenate([qs[0], zero], axis=1),
                                   jnp.concatenate([zero, qs[1]], axis=1)], axis=0)
            g = _dot(jnp.concatenate(pts, axis=1), rhs)
            for k, rr in enumerate((r, r + 1)):
                gb = pltpu.bitcast(g[:, k * PEER_NKEYS:(k + 1) * PEER_NKEYS], jnp.uint32)
                packed = (gb[half:] & hi_mask) | (gb[:half] >> 16)
                dst[pl.ds(pl.multiple_of((base + rr) * W_PITCH, 8), half), :] = packed

    def step(dst, src):
        parts = []
        for ii in range(ib):
            pk = src[pl.ds(e * ib + ii, tt, stride=W_PITCH), :]
            parts.append(pltpu.bitcast(pk << 16, F32).astype(BF16))
            parts.append(pltpu.bitcast(pk & hi_mask, F32).astype(BF16))
        lhs = jnp.concatenate(parts, axis=1)
        nslice = acc_ref.shape[1] // TOEP
        per_slice = per_step // nslice
        dep = None
        for c in range(nslice):
            cols = slice(c * TOEP, (c + 1) * TOEP)
            res = acc_ref[:, cols] + _dot(lhs, v_ref[:, cols])
            acc_ref[:, cols] = res
            for g0 in range(c * per_slice, (c + 1) * per_slice, 8):
                scatter(dst, g0, dep)
            dep = pltpu.bitcast(res[0:8, 0:LANES], I32)

    @pl.when((i == 0) & (e == 0))
    def _():
        wd1_ref[...] = jnp.zeros_like(wd1_ref)

    @pl.when(e == 0)
    def _():
        acc_ref[...] = jnp.zeros_like(acc_ref)

    @pl.when(i % 2 == 0)
    def _():
        step(wd0_ref, wd1_ref)

    @pl.when(i % 2 == 1)
    def _():
        step(wd1_ref, wd0_ref)

    @pl.when(e == ne - 1)
    def _():
        o_ref[...] = _layer_norm(ALPHA * h2_ref[...] + acc_ref[...], lg_ref[...], lb_ref[...])


def _peer_down(isel, jsel, w, vperm, h2, lg, lb, tt, ib):
    t = h2.shape[0]
    ne = (PEER_NKEYS // 2) // ib
    ntile = t // tt
    vb = ib * 2 * PEER_NKEYS
    assert tt % (8 * ne) == 0
    sel = pl.BlockSpec((tt, N_SEL), lambda i, e: (jnp.minimum(i, ntile - 1), 0))
    tok = pl.BlockSpec((tt, D_MODEL), lambda i, e: (jnp.maximum(i - 1, 0), 0))
    full = lambda a: pl.BlockSpec(a.shape, lambda i, e: (0,) * a.ndim)
    wd = pltpu.VMEM((tt * W_PITCH, N_SEL), jnp.uint32)
    return pl.pallas_call(
        functools.partial(_peer_down_kernel, ib=ib, ne=ne),
        grid=(ntile + 1, ne),
        in_specs=[sel, sel, sel, pl.BlockSpec((vb, D_MODEL), lambda i, e: (e, 0)), tok, full(lg), full(lb)],
        out_specs=tok,
        out_shape=jax.ShapeDtypeStruct((t, D_MODEL), F32),
        scratch_shapes=[wd, wd, pltpu.VMEM((tt, D_MODEL), F32)],
        compiler_params=_cparams(("arbitrary", "arbitrary")),
        name="peer_down",
    )(isel, jsel, w, vperm, h2, lg, lb)


def _rope_tables(seq):
    half = QK_ROPE_DIM // 2
    inv = 1.0 / (ROPE_THETA ** (jnp.arange(0, QK_ROPE_DIM, 2, dtype=F32) / QK_ROPE_DIM))
    ang = jnp.arange(seq, dtype=F32)[:, None] * inv[None, :]
    cos, sin = jnp.cos(ang), jnp.sin(ang)
    c32 = jnp.concatenate([cos, cos], axis=1)
    s32 = jnp.concatenate([-sin, sin], axis=1)
    z = lambda w: jnp.zeros((seq, w), F32)
    scale = (QK_NOPE_DIM + QK_ROPE_DIM) ** -0.5 * math.log2(math.e)
    pad = HEAD_SLOT - QK_NOPE_DIM - QK_ROPE_DIM
    cq = jnp.concatenate([jnp.full((seq, QK_NOPE_DIM), scale, F32), scale * c32, z(pad)], axis=1)
    sq = jnp.concatenate([z(QK_NOPE_DIM), scale * s32, z(pad)], axis=1)
    ck = jnp.concatenate([z(QK_NOPE_DIM), c32, z(pad)], axis=1)
    sk = jnp.concatenate([z(QK_NOPE_DIM), s32, z(pad)], axis=1)
    return cq, sq, ck, sk


def _swap_halves(w):
    half = w.shape[-1] // 2
    return jnp.concatenate([w[..., half:], w[..., :half]], axis=-1)


def _slot(w, offset):
    return jnp.pad(w, ((0, 0), (offset, HEAD_SLOT - offset - w.shape[1])))


def _position_features(seq, pos):
    pos = pos.astype(F32)
    t = (pos / (seq - 1))[:, None]
    bands = (HY_EMB_DIM - 1) // 2
    w = 2.0 * math.pi * pos / seq
    f = jnp.linspace(1e-4, bands - 1, bands, dtype=F32)
    ang = w[:, None] * f[None, :]
    return jnp.concatenate([t, jnp.cos(ang), -jnp.sin(ang)], axis=-1)


def kernel(x, emb_ln_g, emb_ln_b, w_in, q_norm_g, w_uq, kv_norm_g, w_ukv, hy_short_w, hy_short_b, hy_filt_w1,
           hy_filt_b1, hy_filt_freq1, hy_filt_w2, hy_filt_b2, hy_filt_freq2, hy_filt_w3, hy_bias, attn_out_g,
           hy_out_g, w_o, ln_mix_g, ln_mix_b, peer_wq, peer_sub_keys, peer_u, peer_v, ln_ffn_g, ln_ffn_b):
    batch, seq, _ = x.shape
    assert w_in.shape[0] == DEPTH == 1 and seq % TOEP == 0
    t = batch * seq
    nblk = seq // TOEP
    r2 = lambda a: a.reshape(1, -1)
    x2 = x.reshape(t, D_MODEL)
    lg, lb = r2(emb_ln_g), r2(emb_ln_b)

    wi = w_in[0]
    w_kr = wi[:, OFF_CKV:OFF_KR]
    wall = jnp.concatenate([wi[:, :OFF_CKV], _slot(w_kr, QK_NOPE_DIM), _slot(_swap_halves(w_kr), QK_NOPE_DIM),
                            wi[:, OFF_KR:]], axis=1).astype(BF16)
    dq = QK_NOPE_DIM + QK_ROPE_DIM
    wuq = w_uq[0].reshape(Q_LORA_RANK, ATTN_HEADS, dq)
    wq = jnp.pad(wuq, ((0, 0), (0, 0), (0, HEAD_SLOT - dq))).reshape(Q_LORA_RANK, -1).astype(BF16)
    wqs = jnp.pad(_swap_halves(wuq[..., QK_NOPE_DIM:]),
                  ((0, 0), (0, 0), (QK_NOPE_DIM, HEAD_SLOT - dq))).reshape(Q_LORA_RANK, -1).astype(BF16)
    wukv = w_ukv[0].reshape(KV_LORA_RANK, ATTN_HEADS, QK_NOPE_DIM + V_HEAD_DIM)
    wk = jnp.pad(wukv[..., :QK_NOPE_DIM],
                 ((0, 0), (0, 0), (0, HEAD_SLOT - QK_NOPE_DIM))).reshape(KV_LORA_RANK, -1).astype(BF16)
    wv = wukv[..., QK_NOPE_DIM:].reshape(KV_LORA_RANK, -1).T.astype(BF16)
    cq, sq, ck, sk = _rope_tables(seq)

    q, k, vt, ut = _ln_proj(x2, lg, lb, wall, r2(q_norm_g[0]), wq, wqs, r2(kv_norm_g[0]), wk, wv,
                           hy_short_w[0], r2(hy_short_b[0]), cq, sq, ck, sk, batch, seq)

    a = _attention(q, k, vt, attn_out_g[0].reshape(-1, 1), batch, seq, min(512, seq))

    ut = ut.reshape((HY_ORDER + 1) * HY_WIDTH, nblk * batch, TOEP)

    lag = jnp.abs(jnp.arange(2 * seq) - seq)
    feats = _position_features(seq, jnp.minimum(lag, seq - 1))
    zt = jnp.pad(feats.T, ((0, HY_FILTER_HIDDEN - HY_EMB_DIM), (0, 0)))
    col = lambda a_: a_.reshape(-1, 1)
    w1t = jnp.pad(hy_filt_w1[0].T, ((0, 0), (0, HY_FILTER_HIDDEN - HY_EMB_DIM)))
    w3t = hy_filt_w3[0].T.reshape(HY_ORDER, 2, HY_WIDTH, HY_FILTER_HIDDEN)
    deltas = jnp.abs(jnp.linspace(math.log(HY_TARGET) / HY_SLOW_DECAY, math.log(HY_TARGET) / HY_FAST_DECAY,
                                  HY_WIDTH, dtype=F32))
    g = _filters(zt, w1t, col(hy_filt_b1[0]), col(hy_filt_freq1[0]), hy_filt_w2[0].T, col(hy_filt_b2[0]),
                 col(hy_filt_freq2[0]), w3t, col(deltas))
    yt = _hyena(hy_bias[0], ut, g.reshape(HY_ORDER, HY_WIDTH, 1, 2 * seq), batch, nblk, 8)

    h2, h2b, qt = _mix(x2, lg, lb, a, yt.reshape(HY_WIDTH, nblk * batch * TOEP), r2(hy_out_g[0]),
                       w_o[0].astype(BF16), r2(ln_mix_g[0]), r2(ln_mix_b[0]), peer_wq[0].T.astype(BF16), batch, nblk)

    keys = peer_sub_keys[0].astype(BF16).reshape(N_KEYSETS, PEER_NKEYS, PEER_DK // 2)
    w, isel, jsel = _peer_up(h2b, peer_u[0].astype(BF16), qt, keys, min(1024, t), 1024)
    ib = 8
    half = PEER_NKEYS // 2
    vperm = peer_v[0].astype(BF16).reshape(2, half // ib, ib, PEER_NKEYS, D_MODEL)
    vperm = vperm.transpose(1, 2, 0, 3, 4).reshape(PEER_NKEYS * PEER_NKEYS, D_MODEL)
    out = _peer_down(isel, jsel, w, vperm, h2, r2(ln_ffn_g[0]), r2(ln_ffn_b[0]), min(512, t), ib)
    return out.reshape(batch, seq, D_MODEL)
```
